```python
import jax, jax.numpy as jnp
from jax import lax
import numpy as np

D_MODEL = 1024
BATCH = 8
SEQ = 4096
DEPTH = 1
DEC_BATCH = 8
DEC_SEQ = 32
PAST_LEN = 4096

CHUNK = 64
N_META = 16
Q_BLOCK = 128
HEAD_DIM = 64
SB_HEADS = 8
DSA_HEADS = 8
DSA_KV_HEADS = 2
DSA_GROUP = DSA_HEADS // DSA_KV_HEADS
IDX_HEADS = 8
IDX_DIM = 64
TOPK_MAX = 256
ROPE_THETA = 10000.0
N_GROUPS = 4
EXPERTS_PER_GROUP = 8
N_EXPERTS = N_GROUPS * EXPERTS_PER_GROUP
EXPERT_TOPK = 2
EXPERT_DIM = 256
EPS = 1e-6

SB_WIDTH = SB_HEADS * HEAD_DIM
DSA_WIDTH = DSA_HEADS * HEAD_DIM
DSA_KV_WIDTH = DSA_KV_HEADS * HEAD_DIM
SB_SCALE = HEAD_DIM ** -0.5
ATTN_SCALE = HEAD_DIM ** -0.5
IDX_SCALE = IDX_DIM ** -0.5
IDX_W_SCALE = IDX_HEADS ** -0.5
IN_SPLIT_SIZES = (SB_WIDTH, SB_WIDTH, SB_WIDTH, DSA_WIDTH, DSA_KV_WIDTH, DSA_KV_WIDTH,
                  IDX_HEADS * IDX_DIM, IDX_DIM, IDX_HEADS, D_MODEL, D_MODEL)
IN_SPLITS = tuple(sum(IN_SPLIT_SIZES[:i + 1]) for i in range(len(IN_SPLIT_SIZES) - 1))
IN_COLS = sum(IN_SPLIT_SIZES)

kernel_name = 'chunk_stream_sb_dsa_hmoe_step'


def rms_norm(x, g):
    xf = x.astype(jnp.float32)
    y = xf * lax.rsqrt(jnp.mean(xf * xf, axis=-1, keepdims=True) + EPS)
    return (y * g.astype(jnp.float32)).astype(x.dtype)


def rope(x, pos):
    half = x.shape[-1] // 2
    freqs = ROPE_THETA ** (-jnp.arange(half, dtype=jnp.float32) / half)
    ang = pos.astype(jnp.float32)[:, None] * freqs[None, :]
    cos = jnp.cos(ang)[None, :, None, :]
    sin = jnp.sin(ang)[None, :, None, :]
    xf = x.astype(jnp.float32)
    x1, x2 = xf[..., :half], xf[..., half:]
    return jnp.concatenate([x1 * cos - x2 * sin, x2 * cos + x1 * sin], axis=-1).astype(x.dtype)


def prompt_chunk_id(pos):
    return jnp.where(pos < N_META, 0, (pos - N_META) // CHUNK + 1)


def project(x, pos, norm1_g, w_in, q_norm_g, k_norm_g):
    B, L, _ = x.shape
    xn = rms_norm(x, norm1_g)
    sq, sk, sv, dq, dk, dv, iq, ik, iw, g_sb, g_dsa = jnp.split(xn @ w_in, IN_SPLITS, axis=-1)
    sq = sq.reshape(B, L, SB_HEADS, HEAD_DIM)
    sk = sk.reshape(B, L, SB_HEADS, HEAD_DIM)
    sv = sv.reshape(B, L, SB_HEADS, HEAD_DIM)
    dq = rope(rms_norm(dq.reshape(B, L, DSA_HEADS, HEAD_DIM), q_norm_g), pos)
    dk = rope(rms_norm(dk.reshape(B, L, DSA_KV_HEADS, HEAD_DIM), k_norm_g), pos)
    dv = dv.reshape(B, L, DSA_KV_HEADS, HEAD_DIM)
    iq = rope(iq.reshape(B, L, IDX_HEADS, IDX_DIM), pos)
    ik = rope(ik[:, :, None, :], pos)[:, :, 0, :]
    return (sq, sk, sv, dq, dk, dv, iq, ik, iw, jax.nn.sigmoid(g_sb), jax.nn.sigmoid(g_dsa))


def sb_attend(q, k, v, q_pos, k_pos):
    z = jnp.einsum('bqhd,bkhd->bhqk', q, k).astype(jnp.float32) * SB_SCALE
    visible = (k_pos[None, :] < q_pos[:, None])[None, None]
    log_skip = jnp.where(visible, jax.nn.log_sigmoid(-z), 0.0)
    between = lax.cumsum(log_skip, axis=3, reverse=True) - log_skip
    weight = jnp.where(visible, jnp.exp(jax.nn.log_sigmoid(z) + between), 0.0)
    return jnp.einsum('bhqk,bkhd->bqhd', weight.astype(v.dtype), v)


def dsa_attend(q, iq, iw, q_chunk, k, v, ik, k_chunk, n_sel):
    B, Tq = q.shape[:2]
    iscore = jnp.einsum('bqhd,bkd->bqhk', iq, ik).astype(jnp.float32) * IDX_SCALE
    score = jnp.einsum('bqhk,bqh->bqk', jax.nn.relu(iscore), iw.astype(jnp.float32)) * IDX_W_SCALE
    admissible = k_chunk[None, :] <= q_chunk[:, None]
    score = jnp.where(admissible[None], score, -jnp.inf)
    _, sel = lax.top_k(score, n_sel)
    sel_ok = k_chunk[sel] <= q_chunk[None, :, None]
    k_sel = jax.vmap(lambda kb, ib: kb[ib])(k, sel)
    v_sel = jax.vmap(lambda vb, ib: vb[ib])(v, sel)
    qg = q.reshape(B, Tq, DSA_KV_HEADS, DSA_GROUP, HEAD_DIM)
    logits = jnp.einsum('bqhgd,bqkhd->bqhgk', qg, k_sel).astype(jnp.float32) * ATTN_SCALE
    logits = jnp.where(sel_ok[:, :, None, None, :], logits, -jnp.inf)
    p = jax.nn.softmax(logits, axis=-1).astype(v.dtype)
    out = jnp.einsum('bqhgk,bqkhd->bqhgd', p, v_sel)
    return out.reshape(B, Tq, DSA_HEADS, HEAD_DIM)


def pad_seq(a, length):
    return jnp.pad(a, [(0, 0), (0, length - a.shape[1])] + [(0, 0)] * (a.ndim - 2))


def to_blocks(a):
    B, Lp = a.shape[:2]
    return jnp.moveaxis(a.reshape((B, Lp // Q_BLOCK, Q_BLOCK) + a.shape[2:]), 1, 0)


def from_blocks(a):
    a = jnp.moveaxis(a, 0, 1)
    return a.reshape((a.shape[0], -1) + a.shape[3:])


def hier_moe(h, w_group, w_router, w_gate, w_up, w_down):
    B, L, D = h.shape
    xt = h.reshape(-1, D)
    group_probs = jax.nn.softmax((xt @ w_group).astype(jnp.float32), axis=-1)
    gp, gi = lax.top_k(group_probs, 1)
    elog = (xt @ w_router).astype(jnp.float32).reshape(-1, N_GROUPS, EXPERTS_PER_GROUP)
    elog = jnp.take_along_axis(elog, gi[:, :, None], axis=1)[:, 0]
    ev, ei = lax.top_k(elog, EXPERT_TOPK)
    ew = jax.nn.softmax(ev, axis=-1) * gp
    gid = gi * EXPERTS_PER_GROUP + ei
    comb = jnp.sum(jax.nn.one_hot(gid, N_EXPERTS, dtype=jnp.float32) * ew[..., None], axis=1).astype(h.dtype)
    y = jnp.zeros_like(xt)
    for g in range(N_GROUPS):
        sl = slice(g * EXPERTS_PER_GROUP, (g + 1) * EXPERTS_PER_GROUP)
        a = jnp.einsum('nd,edf->nef', xt, w_gate[sl])
        b = jnp.einsum('nd,edf->nef', xt, w_up[sl])
        hh = jax.nn.silu(a) * b * comb[:, sl, None]
        y = y + jnp.einsum('nef,efd->nd', hh, w_down[sl])
    return y.reshape(B, L, D)


def layer_tail(x, o_sb, o_dsa, g_sb, g_dsa, w_sb_branch, w_dsa_branch, w_out, norm2_g,
               w_group, w_router, w_gate, w_up, w_down):
    B, L, _ = x.shape
    merged = (g_sb * (o_sb.reshape(B, L, SB_WIDTH) @ w_sb_branch)
              + g_dsa * (o_dsa.reshape(B, L, DSA_WIDTH) @ w_dsa_branch))
    h = x + merged @ w_out
    return h + hier_moe(rms_norm(h, norm2_g), w_group, w_router, w_gate, w_up, w_down)


def setup_inputs(seed: int = 0) -> dict:
    key = jax.random.key(seed)
    ks = jax.random.split(key, 24)
    f32 = jnp.float32
    nrm = lambda k, shape, scale: scale * jax.random.normal(k, shape, f32)
    return {
        'x_prompt': nrm(ks[0], (BATCH, SEQ, D_MODEL), 1.0),
        'x_sample': nrm(ks[1], (DEC_BATCH, DEC_SEQ, D_MODEL), 1.0),
        'cache_sb_k': nrm(ks[2], (DEPTH, DEC_BATCH, PAST_LEN, SB_HEADS, HEAD_DIM), 1.0),
        'cache_sb_v': nrm(ks[3], (DEPTH, DEC_BATCH, PAST_LEN, SB_HEADS, HEAD_DIM), 1.0),
        'cache_dsa_k': nrm(ks[4], (DEPTH, DEC_BATCH, PAST_LEN, DSA_KV_HEADS, HEAD_DIM), 1.0),
        'cache_dsa_v': nrm(ks[5], (DEPTH, DEC_BATCH, PAST_LEN, DSA_KV_HEADS, HEAD_DIM), 1.0),
        'cache_idx_k': nrm(ks[6], (DEPTH, DEC_BATCH, PAST_LEN, IDX_DIM), 1.0),
        'meta_tokens': nrm(ks[7], (N_META, D_MODEL), 1.0),
        'norm1_g': 1.0 + nrm(ks[8], (DEPTH, D_MODEL), 0.02),
        'w_in': nrm(ks[9], (DEPTH, D_MODEL, IN_COLS), D_MODEL ** -0.5),
        'dsa_q_norm_g': 1.0 + nrm(ks[10], (DEPTH, HEAD_DIM), 0.02),
        'dsa_k_norm_g': 1.0 + nrm(ks[11], (DEPTH, HEAD_DIM), 0.02),
        'w_sb_branch': nrm(ks[12], (DEPTH, SB_WIDTH, D_MODEL), SB_WIDTH ** -0.5),
        'w_dsa_branch': nrm(ks[13], (DEPTH, DSA_WIDTH, D_MODEL), DSA_WIDTH ** -0.5),
        'w_out': nrm(ks[14], (DEPTH, D_MODEL, D_MODEL), D_MODEL ** -0.5),
        'norm2_g': 1.0 + nrm(ks[15], (DEPTH, D_MODEL), 0.02),
        'w_group': nrm(ks[16], (DEPTH, D_MODEL, N_GROUPS), D_MODEL ** -0.5),
        'w_router': nrm(ks[17], (DEPTH, D_MODEL, N_EXPERTS), D_MODEL ** -0.5),
        'w_gate': nrm(ks[18], (DEPTH, N_EXPERTS, D_MODEL, EXPERT_DIM), D_MODEL ** -0.5),
        'w_up': nrm(ks[19], (DEPTH, N_EXPERTS, D_MODEL, EXPERT_DIM), D_MODEL ** -0.5),
        'w_down': nrm(ks[20], (DEPTH, N_EXPERTS, EXPERT_DIM, D_MODEL), EXPERT_DIM ** -0.5),
    }


def reference(x_prompt, x_sample, cache_sb_k, cache_sb_v, cache_dsa_k, cache_dsa_v, cache_idx_k,
              meta_tokens, norm1_g, w_in, dsa_q_norm_g, dsa_k_norm_g, w_sb_branch, w_dsa_branch,
              w_out, norm2_g, w_group, w_router, w_gate, w_up, w_down):
    B = x_prompt.shape[0]
    meta = jnp.broadcast_to(meta_tokens[None].astype(x_prompt.dtype), (B, N_META, D_MODEL))
    xp = jnp.concatenate([meta, x_prompt], axis=1)
    Lr = xp.shape[1]
    Lp = -(-Lr // Q_BLOCK) * Q_BLOCK
    pos_p = jnp.arange(Lp, dtype=jnp.int32)
    chunk_p = prompt_chunk_id(pos_p)
    n_sel_p = min(TOPK_MAX, x_prompt.shape[1] // 4)
    xs = x_sample
    T = xs.shape[1]
    pos_s = PAST_LEN + jnp.arange(T, dtype=jnp.int32)
    chunk_s = pos_s // CHUNK
    kpos_s = jnp.arange(PAST_LEN + T, dtype=jnp.int32)
    kchunk_s = kpos_s // CHUNK
    n_sel_s = min(TOPK_MAX, (PAST_LEN + T) // 4)

    p_sbk, p_sbv, p_dk, p_dv, p_ik = [], [], [], [], []
    s_sbk, s_sbv, s_dk, s_dv, s_ik = [], [], [], [], []
    for layer in range(DEPTH):
        proj_w = (norm1_g[layer], w_in[layer], dsa_q_norm_g[layer], dsa_k_norm_g[layer])
        tail_w = (w_sb_branch[layer], w_dsa_branch[layer], w_out[layer], norm2_g[layer],
                  w_group[layer], w_router[layer], w_gate[layer], w_up[layer], w_down[layer])

        sq, sk, sv, dq, dk, dv, iq, ik, iw, g_sb, g_dsa = project(xp, pos_p[:Lr], *proj_w)
        skp, svp, dkp, dvp, ikp = (pad_seq(a, Lp) for a in (sk, sv, dk, dv, ik))
        o_sb = from_blocks(lax.map(
            lambda blk: sb_attend(blk[0], skp, svp, blk[1], pos_p),
            (to_blocks(pad_seq(sq, Lp)), pos_p.reshape(-1, Q_BLOCK))))[:, :Lr]
        o_dsa = from_blocks(lax.map(
            lambda blk: dsa_attend(blk[0], blk[1], blk[2], blk[3], dkp, dvp, ikp, chunk_p, n_sel_p),
            (to_blocks(pad_seq(dq, Lp)), to_blocks(pad_seq(iq, Lp)), to_blocks(pad_seq(iw, Lp)),
             chunk_p.reshape(-1, Q_BLOCK))))[:, :Lr]
        xp = layer_tail(xp, o_sb, o_dsa, g_sb, g_dsa, *tail_w)
        p_sbk.append(sk); p_sbv.append(sv); p_dk.append(dk); p_dv.append(dv); p_ik.append(ik)

        dt = xs.dtype
        tq, tk, tv, tdq, tdk, tdv, tiq, tik, tiw, tg_sb, tg_dsa = project(xs, pos_s, *proj_w)
        k_sb = jnp.concatenate([cache_sb_k[layer].astype(dt), tk], axis=1)
        v_sb = jnp.concatenate([cache_sb_v[layer].astype(dt), tv], axis=1)
        k_ds = jnp.concatenate([cache_dsa_k[layer].astype(dt), tdk], axis=1)
        v_ds = jnp.concatenate([cache_dsa_v[layer].astype(dt), tdv], axis=1)
        k_ix = jnp.concatenate([cache_idx_k[layer].astype(dt), tik], axis=1)
        o_sb_s = sb_attend(tq, k_sb, v_sb, pos_s, kpos_s)
        o_dsa_s = dsa_attend(tdq, tiq, tiw, chunk_s, k_ds, v_ds, k_ix, kchunk_s, n_sel_s)
        xs = layer_tail(xs, o_sb_s, o_dsa_s, tg_sb, tg_dsa, *tail_w)
        s_sbk.append(tk); s_sbv.append(tv); s_dk.append(tdk); s_dv.append(tdv); s_ik.append(tik)

    y_prompt = xp[:, N_META:]
    y_sample = xs
    return (y_prompt, y_sample,
            jnp.stack(p_sbk), jnp.stack(p_sbv), jnp.stack(p_dk), jnp.stack(p_dv), jnp.stack(p_ik),
            jnp.stack(s_sbk), jnp.stack(s_sbv), jnp.stack(s_dk), jnp.stack(s_dv), jnp.stack(s_ik))
```

```python
import functools

import jax
import jax.numpy as jnp
from jax import lax
from jax.experimental import pallas as pl
from jax.experimental.pallas import tpu as pltpu

F32 = jnp.float32
BF16 = jnp.bfloat16

CHUNK = 64
TOPK_MAX = 256
ROPE_THETA = 10000.0
EPS = 1e-6
HEAD_DIM = 64
IDX_HEADS = 8
EXPERT_TOPK = 2

LANES = 128
BLK = LANES
VMEM_LIMIT = 56 * 1024 * 1024
INT_MIN = -(2 ** 31)
NEG_BIG = -1e30
SB_UNDERFLOW = -104.0


def _cparams(sem):
    return pltpu.CompilerParams(dimension_semantics=sem, vmem_limit_bytes=VMEM_LIMIT)


def _dot(a, b):
    return jnp.dot(a, b, preferred_element_type=F32)


def _dot_nt(a, b):
    return lax.dot_general(a, b, (((1,), (1,)), ((), ())), preferred_element_type=F32)


def _split_bf16(x):
    hi = x.astype(BF16)
    lo = (x - hi.astype(F32)).astype(BF16)
    return hi, lo


def _proj_kernel(x_ref, g1_ref, wa_ref, wg_ref, wiwt_ref, wdvt_ref, cos_ref, sin_ref,
                 qg_ref, kg_ref, gmat_ref,
                 sqb_ref, skf_ref, svf_ref, skb_ref, svb_ref, dqb_ref, dkf_ref, dvf_ref,
                 dkb_ref, dvt_ref, iqb_ref, ikf_ref, ikb_ref, iwt_ref, gsb_ref, gdsa_ref,
                 *, sb_w, dsa_w, kv_w, idx_w):
    x = x_ref[...]
    ms = jnp.mean(x * x, axis=-1, keepdims=True)
    xn = x * lax.rsqrt(ms + EPS) * g1_ref[...]
    xb = xn.astype(BF16)
    cos = cos_ref[...]
    sin = sin_ref[...]
    gmat = gmat_ref[...]
    lane = lax.broadcasted_iota(jnp.int32, cos.shape, 1)
    first_half = (lane % HEAD_DIM) < (HEAD_DIM // 2)

    def mm(lo, width):
        return _dot(xb, wa_ref[:, lo:lo + width])

    def head_norm(y, gain):
        hi, lo = _split_bf16(y * y)
        m = _dot(hi, gmat) + _dot(lo, gmat)
        return y * lax.rsqrt(m + EPS) * gain

    def rope(y):
        swapped = jnp.where(first_half, pltpu.roll(y, LANES - HEAD_DIM // 2, 1),
                            pltpu.roll(y, HEAD_DIM // 2, 1))
        return y * cos + swapped * sin

    scale = HEAD_DIM ** -0.5
    off = 0
    sqb_ref[...] = (mm(off, sb_w) * scale).astype(BF16)
    off += sb_w
    sk = mm(off, sb_w)
    skf_ref[...] = sk
    skb_ref[...] = sk.astype(BF16)
    off += sb_w
    sv = mm(off, sb_w)
    svf_ref[...] = sv
    svb_ref[...] = sv.astype(BF16)
    off += sb_w
    for c in range(dsa_w // LANES):
        y = rope(head_norm(mm(off + c * LANES, LANES), qg_ref[...]))
        dqb_ref[:, c * LANES:(c + 1) * LANES] = (y * scale).astype(BF16)
    off += dsa_w
    for c in range(kv_w // LANES):
        y = rope(head_norm(mm(off + c * LANES, LANES), kg_ref[...]))
        dkf_ref[:, c * LANES:(c + 1) * LANES] = y
        dkb_ref[:, c * LANES:(c + 1) * LANES] = y.astype(BF16)
    off += kv_w
    dvf_ref[...] = mm(off, kv_w)
    off += kv_w
    for c in range(idx_w // LANES):
        y = rope(mm(off + c * LANES, LANES))
        iqb_ref[:, c * LANES:(c + 1) * LANES] = (y * scale).astype(BF16)
    off += idx_w
    y = rope(mm(off, LANES))
    ikf_ref[...] = y[:, :HEAD_DIM]
    ikb_ref[...] = y.astype(BF16)
    iwt_ref[...] = _dot_nt(wiwt_ref[...], xb)
    for r in range(dvt_ref.shape[0]):
        dvt_ref[r] = _dot_nt(wdvt_ref[...], xb[r * BLK:(r + 1) * BLK]).astype(BF16)
    d = gsb_ref.shape[-1]
    gsb_ref[...] = jax.nn.sigmoid(_dot(xb, wg_ref[:, :d]))
    gdsa_ref[...] = jax.nn.sigmoid(_dot(xb, wg_ref[:, d:]))


def _rope_tables(pos):
    half = HEAD_DIM // 2
    freqs = ROPE_THETA ** (-jnp.arange(half, dtype=F32) / half)
    ang = pos.astype(F32)[:, None] * freqs[None, :]
    cos, sin = jnp.cos(ang), jnp.sin(ang)
    cos_t = jnp.tile(cos, (1, LANES // half))
    sin_t = jnp.tile(jnp.concatenate([-sin, sin], axis=1), (1, LANES // HEAD_DIM))
    return cos_t, sin_t


def _project(xpad, pos, w, dims):
    B, L, D = xpad.shape
    sb_w, dsa_w, kv_w, idx_w = dims
    tm = 256 if L % 256 == 0 else BLK
    cos_t, sin_t = _rope_tables(pos)
    grid = (B, L // tm)
    row = lambda width: pl.BlockSpec((None, tm, width), lambda b, i: (b, i, 0))
    full = lambda a: pl.BlockSpec(a.shape, lambda b, i: (0,) * a.ndim)
    tab = pl.BlockSpec((tm, LANES), lambda b, i: (i, 0))
    out_shapes = dict(
        sqb=(BF16, sb_w), skf=(F32, sb_w), svf=(F32, sb_w), skb=(BF16, sb_w), svb=(BF16, sb_w),
        dqb=(BF16, dsa_w), dkf=(F32, kv_w), dvf=(F32, kv_w), dkb=(BF16, kv_w))
    names = ["sqb", "skf", "svf", "skb", "svb", "dqb", "dkf", "dvf", "dkb", "dvt", "iqb", "ikf",
             "ikb", "iwt", "gsb", "gdsa"]
    shapes, specs = [], []
    for n in names:
        if n in out_shapes:
            dt, width = out_shapes[n]
        elif n == "dvt":
            shapes.append(jax.ShapeDtypeStruct((B, L // BLK, kv_w, BLK), BF16))
            specs.append(pl.BlockSpec((None, tm // BLK, kv_w, BLK), lambda b, i: (b, i, 0, 0)))
            continue
        elif n == "iqb":
            dt, width = BF16, idx_w
        elif n == "ikf":
            dt, width = F32, HEAD_DIM
        elif n == "ikb":
            dt, width = BF16, LANES
        elif n == "iwt":
            shapes.append(jax.ShapeDtypeStruct((B, IDX_HEADS, L), F32))
            specs.append(pl.BlockSpec((None, IDX_HEADS, tm), lambda b, i: (b, 0, i)))
            continue
        else:
            dt, width = F32, D
        shapes.append(jax.ShapeDtypeStruct((B, L, width), dt))
        specs.append(row(width))
    ins = [xpad, w["g1"], w["wa"], w["wg"], w["wiwt"], w["wdvt"], cos_t, sin_t, w["qg"], w["kg"], w["gmat"]]
    in_specs = [row(D), full(w["g1"]), full(w["wa"]), full(w["wg"]), full(w["wiwt"]), full(w["wdvt"]),
                tab, tab, full(w["qg"]), full(w["kg"]), full(w["gmat"])]
    outs = pl.pallas_call(
        functools.partial(_proj_kernel, sb_w=sb_w, dsa_w=dsa_w, kv_w=kv_w, idx_w=idx_w),
        grid=grid, in_specs=in_specs, out_specs=specs, out_shape=shapes,
        compiler_params=_cparams(("parallel", "parallel")), name="proj",
    )(*ins)
    return dict(zip(names, outs))


def _sb_kernel(q_ref, k_ref, v_ref, uo_ref, o_ref, *, q0):
    qi = q0 + pl.program_id(2)
    q = q_ref[...]
    uo = uo_ref[...]
    row = lax.broadcasted_iota(jnp.int32, (BLK, BLK), 0)
    col = lax.broadcasted_iota(jnp.int32, (BLK, BLK), 1)
    causal = col < row
    acc = jnp.zeros((BLK, LANES), F32)

    for hh in range(LANES // HEAD_DIM):
        hs = slice(hh * HEAD_DIM, (hh + 1) * HEAD_DIM)
        qh = q[:, hs]
        head_lanes = (col // HEAD_DIM) == hh

        def tile(j, c, acc, masked):
            start = pl.multiple_of(j * BLK, BLK)
            kj = k_ref[pl.ds(start, BLK), hs]
            vj = v_ref[pl.ds(start, BLK), :]
            vj = jnp.where(head_lanes, vj, jnp.zeros_like(vj))
            z = _dot_nt(qh, kj)
            soft = jnp.log1p(jnp.exp(-jnp.abs(z)))
            ls_pos = jnp.minimum(z, 0.0) - soft
            ls_neg = jnp.minimum(-z, 0.0) - soft
            if masked:
                ls_neg = jnp.where(causal, ls_neg, 0.0)
            hi, lo = _split_bf16(ls_neg)
            r = _dot(hi, uo) + _dot(lo, uo)
            wgt = jnp.exp(ls_pos + r[:, :BLK] + c)
            if masked:
                wgt = jnp.where(causal, wgt, 0.0)
            acc = acc + _dot(wgt.astype(BF16), vj)
            return c + r[:, BLK:], acc

        c, acc = tile(qi, jnp.zeros((BLK, BLK), F32), acc, True)

        def cond(carry):
            j, _, _, cmax = carry
            return jnp.logical_and(j >= 0, cmax > SB_UNDERFLOW)

        def body(carry):
            j, c, acc, _ = carry
            c, acc = tile(j, c, acc, False)
            return j - 1, c, acc, jnp.max(c)

        _, _, acc, _ = lax.while_loop(cond, body, (qi - 1, c, acc, jnp.max(c)))

    o_ref[...] = acc.astype(o_ref.dtype)


def _sb_attention(q, k, v, q0):
    B, Lq, W = q.shape
    Lk = k.shape[1]
    uo = jnp.concatenate([jnp.tril(jnp.ones((BLK, BLK), F32), -1), jnp.ones((BLK, BLK), F32)], axis=1).astype(BF16)
    grid = (B, W // LANES, Lq // BLK)
    return pl.pallas_call(
        functools.partial(_sb_kernel, q0=q0),
        grid=grid,
        in_specs=[pl.BlockSpec((None, BLK, LANES), lambda b, h, i: (b, i, h)),
                  pl.BlockSpec((None, Lk, LANES), lambda b, h, i: (b, 0, h)),
                  pl.BlockSpec((None, Lk, LANES), lambda b, h, i: (b, 0, h)),
                  pl.BlockSpec(uo.shape, lambda b, h, i: (0, 0))],
        out_specs=pl.BlockSpec((None, BLK, LANES), lambda b, h, i: (b, i, h)),
        out_shape=jax.ShapeDtypeStruct((B, Lq, W), BF16),
        compiler_params=_cparams(("parallel", "parallel", "arbitrary")), name="sb_attn",
    )(q, k, v, uo)


def _dsa_kernel(q_ref, iq_ref, iwt_ref, k_ref, vt_ref, ik_ref, lt_ref, o_ref,
                keys_ref, m_ref, l_ref, acc_ref, *, q0, nkb_total, chunk_off, n_valid, n_sel, n_heads, kv_heads):
    qi = q0 + pl.program_id(1)
    nkb = jnp.minimum(qi + 2, nkb_total)
    group = n_heads // kv_heads
    idx_w_scale = IDX_HEADS ** -0.5

    def chunk_of(pos):
        return jnp.where(pos < chunk_off, 0, (pos - chunk_off) // CHUNK + 1)

    qpos = qi * BLK + lax.broadcasted_iota(jnp.int32, (1, BLK), 1)
    qchunk = chunk_of(qpos)
    iq = iq_ref[...]
    iwt = iwt_ref[...]

    def score_block(j, _):
        start = pl.multiple_of(j * BLK, BLK)
        ikj = ik_ref[pl.ds(start, BLK), :][:, :HEAD_DIM]
        acc = jnp.zeros((BLK, BLK), F32)
        for h in range(IDX_HEADS):
            s = _dot_nt(ikj, iq[:, h * HEAD_DIM:(h + 1) * HEAD_DIM])
            acc = acc + jnp.maximum(s, 0.0) * iwt[h:h + 1, :]
        score = acc * idx_w_scale
        kpos = start + lax.broadcasted_iota(jnp.int32, (BLK, 1), 0)
        admissible = jnp.logical_and(chunk_of(kpos) <= qchunk, kpos < n_valid)
        bits = lax.bitcast_convert_type(score, jnp.int32)
        key = bits ^ ((bits >> 31) & 0x7FFFFFFF)
        keys_ref[pl.ds(start, BLK), :] = jnp.where(admissible, key, INT_MIN)
        return 0

    lax.fori_loop(0, nkb, score_block, 0)

    def count_ge(cand):
        def body(j, cnt):
            start = pl.multiple_of(j * BLK, BLK)
            return cnt + jnp.where(keys_ref[pl.ds(start, BLK), :] >= cand, 1.0, 0.0)
        cnt = lax.fori_loop(0, nkb, body, jnp.zeros((BLK, BLK), F32))
        return jnp.sum(cnt, axis=0, keepdims=True)

    def search(it, thr):
        cand = thr + jnp.left_shift(jnp.int32(1), 31 - it)
        return jnp.where(count_ge(cand) >= n_sel, cand, thr)

    thr = lax.fori_loop(0, 32, search, jnp.full((1, BLK), INT_MIN, jnp.int32))
    room = n_sel - count_ge(thr + 1)

    m_ref[...] = jnp.full(m_ref.shape, NEG_BIG, F32)
    l_ref[...] = jnp.zeros(l_ref.shape, F32)
    acc_ref[...] = jnp.zeros(acc_ref.shape, F32)
    q = q_ref[...]
    lt = lt_ref[...]

    def attend_block(j, eq_seen):
        start = pl.multiple_of(j * BLK, BLK)
        kk = keys_ref[pl.ds(start, BLK), :]
        eq = kk == thr
        prefix = _dot(lt, jnp.where(eq, 1.0, 0.0).astype(BF16)) + eq_seen
        sel = jnp.logical_or(kk > thr, jnp.logical_and(eq, prefix <= room))
        sel = jnp.logical_and(sel, kk != INT_MIN)
        kj = k_ref[pl.ds(start, BLK), :]
        vtj = vt_ref[j]
        for kvh in range(kv_heads):
            kh = kj[:, kvh * HEAD_DIM:(kvh + 1) * HEAD_DIM]
            vth = vtj[kvh * HEAD_DIM:(kvh + 1) * HEAD_DIM, :]
            for g in range(group):
                h = kvh * group + g
                s = _dot_nt(kh, q[:, h * HEAD_DIM:(h + 1) * HEAD_DIM])
                s = jnp.where(sel, s, NEG_BIG)
                m_old = m_ref[h:h + 1, :]
                m_new = jnp.maximum(m_old, jnp.max(s, axis=0, keepdims=True))
                p = jnp.where(sel, jnp.exp(s - m_new), 0.0)
                alpha = jnp.exp(m_old - m_new)
                l_ref[h:h + 1, :] = alpha * l_ref[h:h + 1, :] + jnp.sum(p, axis=0, keepdims=True)
                hs = slice(h * HEAD_DIM, (h + 1) * HEAD_DIM)
                acc_ref[hs, :] = alpha * acc_ref[hs, :] + _dot(vth, p.astype(BF16))
                m_ref[h:h + 1, :] = m_new
        return prefix[BLK - 1:BLK, :]

    lax.fori_loop(0, nkb, attend_block, jnp.zeros((1, BLK), F32))

    for h in range(n_heads):
        hs = slice(h * HEAD_DIM, (h + 1) * HEAD_DIM)
        acc_ref[hs, :] = acc_ref[hs, :] / l_ref[h:h + 1, :]
    o_ref[...] = acc_ref[...].T.astype(o_ref.dtype)


def _dsa_attention(q, iq, iwt, k, vt, ik, *, q0, chunk_off, n_valid, n_sel, kv_heads):
    B, Lq, W = q.shape
    Lk = k.shape[1]
    n_heads = W // HEAD_DIM
    lt = jnp.tril(jnp.ones((BLK, BLK), F32)).astype(BF16)
    kern = functools.partial(_dsa_kernel, q0=q0, nkb_total=Lk // BLK, chunk_off=chunk_off, n_valid=n_valid,
                             n_sel=n_sel, n_heads=n_heads, kv_heads=kv_heads)
    return pl.pallas_call(
        kern,
        grid=(B, Lq // BLK),
        in_specs=[pl.BlockSpec((None, BLK, W), lambda b, i: (b, i, 0)),
                  pl.BlockSpec((None, BLK, iq.shape[2]), lambda b, i: (b, i, 0)),
                  pl.BlockSpec((None, IDX_HEADS, BLK), lambda b, i: (b, 0, i)),
                  pl.BlockSpec((None, Lk, k.shape[2]), lambda b, i: (b, 0, 0)),
                  pl.BlockSpec((None,) + vt.shape[1:], lambda b, i: (b, 0, 0, 0)),
                  pl.BlockSpec((None, Lk, ik.shape[2]), lambda b, i: (b, 0, 0)),
                  pl.BlockSpec(lt.shape, lambda b, i: (0, 0))],
        out_specs=pl.BlockSpec((None, BLK, W), lambda b, i: (b, i, 0)),
        out_shape=jax.ShapeDtypeStruct((B, Lq, W), BF16),
        scratch_shapes=[pltpu.VMEM((Lk, BLK), jnp.int32), pltpu.VMEM((n_heads, BLK), F32),
                        pltpu.VMEM((n_heads, BLK), F32), pltpu.VMEM((W, BLK), F32)],
        compiler_params=_cparams(("parallel", "arbitrary")), name="dsa_attn",
    )(q, iq, iwt, k, vt, ik, lt)


def _merge_kernel(x_ref, osb_ref, odsa_ref, gsb_ref, gdsa_ref, wsb_ref, wdsa_ref, wout_ref, g2_ref,
                  wr_hi_ref, wr_lo_ref, h_ref, hn_ref, comb_ref, *, n_groups, per_group):
    merged = gsb_ref[...] * _dot(osb_ref[...], wsb_ref[...]) + gdsa_ref[...] * _dot(odsa_ref[...], wdsa_ref[...])
    h = x_ref[...] + _dot(merged.astype(BF16), wout_ref[...])
    h_ref[...] = h
    ms = jnp.mean(h * h, axis=-1, keepdims=True)
    hn = h * lax.rsqrt(ms + EPS) * g2_ref[...]
    hn_ref[...] = hn.astype(BF16)

    hi, lo = _split_bf16(hn)
    logits = _dot(hi, wr_hi_ref[...]) + _dot(lo, wr_hi_ref[...]) + _dot(hi, wr_lo_ref[...])
    lane = lax.broadcasted_iota(jnp.int32, logits.shape, 1)
    big = jnp.int32(LANES)
    neg_inf = -jnp.inf

    def first_argmax(vals):
        top = jnp.max(vals, axis=-1, keepdims=True)
        idx = jnp.min(jnp.where(vals == top, lane, big), axis=-1, keepdims=True)
        return top, idx

    is_group = lane < n_groups
    gl = jnp.where(is_group, logits, neg_inf)
    ge = jnp.exp(gl - jnp.max(gl, axis=-1, keepdims=True))
    probs = jnp.where(is_group, ge / jnp.sum(ge, axis=-1, keepdims=True), neg_inf)
    gp, gi = first_argmax(probs)

    expert = lane - n_groups
    in_group = jnp.logical_and(expert >= gi * per_group, expert < (gi + 1) * per_group)
    vals = jnp.where(in_group, logits, neg_inf)
    ev0, i0 = first_argmax(vals)
    ev1, i1 = first_argmax(jnp.where(lane == i0, neg_inf, vals))
    e1 = jnp.exp(ev1 - ev0)
    w0 = gp / (1.0 + e1)
    w1 = gp * e1 / (1.0 + e1)
    comb_ref[...] = jnp.where(lane == i0, w0, jnp.where(lane == i1, w1, 0.0))


def _merge(x, osb, odsa, gsb, gdsa, w):
    N, D = x.shape
    tm = next(t for t in (512, 256, 128, 64, 32, 16) if N % t == 0)
    row = lambda a: pl.BlockSpec((tm, a.shape[1]), lambda i: (i, 0))
    full = lambda a: pl.BlockSpec(a.shape, lambda i: (0,) * a.ndim)
    ins = [x, osb, odsa, gsb, gdsa, w["wsb"], w["wdsa"], w["wout"], w["g2"], w["wr_hi"], w["wr_lo"]]
    in_specs = [row(a) for a in ins[:5]] + [full(a) for a in ins[5:]]
    return pl.pallas_call(
        functools.partial(_merge_kernel, n_groups=w["n_groups"], per_group=w["per_group"]),
        grid=(N // tm,), in_specs=in_specs,
        out_specs=[pl.BlockSpec((tm, D), lambda i: (i, 0)), pl.BlockSpec((tm, D), lambda i: (i, 0)),
                   pl.BlockSpec((tm, LANES), lambda i: (i, 0))],
        out_shape=[jax.ShapeDtypeStruct((N, D), F32), jax.ShapeDtypeStruct((N, D), BF16),
                   jax.ShapeDtypeStruct((N, LANES), F32)],
        compiler_params=_cparams(("parallel",)), name="merge_route",
    )(*ins)


def _moe_kernel(h_ref, hn_ref, comb_ref, wg_ref, wu_ref, wd_ref, y_ref, *, n_groups):
    e = pl.program_id(1)

    @pl.when(e == 0)
    def _():
        y_ref[...] = h_ref[...]

    hn = hn_ref[...]
    comb = comb_ref[...]
    lane = lax.broadcasted_iota(jnp.int32, comb.shape, 1)
    ce = jnp.sum(jnp.where(lane == e + n_groups, comb, 0.0), axis=-1, keepdims=True)
    a = _dot(hn, wg_ref[...])
    b = _dot(hn, wu_ref[...])
    hh = a * jax.nn.sigmoid(a) * b * ce
    y_ref[...] += _dot(hh.astype(BF16), wd_ref[...])


def _moe(h, hn, comb, w):
    N, D = h.shape
    E, _, Fd = w["wgate"].shape
    tm = next(t for t in (1024, 512, 256, 128, 64, 32, 16) if N % t == 0)
    return pl.pallas_call(
        functools.partial(_moe_kernel, n_groups=w["n_groups"]),
        grid=(N // tm, E),
        in_specs=[pl.BlockSpec((tm, D), lambda i, e: (i, 0)),
                  pl.BlockSpec((tm, D), lambda i, e: (i, 0)),
                  pl.BlockSpec((tm, LANES), lambda i, e: (i, 0)),
                  pl.BlockSpec((None, D, Fd), lambda i, e: (e, 0, 0)),
                  pl.BlockSpec((None, D, Fd), lambda i, e: (e, 0, 0)),
                  pl.BlockSpec((None, Fd, D), lambda i, e: (e, 0, 0))],
        out_specs=pl.BlockSpec((tm, D), lambda i, e: (i, 0)),
        out_shape=jax.ShapeDtypeStruct((N, D), F32),
        compiler_params=_cparams(("parallel", "arbitrary")), name="moe_experts",
    )(h, hn, comb, w["wgate"], w["wup"], w["wdown"])


def _prep_weights(norm1_g, w_in, qg, kg, w_sbb, w_dsab, w_out, norm2_g, w_group, w_router, w_gate, w_up, w_down,
                  dims):
    sb_w, dsa_w, kv_w, idx_w = dims
    D = w_in.shape[0]
    o_dv = 3 * sb_w + dsa_w + kv_w
    o_ik = o_dv + kv_w + idx_w
    o_iw = o_ik + HEAD_DIM
    o_g = o_iw + IDX_HEADS
    w_ik = w_in[:, o_ik:o_iw]
    n_groups = w_group.shape[1]
    n_experts = w_router.shape[1]
    assert n_groups + n_experts <= LANES
    wr = jnp.concatenate([w_group, w_router, jnp.zeros((D, LANES - n_groups - n_experts), F32)], axis=1)
    wr_hi, wr_lo = _split_bf16(wr)
    tile2 = lambda g: jnp.tile(g.astype(F32), LANES // HEAD_DIM)[None, :]
    head_id = jnp.arange(LANES) // HEAD_DIM
    return dict(
        g1=norm1_g.astype(F32)[None, :],
        wa=jnp.concatenate([w_in[:, :o_iw], w_ik], axis=1).astype(BF16),
        wg=w_in[:, o_g:].astype(BF16),
        wiwt=w_in[:, o_iw:o_g].T.astype(BF16),
        wdvt=w_in[:, o_dv:o_dv + kv_w].T.astype(BF16),
        qg=tile2(qg), kg=tile2(kg),
        gmat=((head_id[:, None] == head_id[None, :]).astype(F32) / HEAD_DIM).astype(BF16),
        wsb=w_sbb.astype(BF16), wdsa=w_dsab.astype(BF16), wout=w_out.astype(BF16),
        g2=norm2_g.astype(F32)[None, :], wr_hi=wr_hi, wr_lo=wr_lo,
        wgate=w_gate.astype(BF16), wup=w_up.astype(BF16), wdown=w_down.astype(BF16),
        n_groups=n_groups, per_group=n_experts // n_groups)


def _tail(x2d, osb, odsa, gsb, gdsa, w):
    h, hn, comb = _merge(x2d, osb, odsa, gsb, gdsa, w)
    return _moe(h, hn, comb, w)


def kernel(x_prompt, x_sample, cache_sb_k, cache_sb_v, cache_dsa_k, cache_dsa_v, cache_idx_k, meta_tokens, norm1_g, w_in, dsa_q_norm_g, dsa_k_norm_g, w_sb_branch, w_dsa_branch, w_out, norm2_g, w_group, w_router, w_gate, w_up, w_down):
    B, S, D = x_prompt.shape
    Bs, T, _ = x_sample.shape
    depth, _, past, sb_heads, hd = cache_sb_k.shape
    kv_heads = cache_dsa_k.shape[3]
    n_meta = meta_tokens.shape[0]
    assert depth == 1 and hd == HEAD_DIM and cache_idx_k.shape[-1] == HEAD_DIM
    assert past % BLK == 0 and T <= BLK and S % CHUNK == 0
    sb_w = sb_heads * HEAD_DIM
    dsa_w = w_dsa_branch.shape[1]
    kv_w = kv_heads * HEAD_DIM
    idx_w = IDX_HEADS * HEAD_DIM
    assert kv_w == LANES and sb_w % LANES == 0 and dsa_w % LANES == 0
    dims = (sb_w, dsa_w, kv_w, idx_w)
    w = _prep_weights(norm1_g[0], w_in[0], dsa_q_norm_g[0], dsa_k_norm_g[0], w_sb_branch[0], w_dsa_branch[0],
                      w_out[0], norm2_g[0], w_group[0], w_router[0], w_gate[0], w_up[0], w_down[0], dims)

    Lr = n_meta + S
    Lp = -(-Lr // BLK) * BLK
    meta = jnp.broadcast_to(meta_tokens[None].astype(x_prompt.dtype), (B, n_meta, D))
    xp = jnp.concatenate([meta, x_prompt, jnp.zeros((B, Lp - Lr, D), x_prompt.dtype)], axis=1)
    pr = _project(xp, jnp.arange(Lp, dtype=jnp.int32), w, dims)
    o_sb = _sb_attention(pr["sqb"], pr["skb"], pr["svb"], 0)
    o_dsa = _dsa_attention(pr["dqb"], pr["iqb"], pr["iwt"], pr["dkb"], pr["dvt"], pr["ikb"], q0=0,
                           chunk_off=n_meta, n_valid=Lr, n_sel=min(TOPK_MAX, S // 4), kv_heads=kv_heads)
    flat = lambda a: a.reshape(B * Lp, a.shape[-1])
    yp = _tail(flat(xp), flat(o_sb), flat(o_dsa), flat(pr["gsb"]), flat(pr["gdsa"]), w)
    y_prompt = yp.reshape(B, Lp, D)[:, n_meta:Lr]

    xs = jnp.concatenate([x_sample, jnp.zeros((Bs, BLK - T, D), x_sample.dtype)], axis=1)
    sr = _project(xs, past + jnp.arange(BLK, dtype=jnp.int32), w, dims)
    cat = lambda c, new: jnp.concatenate([c.astype(BF16), new], axis=1)
    k_sb = cat(cache_sb_k[0].reshape(Bs, past, sb_w), sr["skb"])
    v_sb = cat(cache_sb_v[0].reshape(Bs, past, sb_w), sr["svb"])
    k_ds = cat(cache_dsa_k[0].reshape(Bs, past, kv_w), sr["dkb"])
    k_ix = cat(jnp.tile(cache_idx_k[0], (1, 1, LANES // HEAD_DIM)), sr["ikb"])
    vt_c = cache_dsa_v[0].reshape(Bs, past // BLK, BLK, kv_w).transpose(0, 1, 3, 2).astype(BF16)
    vt_ds = jnp.concatenate([vt_c, sr["dvt"]], axis=1)
    qb = past // BLK
    o_sb_s = _sb_attention(sr["sqb"], k_sb, v_sb, qb)
    o_dsa_s = _dsa_attention(sr["dqb"], sr["iqb"], sr["iwt"], k_ds, vt_ds, k_ix, q0=qb, chunk_off=0,
                             n_valid=past + T, n_sel=min(TOPK_MAX, (past + T) // 4), kv_heads=kv_heads)
    real = lambda a: a[:, :T].reshape(Bs * T, a.shape[-1])
    ys = _tail(real(xs), real(o_sb_s), real(o_dsa_s), real(sr["gsb"]), real(sr["gdsa"]), w)
    y_sample = ys.reshape(Bs, T, D)

    heads = lambda a, L, n: a[:, :L].reshape(1, a.shape[0], L, n, HEAD_DIM)
    idx = lambda a, L: a[:, :L][None]
    return (y_prompt, y_sample,
            heads(pr["skf"], Lr, sb_heads), heads(pr["svf"], Lr, sb_heads),
            heads(pr["dkf"], Lr, kv_heads), heads(pr["dvf"], Lr, kv_heads), idx(pr["ikf"], Lr),
            heads(sr["skf"], T, sb_heads), heads(sr["svf"], T, sb_heads),
            heads(sr["dkf"], T, kv_heads), heads(sr["dvf"], T, kv_heads), idx(sr["ikf"], T))
```

```python
import functools

import jax
import jax.numpy as jnp
from jax import lax
from jax.experimental import pallas as pl
from jax.experimental.pallas import tpu as pltpu

F32 = jnp.float32
BF16 = jnp.bfloat16

CHUNK = 64
TOPK_MAX = 256
ROPE_THETA = 10000.0
EPS = 1e-6
HEAD_DIM = 64
IDX_HEADS = 8
EXPERT_TOPK = 2

LANES = 128
BLK = LANES
SLAB = 2 * BLK
VMEM_LIMIT = 56 * 1024 * 1024
INT_MIN = -(2 ** 31)
NEG_BIG = -1e30
SB_UNDERFLOW = -104.0


def _cparams(sem):
    return pltpu.CompilerParams(dimension_semantics=sem, vmem_limit_bytes=VMEM_LIMIT)


def _dot(a, b):
    return jnp.dot(a, b, preferred_element_type=F32)


def _dot_nt(a, b):
    return lax.dot_general(a, b, (((1,), (1,)), ((), ())), preferred_element_type=F32)


def _split_bf16(x):
    hi = x.astype(BF16)
    lo = (x - hi.astype(F32)).astype(BF16)
    return hi, lo


def _proj_kernel(x_ref, g1_ref, wa_ref, wg_ref, wiwt_ref, wdvt_ref, cos_ref, sin_ref,
                 qg_ref, kg_ref, gmat_ref,
                 sqb_ref, skf_ref, svf_ref, skb_ref, svb_ref, dqb_ref, dkf_ref, dvf_ref,
                 dkb_ref, dvt_ref, iqb_ref, ikf_ref, ikb_ref, iwt_ref, gsb_ref, gdsa_ref,
                 *, sb_w, dsa_w, kv_w, idx_w):
    x = x_ref[...]
    ms = jnp.mean(x * x, axis=-1, keepdims=True)
    xn = x * lax.rsqrt(ms + EPS) * g1_ref[...]
    xb = xn.astype(BF16)
    cos = cos_ref[...]
    sin = sin_ref[...]
    gmat = gmat_ref[...]
    lane = lax.broadcasted_iota(jnp.int32, cos.shape, 1)
    first_half = (lane % HEAD_DIM) < (HEAD_DIM // 2)

    def mm(lo, width):
        return _dot(xb, wa_ref[:, lo:lo + width])

    def head_norm(y, gain):
        hi, lo = _split_bf16(y * y)
        m = _dot(hi, gmat) + _dot(lo, gmat)
        return y * lax.rsqrt(m + EPS) * gain

    def rope(y):
        swapped = jnp.where(first_half, pltpu.roll(y, LANES - HEAD_DIM // 2, 1),
                            pltpu.roll(y, HEAD_DIM // 2, 1))
        return y * cos + swapped * sin

    scale = HEAD_DIM ** -0.5
    off = 0
    sqb_ref[...] = (mm(off, sb_w) * scale).astype(BF16)
    off += sb_w
    sk = mm(off, sb_w)
    skf_ref[...] = sk
    skb_ref[...] = sk.astype(BF16)
    off += sb_w
    sv = mm(off, sb_w)
    svf_ref[...] = sv
    svb_ref[...] = sv.astype(BF16)
    off += sb_w
    for c in range(dsa_w // LANES):
        y = rope(head_norm(mm(off + c * LANES, LANES), qg_ref[...]))
        dqb_ref[:, c * LANES:(c + 1) * LANES] = (y * scale).astype(BF16)
    off += dsa_w
    for c in range(kv_w // LANES):
        y = rope(head_norm(mm(off + c * LANES, LANES), kg_ref[...]))
        dkf_ref[:, c * LANES:(c + 1) * LANES] = y
        dkb_ref[:, c * LANES:(c + 1) * LANES] = y.astype(BF16)
    off += kv_w
    dvf_ref[...] = mm(off, kv_w)
    off += kv_w
    for c in range(idx_w // LANES):
        y = rope(mm(off + c * LANES, LANES))
        iqb_ref[:, c * LANES:(c + 1) * LANES] = (y * scale).astype(BF16)
    off += idx_w
    y = rope(mm(off, LANES))
    ikf_ref[...] = y[:, :HEAD_DIM]
    ikb_ref[...] = y.astype(BF16)
    iwt_ref[...] = _dot_nt(wiwt_ref[...], xb)
    dvt_ref[...] = _dot_nt(wdvt_ref[...], xb).astype(BF16)
    d = gsb_ref.shape[-1]
    gsb_ref[...] = jax.nn.sigmoid(_dot(xb, wg_ref[:, :d]))
    gdsa_ref[...] = jax.nn.sigmoid(_dot(xb, wg_ref[:, d:]))


def _rope_tables(pos):
    half = HEAD_DIM // 2
    freqs = ROPE_THETA ** (-jnp.arange(half, dtype=F32) / half)
    ang = pos.astype(F32)[:, None] * freqs[None, :]
    cos, sin = jnp.cos(ang), jnp.sin(ang)
    cos_t = jnp.tile(cos, (1, LANES // half))
    sin_t = jnp.tile(jnp.concatenate([-sin, sin], axis=1), (1, LANES // HEAD_DIM))
    return cos_t, sin_t


def _project(xpad, pos, w, dims):
    B, L, D = xpad.shape
    sb_w, dsa_w, kv_w, idx_w = dims
    tm = SLAB
    cos_t, sin_t = _rope_tables(pos)
    grid = (B, L // tm)
    row = lambda width: pl.BlockSpec((None, tm, width), lambda b, i: (b, i, 0))
    full = lambda a: pl.BlockSpec(a.shape, lambda b, i: (0,) * a.ndim)
    tab = pl.BlockSpec((tm, LANES), lambda b, i: (i, 0))
    out_shapes = dict(
        sqb=(BF16, sb_w), skf=(F32, sb_w), svf=(F32, sb_w), skb=(BF16, sb_w), svb=(BF16, sb_w),
        dqb=(BF16, dsa_w), dkf=(F32, kv_w), dvf=(F32, kv_w), dkb=(BF16, kv_w))
    names = ["sqb", "skf", "svf", "skb", "svb", "dqb", "dkf", "dvf", "dkb", "dvt", "iqb", "ikf",
             "ikb", "iwt", "gsb", "gdsa"]
    shapes, specs = [], []
    for n in names:
        if n in out_shapes:
            dt, width = out_shapes[n]
        elif n == "dvt":
            shapes.append(jax.ShapeDtypeStruct((B, L // SLAB, kv_w, SLAB), BF16))
            specs.append(pl.BlockSpec((None, None, kv_w, SLAB), lambda b, i: (b, i, 0, 0)))
            continue
        elif n == "iqb":
            dt, width = BF16, idx_w
        elif n == "ikf":
            dt, width = F32, HEAD_DIM
        elif n == "ikb":
            dt, width = BF16, LANES
        elif n == "iwt":
            shapes.append(jax.ShapeDtypeStruct((B, IDX_HEADS, L), F32))
            specs.append(pl.BlockSpec((None, IDX_HEADS, tm), lambda b, i: (b, 0, i)))
            continue
        else:
            dt, width = F32, D
        shapes.append(jax.ShapeDtypeStruct((B, L, width), dt))
        specs.append(row(width))
    ins = [xpad, w["g1"], w["wa"], w["wg"], w["wiwt"], w["wdvt"], cos_t, sin_t, w["qg"], w["kg"], w["gmat"]]
    in_specs = [row(D), full(w["g1"]), full(w["wa"]), full(w["wg"]), full(w["wiwt"]), full(w["wdvt"]),
                tab, tab, full(w["qg"]), full(w["kg"]), full(w["gmat"])]
    outs = pl.pallas_call(
        functools.partial(_proj_kernel, sb_w=sb_w, dsa_w=dsa_w, kv_w=kv_w, idx_w=idx_w),
        grid=grid, in_specs=in_specs, out_specs=specs, out_shape=shapes,
        compiler_params=_cparams(("parallel", "parallel")), name="proj",
    )(*ins)
    return dict(zip(names, outs))


def _sb_kernel(q_ref, k_ref, v_ref, uo_ref, o_ref, *, q0):
    qi = q0 + pl.program_id(2)
    q = q_ref[...]
    uo = uo_ref[...]
    n_pair = LANES // HEAD_DIM
    row = lax.broadcasted_iota(jnp.int32, (BLK, SLAB), 0)
    col = lax.broadcasted_iota(jnp.int32, (BLK, SLAB), 1)
    dcol = col - row
    v_lane_head = lax.broadcasted_iota(jnp.int32, (SLAB, LANES), 1) // HEAD_DIM

    def body(carry):
        j, cs, acc, _ = carry
        jlo = jnp.maximum(j - 1, 0)
        start = pl.multiple_of(jlo * BLK, BLK)
        kslab = k_ref[pl.ds(start, SLAB), :]
        vslab = v_ref[pl.ds(start, SLAB), :]
        vis = jnp.logical_and(dcol < (qi - jlo) * BLK, col < (j - jlo + 1) * BLK)
        cs_out = []
        for hh in range(n_pair):
            hs = slice(hh * HEAD_DIM, (hh + 1) * HEAD_DIM)
            z = _dot_nt(q[:, hs], kslab[:, hs])
            soft = jnp.log1p(jnp.exp(-jnp.abs(z)))
            ls_pos = jnp.minimum(z, 0.0) - soft
            ls_neg = jnp.where(vis, jnp.minimum(-z, 0.0) - soft, 0.0)
            hi, lo = _split_bf16(ls_neg)
            r = _dot(hi, uo) + _dot(lo, uo)
            c = cs[hh]
            wgt = jnp.where(vis, jnp.exp(ls_pos + r[:, :SLAB] + jnp.concatenate([c] * (SLAB // BLK), axis=1)), 0.0)
            vh = jnp.where(v_lane_head == hh, vslab, jnp.zeros_like(vslab))
            acc = acc + _dot(wgt.astype(BF16), vh)
            cs_out.append(c + r[:, SLAB:])
        cmax = functools.reduce(jnp.maximum, [jnp.max(c) for c in cs_out])
        return j - SLAB // BLK, tuple(cs_out), acc, cmax

    def cond(carry):
        j, _, _, cmax = carry
        return jnp.logical_and(j >= 0, cmax > SB_UNDERFLOW)

    init = (qi, (jnp.zeros((BLK, BLK), F32),) * n_pair, jnp.zeros((BLK, LANES), F32), jnp.float32(0.0))
    _, _, acc, _ = lax.while_loop(cond, body, init)
    o_ref[...] = acc.astype(o_ref.dtype)


def _sb_attention(q, k, v, q0):
    B, Lq, W = q.shape
    Lk = k.shape[1]
    uo = jnp.concatenate([jnp.tril(jnp.ones((SLAB, SLAB), F32), -1), jnp.ones((SLAB, BLK), F32)], axis=1).astype(BF16)
    grid = (B, W // LANES, Lq // BLK)
    return pl.pallas_call(
        functools.partial(_sb_kernel, q0=q0),
        grid=grid,
        in_specs=[pl.BlockSpec((None, BLK, LANES), lambda b, h, i: (b, i, h)),
                  pl.BlockSpec((None, Lk, LANES), lambda b, h, i: (b, 0, h)),
                  pl.BlockSpec((None, Lk, LANES), lambda b, h, i: (b, 0, h)),
                  pl.BlockSpec(uo.shape, lambda b, h, i: (0, 0))],
        out_specs=pl.BlockSpec((None, BLK, LANES), lambda b, h, i: (b, i, h)),
        out_shape=jax.ShapeDtypeStruct((B, Lq, W), BF16),
        compiler_params=_cparams(("parallel", "parallel", "arbitrary")), name="sb_attn",
    )(q, k, v, uo)


def _dsa_kernel(q_ref, iq_ref, iwt_ref, k_ref, vt_ref, ik_ref, lt_ref, o_ref,
                keys_ref, acc_ref, *, q0, nslab_total, chunk_off, n_valid, n_sel, n_heads, kv_heads):
    qi = q0 + pl.program_id(1)
    nslab = jnp.minimum((qi + 2 + SLAB // BLK - 1) // (SLAB // BLK), nslab_total)
    group = n_heads // kv_heads
    idx_w_scale = IDX_HEADS ** -0.5

    qpos = qi * BLK + lax.broadcasted_iota(jnp.int32, (1, BLK), 1)
    chunk_end = chunk_off + CHUNK * (jnp.right_shift(qpos - chunk_off, CHUNK.bit_length() - 1) + 1)
    key_limit = jnp.minimum(jnp.where(qpos < chunk_off, chunk_off, chunk_end), n_valid)
    slab_row = lax.broadcasted_iota(jnp.int32, (SLAB, BLK), 0)

    iwt = iwt_ref[...]
    by_head = lambda x, heads: jnp.concatenate([x[:, h * HEAD_DIM:(h + 1) * HEAD_DIM] for h in heads], axis=0)
    iq_rows = by_head(iq_ref[...], range(IDX_HEADS))

    def score_slab(j, _):
        start = pl.multiple_of(j * SLAB, SLAB)
        s = _dot_nt(ik_ref[pl.ds(start, SLAB), :][:, :HEAD_DIM], iq_rows)
        acc = jnp.zeros((SLAB, BLK), F32)
        for h in range(IDX_HEADS):
            acc = acc + jnp.maximum(s[:, h * BLK:(h + 1) * BLK], 0.0) * iwt[h:h + 1, :]
        bits = lax.bitcast_convert_type(acc * idx_w_scale, jnp.int32)
        key = bits ^ ((bits >> 31) & 0x7FFFFFFF)
        keys_ref[pl.ds(start, SLAB), :] = jnp.where(slab_row < key_limit - start, key, INT_MIN)
        return 0

    lax.fori_loop(0, nslab, score_slab, 0)

    def count_ge(cand):
        def body(j, cnt):
            start = pl.multiple_of(j * SLAB, SLAB)
            return cnt + jnp.where(keys_ref[pl.ds(start, SLAB), :] >= cand, 1.0, 0.0)
        cnt = lax.fori_loop(0, nslab, body, jnp.zeros((SLAB, BLK), F32))
        return jnp.sum(cnt, axis=0, keepdims=True)

    def search(it, thr):
        cand = thr + jnp.left_shift(jnp.int32(1), 31 - it)
        return jnp.where(count_ge(cand) >= n_sel, cand, thr)

    thr = lax.fori_loop(0, 32, search, jnp.full((1, BLK), INT_MIN, jnp.int32))
    room = n_sel - count_ge(thr + 1)

    acc_ref[...] = jnp.zeros(acc_ref.shape, F32)
    q = q_ref[...]
    q_rows = [by_head(q, range(kvh * group, (kvh + 1) * group)) for kvh in range(kv_heads)]
    lt = lt_ref[...]

    def attend_slab(j, carry):
        eq_seen, m_all, l_all = carry
        start = pl.multiple_of(j * SLAB, SLAB)
        kk = keys_ref[pl.ds(start, SLAB), :]
        eq = kk == thr
        prefix = _dot(lt, jnp.where(eq, 1.0, 0.0).astype(BF16)) + eq_seen
        sel = jnp.logical_or(kk > thr, jnp.logical_and(eq, prefix <= room))
        sel = jnp.logical_and(sel, kk != INT_MIN)
        kj = k_ref[pl.ds(start, SLAB), :]
        vtj = vt_ref[j]
        m_out, l_out = [], []
        for kvh in range(kv_heads):
            s_all = _dot_nt(kj[:, kvh * HEAD_DIM:(kvh + 1) * HEAD_DIM], q_rows[kvh])
            ps, alphas = [], []
            for g in range(group):
                h = kvh * group + g
                s = jnp.where(sel, s_all[:, g * BLK:(g + 1) * BLK], NEG_BIG)
                m_new = jnp.maximum(m_all[h], jnp.max(s, axis=0, keepdims=True))
                p = jnp.where(sel, jnp.exp(s - m_new), 0.0)
                alpha = jnp.exp(m_all[h] - m_new)
                l_out.append(alpha * l_all[h] + jnp.sum(p, axis=0, keepdims=True))
                m_out.append(m_new)
                ps.append(p.astype(BF16))
                alphas.append(alpha)
            pv = _dot(vtj[kvh * HEAD_DIM:(kvh + 1) * HEAD_DIM, :], jnp.concatenate(ps, axis=1))
            acc_ref[kvh] = jnp.concatenate(alphas, axis=1) * acc_ref[kvh] + pv
        return prefix[SLAB - 1:SLAB, :], tuple(m_out), tuple(l_out)

    init = (jnp.zeros((1, BLK), F32), (jnp.full((1, BLK), NEG_BIG, F32),) * n_heads,
            (jnp.zeros((1, BLK), F32),) * n_heads)
    _, _, l_all = lax.fori_loop(0, nslab, attend_slab, init)

    outs = []
    for kvh in range(kv_heads):
        acc = acc_ref[kvh]
        for g in range(group):
            outs.append(acc[:, g * BLK:(g + 1) * BLK] / l_all[kvh * group + g])
    o_ref[...] = jnp.concatenate(outs, axis=0).T.astype(o_ref.dtype)


def _dsa_attention(q, iq, iwt, k, vt, ik, *, q0, chunk_off, n_valid, n_sel, kv_heads):
    B, Lq, W = q.shape
    Lk = k.shape[1]
    n_heads = W // HEAD_DIM
    lt = jnp.tril(jnp.ones((SLAB, SLAB), F32)).astype(BF16)
    kern = functools.partial(_dsa_kernel, q0=q0, nslab_total=Lk // SLAB, chunk_off=chunk_off, n_valid=n_valid,
                             n_sel=n_sel, n_heads=n_heads, kv_heads=kv_heads)
    return pl.pallas_call(
        kern,
        grid=(B, Lq // BLK),
        in_specs=[pl.BlockSpec((None, BLK, W), lambda b, i: (b, i, 0)),
                  pl.BlockSpec((None, BLK, iq.shape[2]), lambda b, i: (b, i, 0)),
                  pl.BlockSpec((None, IDX_HEADS, BLK), lambda b, i: (b, 0, i)),
                  pl.BlockSpec((None, Lk, k.shape[2]), lambda b, i: (b, 0, 0)),
                  pl.BlockSpec((None,) + vt.shape[1:], lambda b, i: (b, 0, 0, 0)),
                  pl.BlockSpec((None, Lk, ik.shape[2]), lambda b, i: (b, 0, 0)),
                  pl.BlockSpec(lt.shape, lambda b, i: (0, 0))],
        out_specs=pl.BlockSpec((None, BLK, W), lambda b, i: (b, i, 0)),
        out_shape=jax.ShapeDtypeStruct((B, Lq, W), BF16),
        scratch_shapes=[pltpu.VMEM((Lk, BLK), jnp.int32),
                        pltpu.VMEM((kv_heads, HEAD_DIM, (n_heads // kv_heads) * BLK), F32)],
        compiler_params=_cparams(("parallel", "arbitrary")), name="dsa_attn",
    )(q, iq, iwt, k, vt, ik, lt)


def _merge_kernel(x_ref, osb_ref, odsa_ref, gsb_ref, gdsa_ref, wsb_ref, wdsa_ref, wout_ref, g2_ref,
                  wr_hi_ref, wr_lo_ref, h_ref, hn_ref, comb_ref, *, n_groups, per_group):
    merged = gsb_ref[...] * _dot(osb_ref[...], wsb_ref[...]) + gdsa_ref[...] * _dot(odsa_ref[...], wdsa_ref[...])
    h = x_ref[...] + _dot(merged.astype(BF16), wout_ref[...])
    h_ref[...] = h
    ms = jnp.mean(h * h, axis=-1, keepdims=True)
    hn = h * lax.rsqrt(ms + EPS) * g2_ref[...]
    hn_ref[...] = hn.astype(BF16)

    hi, lo = _split_bf16(hn)
    logits = _dot(hi, wr_hi_ref[...]) + _dot(lo, wr_hi_ref[...]) + _dot(hi, wr_lo_ref[...])
    lane = lax.broadcasted_iota(jnp.int32, logits.shape, 1)
    big = jnp.int32(LANES)
    neg_inf = -jnp.inf

    def first_argmax(vals):
        top = jnp.max(vals, axis=-1, keepdims=True)
        idx = jnp.min(jnp.where(vals == top, lane, big), axis=-1, keepdims=True)
        return top, idx

    is_group = lane < n_groups
    gl = jnp.where(is_group, logits, neg_inf)
    ge = jnp.exp(gl - jnp.max(gl, axis=-1, keepdims=True))
    probs = jnp.where(is_group, ge / jnp.sum(ge, axis=-1, keepdims=True), neg_inf)
    gp, gi = first_argmax(probs)

    expert = lane - n_groups
    in_group = jnp.logical_and(expert >= gi * per_group, expert < (gi + 1) * per_group)
    vals = jnp.where(in_group, logits, neg_inf)
    ev0, i0 = first_argmax(vals)
    ev1, i1 = first_argmax(jnp.where(lane == i0, neg_inf, vals))
    e1 = jnp.exp(ev1 - ev0)
    w0 = gp / (1.0 + e1)
    w1 = gp * e1 / (1.0 + e1)
    comb_ref[...] = jnp.where(lane == i0, w0, jnp.where(lane == i1, w1, 0.0))


def _merge(x, osb, odsa, gsb, gdsa, w):
    N, D = x.shape
    tm = next(t for t in (512, 256, 128, 64, 32, 16) if N % t == 0)
    row = lambda a: pl.BlockSpec((tm, a.shape[1]), lambda i: (i, 0))
    full = lambda a: pl.BlockSpec(a.shape, lambda i: (0,) * a.ndim)
    ins = [x, osb, odsa, gsb, gdsa, w["wsb"], w["wdsa"], w["wout"], w["g2"], w["wr_hi"], w["wr_lo"]]
    in_specs = [row(a) for a in ins[:5]] + [full(a) for a in ins[5:]]
    return pl.pallas_call(
        functools.partial(_merge_kernel, n_groups=w["n_groups"], per_group=w["per_group"]),
        grid=(N // tm,), in_specs=in_specs,
        out_specs=[pl.BlockSpec((tm, D), lambda i: (i, 0)), pl.BlockSpec((tm, D), lambda i: (i, 0)),
                   pl.BlockSpec((tm, LANES), lambda i: (i, 0))],
        out_shape=[jax.ShapeDtypeStruct((N, D), F32), jax.ShapeDtypeStruct((N, D), BF16),
                   jax.ShapeDtypeStruct((N, LANES), F32)],
        compiler_params=_cparams(("parallel",)), name="merge_route",
    )(*ins)


def _moe_kernel(h_ref, hn_ref, comb_ref, wg_ref, wu_ref, wd_ref, y_ref, *, n_groups):
    e = pl.program_id(1)

    @pl.when(e == 0)
    def _():
        y_ref[...] = h_ref[...]

    hn = hn_ref[...]
    comb = comb_ref[...]
    lane = lax.broadcasted_iota(jnp.int32, comb.shape, 1)
    ce = jnp.sum(jnp.where(lane == e + n_groups, comb, 0.0), axis=-1, keepdims=True)
    a = _dot(hn, wg_ref[...])
    b = _dot(hn, wu_ref[...])
    hh = a * jax.nn.sigmoid(a) * b * ce
    y_ref[...] += _dot(hh.astype(BF16), wd_ref[...])


def _moe(h, hn, comb, w):
    N, D = h.shape
    E, _, Fd = w["wgate"].shape
    tm = next(t for t in (1024, 512, 256, 128, 64, 32, 16) if N % t == 0)
    return pl.pallas_call(
        functools.partial(_moe_kernel, n_groups=w["n_groups"]),
        grid=(N // tm, E),
        in_specs=[pl.BlockSpec((tm, D), lambda i, e: (i, 0)),
                  pl.BlockSpec((tm, D), lambda i, e: (i, 0)),
                  pl.BlockSpec((tm, LANES), lambda i, e: (i, 0)),
                  pl.BlockSpec((None, D, Fd), lambda i, e: (e, 0, 0)),
                  pl.BlockSpec((None, D, Fd), lambda i, e: (e, 0, 0)),
                  pl.BlockSpec((None, Fd, D), lambda i, e: (e, 0, 0))],
        out_specs=pl.BlockSpec((tm, D), lambda i, e: (i, 0)),
        out_shape=jax.ShapeDtypeStruct((N, D), F32),
        compiler_params=_cparams(("parallel", "arbitrary")), name="moe_experts",
    )(h, hn, comb, w["wgate"], w["wup"], w["wdown"])


def _prep_weights(norm1_g, w_in, qg, kg, w_sbb, w_dsab, w_out, norm2_g, w_group, w_router, w_gate, w_up, w_down,
                  dims):
    sb_w, dsa_w, kv_w, idx_w = dims
    D = w_in.shape[0]
    o_dv = 3 * sb_w + dsa_w + kv_w
    o_ik = o_dv + kv_w + idx_w
    o_iw = o_ik + HEAD_DIM
    o_g = o_iw + IDX_HEADS
    w_ik = w_in[:, o_ik:o_iw]
    n_groups = w_group.shape[1]
    n_experts = w_router.shape[1]
    assert n_groups + n_experts <= LANES
    wr = jnp.concatenate([w_group, w_router, jnp.zeros((D, LANES - n_groups - n_experts), F32)], axis=1)
    wr_hi, wr_lo = _split_bf16(wr)
    tile2 = lambda g: jnp.tile(g.astype(F32), LANES // HEAD_DIM)[None, :]
    head_id = jnp.arange(LANES) // HEAD_DIM
    return dict(
        g1=norm1_g.astype(F32)[None, :],
        wa=jnp.concatenate([w_in[:, :o_iw], w_ik], axis=1).astype(BF16),
        wg=w_in[:, o_g:].astype(BF16),
        wiwt=w_in[:, o_iw:o_g].T.astype(BF16),
        wdvt=w_in[:, o_dv:o_dv + kv_w].T.astype(BF16),
        qg=tile2(qg), kg=tile2(kg),
        gmat=((head_id[:, None] == head_id[None, :]).astype(F32) / HEAD_DIM).astype(BF16),
        wsb=w_sbb.astype(BF16), wdsa=w_dsab.astype(BF16), wout=w_out.astype(BF16),
        g2=norm2_g.astype(F32)[None, :], wr_hi=wr_hi, wr_lo=wr_lo,
        wgate=w_gate.astype(BF16), wup=w_up.astype(BF16), wdown=w_down.astype(BF16),
        n_groups=n_groups, per_group=n_experts // n_groups)


def _tail(x2d, osb, odsa, gsb, gdsa, w):
    h, hn, comb = _merge(x2d, osb, odsa, gsb, gdsa, w)
    return _moe(h, hn, comb, w)


def kernel(x_prompt, x_sample, cache_sb_k, cache_sb_v, cache_dsa_k, cache_dsa_v, cache_idx_k, meta_tokens, norm1_g, w_in, dsa_q_norm_g, dsa_k_norm_g, w_sb_branch, w_dsa_branch, w_out, norm2_g, w_group, w_router, w_gate, w_up, w_down):
    B, S, D = x_prompt.shape
    Bs, T, _ = x_sample.shape
    depth, _, past, sb_heads, hd = cache_sb_k.shape
    kv_heads = cache_dsa_k.shape[3]
    n_meta = meta_tokens.shape[0]
    assert depth == 1 and hd == HEAD_DIM and cache_idx_k.shape[-1] == HEAD_DIM
    assert past % SLAB == 0 and T <= BLK and S % CHUNK == 0
    sb_w = sb_heads * HEAD_DIM
    dsa_w = w_dsa_branch.shape[1]
    kv_w = kv_heads * HEAD_DIM
    idx_w = IDX_HEADS * HEAD_DIM
    assert kv_w == LANES and sb_w % LANES == 0 and dsa_w % LANES == 0
    dims = (sb_w, dsa_w, kv_w, idx_w)
    w = _prep_weights(norm1_g[0], w_in[0], dsa_q_norm_g[0], dsa_k_norm_g[0], w_sb_branch[0], w_dsa_branch[0],
                      w_out[0], norm2_g[0], w_group[0], w_router[0], w_gate[0], w_up[0], w_down[0], dims)

    Lr = n_meta + S
    Lp = -(-Lr // SLAB) * SLAB
    meta = jnp.broadcast_to(meta_tokens[None].astype(x_prompt.dtype), (B, n_meta, D))
    xp = jnp.concatenate([meta, x_prompt, jnp.zeros((B, Lp - Lr, D), x_prompt.dtype)], axis=1)
    pr = _project(xp, jnp.arange(Lp, dtype=jnp.int32), w, dims)
    o_sb = _sb_attention(pr["sqb"], pr["skb"], pr["svb"], 0)
    o_dsa = _dsa_attention(pr["dqb"], pr["iqb"], pr["iwt"], pr["dkb"], pr["dvt"], pr["ikb"], q0=0,
                           chunk_off=n_meta, n_valid=Lr, n_sel=min(TOPK_MAX, S // 4), kv_heads=kv_heads)
    flat = lambda a: a.reshape(B * Lp, a.shape[-1])
    yp = _tail(flat(xp), flat(o_sb), flat(o_dsa), flat(pr["gsb"]), flat(pr["gdsa"]), w)
    y_prompt = yp.reshape(B, Lp, D)[:, n_meta:Lr]

    xs = jnp.concatenate([x_sample, jnp.zeros((Bs, SLAB - T, D), x_sample.dtype)], axis=1)
    sr = _project(xs, past + jnp.arange(SLAB, dtype=jnp.int32), w, dims)
    cat = lambda c, new: jnp.concatenate([c.astype(BF16), new], axis=1)
    k_sb = cat(cache_sb_k[0].reshape(Bs, past, sb_w), sr["skb"])
    v_sb = cat(cache_sb_v[0].reshape(Bs, past, sb_w), sr["svb"])
    k_ds = cat(cache_dsa_k[0].reshape(Bs, past, kv_w), sr["dkb"])
    k_ix = cat(jnp.tile(cache_idx_k[0], (1, 1, LANES // HEAD_DIM)), sr["ikb"])
    vt_c = cache_dsa_v[0].reshape(Bs, past // SLAB, SLAB, kv_w).transpose(0, 1, 3, 2).astype(BF16)
    vt_ds = jnp.concatenate([vt_c, sr["dvt"]], axis=1)
    qb = past // BLK
    first = lambda a: a[:, :BLK]
    o_sb_s = _sb_attention(first(sr["sqb"]), k_sb, v_sb, qb)
    o_dsa_s = _dsa_attention(first(sr["dqb"]), first(sr["iqb"]), sr["iwt"][:, :, :BLK], k_ds, vt_ds, k_ix, q0=qb,
                             chunk_off=0, n_valid=past + T, n_sel=min(TOPK_MAX, (past + T) // 4), kv_heads=kv_heads)
    real = lambda a: a[:, :T].reshape(Bs * T, a.shape[-1])
    ys = _tail(real(xs), real(o_sb_s), real(o_dsa_s), real(sr["gsb"]), real(sr["gdsa"]), w)
    y_sample = ys.reshape(Bs, T, D)

    heads = lambda a, L, n: a[:, :L].reshape(1, a.shape[0], L, n, HEAD_DIM)
    idx = lambda a, L: a[:, :L][None]
    return (y_prompt, y_sample,
            heads(pr["skf"], Lr, sb_heads), heads(pr["svf"], Lr, sb_heads),
            heads(pr["dkf"], Lr, kv_heads), heads(pr["dvf"], Lr, kv_heads), idx(pr["ikf"], Lr),
            heads(sr["skf"], T, sb_heads), heads(sr["svf"], T, sb_heads),
            heads(sr["dkf"], T, kv_heads), heads(sr["dvf"], T, kv_heads), idx(sr["ikf"], T))
```

```python
import functools

import jax
import jax.numpy as jnp
from jax import lax
from jax.experimental import pallas as pl
from jax.experimental.pallas import tpu as pltpu

F32 = jnp.float32
BF16 = jnp.bfloat16

CHUNK = 64
TOPK_MAX = 256
ROPE_THETA = 10000.0
EPS = 1e-6
HEAD_DIM = 64
IDX_HEADS = 8
EXPERT_TOPK = 2

LANES = 128
BLK = LANES
SLAB = 2 * BLK
VMEM_LIMIT = 56 * 1024 * 1024
INT_MIN = -(2 ** 31)
NEG_BIG = -1e30
SB_UNDERFLOW = -104.0


def _cparams(sem):
    return pltpu.CompilerParams(dimension_semantics=sem, vmem_limit_bytes=VMEM_LIMIT)


def _dot(a, b):
    return jnp.dot(a, b, preferred_element_type=F32)


def _dot_nt(a, b):
    return lax.dot_general(a, b, (((1,), (1,)), ((), ())), preferred_element_type=F32)


def _split_bf16(x):
    hi = x.astype(BF16)
    lo = (x - hi.astype(F32)).astype(BF16)
    return hi, lo


def _proj_kernel(x_ref, g1_ref, wa_ref, wg_ref, wiwt_ref, wdvt_ref, cos_ref, sin_ref,
                 qg_ref, kg_ref, gmat_ref,
                 sqb_ref, skf_ref, svf_ref, skb_ref, svb_ref, dqb_ref, dkf_ref, dvf_ref,
                 dkb_ref, dvt_ref, iqb_ref, ikf_ref, ikb_ref, iwt_ref, gsb_ref, gdsa_ref,
                 *, sb_w, dsa_w, kv_w, idx_w):
    x = x_ref[...]
    ms = jnp.mean(x * x, axis=-1, keepdims=True)
    xn = x * lax.rsqrt(ms + EPS) * g1_ref[...]
    xb = xn.astype(BF16)
    cos = cos_ref[...]
    sin = sin_ref[...]
    gmat = gmat_ref[...]
    lane = lax.broadcasted_iota(jnp.int32, cos.shape, 1)
    first_half = (lane % HEAD_DIM) < (HEAD_DIM // 2)

    def mm(lo, width):
        return _dot(xb, wa_ref[:, lo:lo + width])

    def head_norm(y, gain):
        hi, lo = _split_bf16(y * y)
        m = _dot(hi, gmat) + _dot(lo, gmat)
        return y * lax.rsqrt(m + EPS) * gain

    def rope(y):
        swapped = jnp.where(first_half, pltpu.roll(y, LANES - HEAD_DIM // 2, 1),
                            pltpu.roll(y, HEAD_DIM // 2, 1))
        return y * cos + swapped * sin

    scale = HEAD_DIM ** -0.5
    off = 0
    sqb_ref[...] = (mm(off, sb_w) * scale).astype(BF16)
    off += sb_w
    sk = mm(off, sb_w)
    skf_ref[...] = sk
    skb_ref[...] = sk.astype(BF16)
    off += sb_w
    sv = mm(off, sb_w)
    svf_ref[...] = sv
    svb_ref[...] = sv.astype(BF16)
    off += sb_w
    for c in range(dsa_w // LANES):
        y = rope(head_norm(mm(off + c * LANES, LANES), qg_ref[...]))
        dqb_ref[:, c * LANES:(c + 1) * LANES] = (y * scale).astype(BF16)
    off += dsa_w
    for c in range(kv_w // LANES):
        y = rope(head_norm(mm(off + c * LANES, LANES), kg_ref[...]))
        dkf_ref[:, c * LANES:(c + 1) * LANES] = y
        dkb_ref[:, c * LANES:(c + 1) * LANES] = y.astype(BF16)
    off += kv_w
    dvf_ref[...] = mm(off, kv_w)
    off += kv_w
    for c in range(idx_w // LANES):
        y = rope(mm(off + c * LANES, LANES))
        iqb_ref[:, c * LANES:(c + 1) * LANES] = (y * scale).astype(BF16)
    off += idx_w
    y = rope(mm(off, LANES))
    ikf_ref[...] = y[:, :HEAD_DIM]
    ikb_ref[...] = y.astype(BF16)
    iwt_ref[...] = _dot_nt(wiwt_ref[...], xb)
    dvt_ref[...] = _dot_nt(wdvt_ref[...], xb).astype(BF16)
    d = gsb_ref.shape[-1]
    gsb_ref[...] = jax.nn.sigmoid(_dot(xb, wg_ref[:, :d]))
    gdsa_ref[...] = jax.nn.sigmoid(_dot(xb, wg_ref[:, d:]))


def _rope_tables(pos):
    half = HEAD_DIM // 2
    freqs = ROPE_THETA ** (-jnp.arange(half, dtype=F32) / half)
    ang = pos.astype(F32)[:, None] * freqs[None, :]
    cos, sin = jnp.cos(ang), jnp.sin(ang)
    cos_t = jnp.tile(cos, (1, LANES // half))
    sin_t = jnp.tile(jnp.concatenate([-sin, sin], axis=1), (1, LANES // HEAD_DIM))
    return cos_t, sin_t


def _project(xpad, pos, w, dims):
    B, L, D = xpad.shape
    sb_w, dsa_w, kv_w, idx_w = dims
    tm = SLAB
    cos_t, sin_t = _rope_tables(pos)
    grid = (B, L // tm)
    row = lambda width: pl.BlockSpec((None, tm, width), lambda b, i: (b, i, 0))
    full = lambda a: pl.BlockSpec(a.shape, lambda b, i: (0,) * a.ndim)
    tab = pl.BlockSpec((tm, LANES), lambda b, i: (i, 0))
    out_shapes = dict(
        sqb=(BF16, sb_w), skf=(F32, sb_w), svf=(F32, sb_w), skb=(BF16, sb_w), svb=(BF16, sb_w),
        dqb=(BF16, dsa_w), dkf=(F32, kv_w), dvf=(F32, kv_w), dkb=(BF16, kv_w))
    names = ["sqb", "skf", "svf", "skb", "svb", "dqb", "dkf", "dvf", "dkb", "dvt", "iqb", "ikf",
             "ikb", "iwt", "gsb", "gdsa"]
    shapes, specs = [], []
    for n in names:
        if n in out_shapes:
            dt, width = out_shapes[n]
        elif n == "dvt":
            shapes.append(jax.ShapeDtypeStruct((B, L // SLAB, kv_w, SLAB), BF16))
            specs.append(pl.BlockSpec((None, None, kv_w, SLAB), lambda b, i: (b, i, 0, 0)))
            continue
        elif n == "iqb":
            dt, width = BF16, idx_w
        elif n == "ikf":
            dt, width = F32, HEAD_DIM
        elif n == "ikb":
            dt, width = BF16, LANES
        elif n == "iwt":
            shapes.append(jax.ShapeDtypeStruct((B, IDX_HEADS, L), F32))
            specs.append(pl.BlockSpec((None, IDX_HEADS, tm), lambda b, i: (b, 0, i)))
            continue
        else:
            dt, width = F32, D
        shapes.append(jax.ShapeDtypeStruct((B, L, width), dt))
        specs.append(row(width))
    ins = [xpad, w["g1"], w["wa"], w["wg"], w["wiwt"], w["wdvt"], cos_t, sin_t, w["qg"], w["kg"], w["gmat"]]
    in_specs = [row(D), full(w["g1"]), full(w["wa"]), full(w["wg"]), full(w["wiwt"]), full(w["wdvt"]),
                tab, tab, full(w["qg"]), full(w["kg"]), full(w["gmat"])]
    outs = pl.pallas_call(
        functools.partial(_proj_kernel, sb_w=sb_w, dsa_w=dsa_w, kv_w=kv_w, idx_w=idx_w),
        grid=grid, in_specs=in_specs, out_specs=specs, out_shape=shapes,
        compiler_params=_cparams(("parallel", "parallel")), name="proj",
    )(*ins)
    return dict(zip(names, outs))


def _sb_kernel(q_ref, k_ref, v_ref, uo_ref, o_ref, *, q0):
    qi = q0 + pl.program_id(2)
    q = q_ref[...]
    uo = uo_ref[...]
    n_pair = LANES // HEAD_DIM
    causal = (lax.broadcasted_iota(jnp.int32, (SLAB, SLAB), 1) < lax.broadcasted_iota(jnp.int32, (SLAB, SLAB), 0))
    v_lane_head = lax.broadcasted_iota(jnp.int32, (SLAB, LANES), 1) // HEAD_DIM

    def fold(j, cs, acc, diagonal):
        start = pl.multiple_of(j * SLAB, SLAB)
        kslab = k_ref[pl.ds(start, SLAB), :]
        vslab = v_ref[pl.ds(start, SLAB), :]
        cs_out = []
        for hh in range(n_pair):
            hs = slice(hh * HEAD_DIM, (hh + 1) * HEAD_DIM)
            z = _dot_nt(q[:, hs], kslab[:, hs])
            ls_pos = jnp.minimum(z, 0.0) - jnp.log(1.0 + jnp.exp(-jnp.abs(z)))
            ls_neg = ls_pos - z
            if diagonal:
                ls_neg = jnp.where(causal, ls_neg, 0.0)
            hi, lo = _split_bf16(ls_neg)
            r = _dot(hi, uo) + _dot(lo, uo)
            c = cs[hh]
            wgt = jnp.exp(ls_pos + r[:, :SLAB] + jnp.concatenate([c] * (SLAB // BLK), axis=1))
            if diagonal:
                wgt = jnp.where(causal, wgt, 0.0)
            vh = jnp.where(v_lane_head == hh, vslab, jnp.zeros_like(vslab))
            acc = acc + _dot(wgt.astype(BF16), vh)
            cs_out.append(c + r[:, SLAB:])
        cmax = functools.reduce(jnp.maximum, [jnp.max(c) for c in cs_out])
        return tuple(cs_out), acc, cmax

    zeros = jnp.zeros((SLAB, BLK), F32)
    cs, acc, cmax = fold(qi, (zeros,) * n_pair, jnp.zeros((SLAB, LANES), F32), True)

    def body(carry):
        j, cs, acc, _ = carry
        cs, acc, cmax = fold(j, cs, acc, False)
        return j - 1, cs, acc, cmax

    def cond(carry):
        j, _, _, cmax = carry
        return jnp.logical_and(j >= 0, cmax > SB_UNDERFLOW)

    _, _, acc, _ = lax.while_loop(cond, body, (qi - 1, cs, acc, cmax))
    o_ref[...] = acc.astype(o_ref.dtype)


def _sb_attention(q, k, v, q0):
    B, Lq, W = q.shape
    Lk = k.shape[1]
    uo = jnp.concatenate([jnp.tril(jnp.ones((SLAB, SLAB), F32), -1), jnp.ones((SLAB, BLK), F32)], axis=1).astype(BF16)
    grid = (B, W // LANES, Lq // SLAB)
    return pl.pallas_call(
        functools.partial(_sb_kernel, q0=q0),
        grid=grid,
        in_specs=[pl.BlockSpec((None, SLAB, LANES), lambda b, h, i: (b, i, h)),
                  pl.BlockSpec((None, Lk, LANES), lambda b, h, i: (b, 0, h)),
                  pl.BlockSpec((None, Lk, LANES), lambda b, h, i: (b, 0, h)),
                  pl.BlockSpec(uo.shape, lambda b, h, i: (0, 0))],
        out_specs=pl.BlockSpec((None, SLAB, LANES), lambda b, h, i: (b, i, h)),
        out_shape=jax.ShapeDtypeStruct((B, Lq, W), BF16),
        compiler_params=_cparams(("parallel", "parallel", "arbitrary")), name="sb_attn",
    )(q, k, v, uo)


def _dsa_kernel(q_ref, iq_ref, iwt_ref, k_ref, vt_ref, ik_ref, lt_ref, o_ref,
                keys_ref, acc_ref, *, q0, nslab_total, chunk_off, n_valid, n_sel, n_heads, kv_heads):
    qi = q0 + pl.program_id(1)
    nslab = jnp.minimum(qi + 2, nslab_total)
    group = n_heads // kv_heads
    idx_w_scale = IDX_HEADS ** -0.5
    tq = SLAB

    qpos = qi * tq + lax.broadcasted_iota(jnp.int32, (1, tq), 1)
    chunk_end = chunk_off + CHUNK * (jnp.right_shift(qpos - chunk_off, CHUNK.bit_length() - 1) + 1)
    key_limit = jnp.minimum(jnp.where(qpos < chunk_off, chunk_off, chunk_end), n_valid)
    slab_row = lax.broadcasted_iota(jnp.int32, (SLAB, tq), 0)

    iwt = iwt_ref[...]
    by_head = lambda x, heads: jnp.concatenate([x[:, h * HEAD_DIM:(h + 1) * HEAD_DIM] for h in heads], axis=0)
    iq_rows = by_head(iq_ref[...], range(IDX_HEADS))

    def score_slab(j, _):
        start = pl.multiple_of(j * SLAB, SLAB)
        s = _dot_nt(ik_ref[pl.ds(start, SLAB), :][:, :HEAD_DIM], iq_rows)
        acc = jnp.zeros((SLAB, tq), F32)
        for h in range(IDX_HEADS):
            acc = acc + jnp.maximum(s[:, h * tq:(h + 1) * tq], 0.0) * iwt[h:h + 1, :]
        bits = lax.bitcast_convert_type(acc * idx_w_scale, jnp.int32)
        key = bits ^ ((bits >> 31) & 0x7FFFFFFF)
        keys_ref[pl.ds(start, SLAB), :] = jnp.where(slab_row < key_limit - start, key, INT_MIN)
        return 0

    lax.fori_loop(0, nslab, score_slab, 0)

    def select_threshold(lanes):
        def count_ge(cand):
            def body(j, cnt):
                start = pl.multiple_of(j * SLAB, SLAB)
                return cnt + jnp.where(keys_ref[pl.ds(start, SLAB), lanes] >= cand, 1.0, 0.0)
            cnt = lax.fori_loop(0, nslab, body, jnp.zeros((SLAB, BLK), F32))
            return jnp.sum(cnt, axis=0, keepdims=True)

        def search(it, thr):
            cand = thr + jnp.left_shift(jnp.int32(1), 31 - it)
            return jnp.where(count_ge(cand) >= n_sel, cand, thr)

        thr = lax.fori_loop(0, 32, search, jnp.full((1, BLK), INT_MIN, jnp.int32))
        return thr, n_sel - count_ge(thr + 1)

    halves = [select_threshold(slice(c * BLK, (c + 1) * BLK)) for c in range(tq // BLK)]
    thr = jnp.concatenate([t for t, _ in halves], axis=1)
    room = jnp.concatenate([r for _, r in halves], axis=1)

    acc_ref[...] = jnp.zeros(acc_ref.shape, F32)
    q = q_ref[...]
    q_rows = [by_head(q, range(kvh * group, (kvh + 1) * group)) for kvh in range(kv_heads)]
    lt = lt_ref[...]

    def attend_slab(j, carry):
        eq_seen, m_all, l_all = carry
        start = pl.multiple_of(j * SLAB, SLAB)
        kk = keys_ref[pl.ds(start, SLAB), :]
        eq = kk == thr
        prefix = _dot(lt, jnp.where(eq, 1.0, 0.0).astype(BF16)) + eq_seen
        sel = jnp.logical_or(kk > thr, jnp.logical_and(eq, prefix <= room))
        sel = jnp.logical_and(sel, kk != INT_MIN)
        bias = jnp.where(sel, 0.0, NEG_BIG)
        keep = jnp.where(sel, 1.0, 0.0)
        kj = k_ref[pl.ds(start, SLAB), :]
        vtj = vt_ref[j]
        m_out, l_out = [], []
        for kvh in range(kv_heads):
            s_all = _dot_nt(kj[:, kvh * HEAD_DIM:(kvh + 1) * HEAD_DIM], q_rows[kvh])
            ps, alphas = [], []
            for g in range(group):
                h = kvh * group + g
                s = s_all[:, g * tq:(g + 1) * tq] + bias
                m_new = jnp.maximum(m_all[h], jnp.max(s, axis=0, keepdims=True))
                p = jnp.exp(s - m_new) * keep
                alpha = jnp.exp(m_all[h] - m_new)
                l_out.append(alpha * l_all[h] + jnp.sum(p, axis=0, keepdims=True))
                m_out.append(m_new)
                ps.append(p.astype(BF16))
                alphas.append(alpha)
            pv = _dot(vtj[kvh * HEAD_DIM:(kvh + 1) * HEAD_DIM, :], jnp.concatenate(ps, axis=1))
            acc_ref[kvh] = jnp.concatenate(alphas, axis=1) * acc_ref[kvh] + pv
        return prefix[SLAB - 1:SLAB, :], tuple(m_out), tuple(l_out)

    init = (jnp.zeros((1, tq), F32), (jnp.full((1, tq), NEG_BIG, F32),) * n_heads,
            (jnp.zeros((1, tq), F32),) * n_heads)
    _, _, l_all = lax.fori_loop(0, nslab, attend_slab, init)

    outs = []
    for kvh in range(kv_heads):
        acc = acc_ref[kvh]
        for g in range(group):
            outs.append(acc[:, g * tq:(g + 1) * tq] / l_all[kvh * group + g])
    o_ref[...] = jnp.concatenate(outs, axis=0).T.astype(o_ref.dtype)


def _dsa_attention(q, iq, iwt, k, vt, ik, *, q0, chunk_off, n_valid, n_sel, kv_heads):
    B, Lq, W = q.shape
    Lk = k.shape[1]
    n_heads = W // HEAD_DIM
    lt = jnp.tril(jnp.ones((SLAB, SLAB), F32)).astype(BF16)
    kern = functools.partial(_dsa_kernel, q0=q0, nslab_total=Lk // SLAB, chunk_off=chunk_off, n_valid=n_valid,
                             n_sel=n_sel, n_heads=n_heads, kv_heads=kv_heads)
    return pl.pallas_call(
        kern,
        grid=(B, Lq // SLAB),
        in_specs=[pl.BlockSpec((None, SLAB, W), lambda b, i: (b, i, 0)),
                  pl.BlockSpec((None, SLAB, iq.shape[2]), lambda b, i: (b, i, 0)),
                  pl.BlockSpec((None, IDX_HEADS, SLAB), lambda b, i: (b, 0, i)),
                  pl.BlockSpec((None, Lk, k.shape[2]), lambda b, i: (b, 0, 0)),
                  pl.BlockSpec((None,) + vt.shape[1:], lambda b, i: (b, 0, 0, 0)),
                  pl.BlockSpec((None, Lk, ik.shape[2]), lambda b, i: (b, 0, 0)),
                  pl.BlockSpec(lt.shape, lambda b, i: (0, 0))],
        out_specs=pl.BlockSpec((None, SLAB, W), lambda b, i: (b, i, 0)),
        out_shape=jax.ShapeDtypeStruct((B, Lq, W), BF16),
        scratch_shapes=[pltpu.VMEM((Lk, SLAB), jnp.int32),
                        pltpu.VMEM((kv_heads, HEAD_DIM, (n_heads // kv_heads) * SLAB), F32)],
        compiler_params=_cparams(("parallel", "arbitrary")), name="dsa_attn",
    )(q, iq, iwt, k, vt, ik, lt)


def _merge_kernel(x_ref, osb_ref, odsa_ref, gsb_ref, gdsa_ref, wsb_ref, wdsa_ref, wout_ref, g2_ref,
                  wr_hi_ref, wr_lo_ref, h_ref, hn_ref, comb_ref, *, n_groups, per_group):
    merged = gsb_ref[...] * _dot(osb_ref[...], wsb_ref[...]) + gdsa_ref[...] * _dot(odsa_ref[...], wdsa_ref[...])
    h = x_ref[...] + _dot(merged.astype(BF16), wout_ref[...])
    h_ref[...] = h
    ms = jnp.mean(h * h, axis=-1, keepdims=True)
    hn = h * lax.rsqrt(ms + EPS) * g2_ref[...]
    hn_ref[...] = hn.astype(BF16)

    hi, lo = _split_bf16(hn)
    logits = _dot(hi, wr_hi_ref[...]) + _dot(lo, wr_hi_ref[...]) + _dot(hi, wr_lo_ref[...])
    lane = lax.broadcasted_iota(jnp.int32, logits.shape, 1)
    big = jnp.int32(LANES)
    neg_inf = -jnp.inf

    def first_argmax(vals):
        top = jnp.max(vals, axis=-1, keepdims=True)
        idx = jnp.min(jnp.where(vals == top, lane, big), axis=-1, keepdims=True)
        return top, idx

    is_group = lane < n_groups
    gl = jnp.where(is_group, logits, neg_inf)
    ge = jnp.exp(gl - jnp.max(gl, axis=-1, keepdims=True))
    probs = jnp.where(is_group, ge / jnp.sum(ge, axis=-1, keepdims=True), neg_inf)
    gp, gi = first_argmax(probs)

    expert = lane - n_groups
    in_group = jnp.logical_and(expert >= gi * per_group, expert < (gi + 1) * per_group)
    vals = jnp.where(in_group, logits, neg_inf)
    ev0, i0 = first_argmax(vals)
    ev1, i1 = first_argmax(jnp.where(lane == i0, neg_inf, vals))
    e1 = jnp.exp(ev1 - ev0)
    w0 = gp / (1.0 + e1)
    w1 = gp * e1 / (1.0 + e1)
    comb_ref[...] = jnp.where(lane == i0, w0, jnp.where(lane == i1, w1, 0.0))


def _merge(x, osb, odsa, gsb, gdsa, w):
    N, D = x.shape
    tm = next(t for t in (512, 256, 128, 64, 32, 16) if N % t == 0)
    row = lambda a: pl.BlockSpec((tm, a.shape[1]), lambda i: (i, 0))
    full = lambda a: pl.BlockSpec(a.shape, lambda i: (0,) * a.ndim)
    ins = [x, osb, odsa, gsb, gdsa, w["wsb"], w["wdsa"], w["wout"], w["g2"], w["wr_hi"], w["wr_lo"]]
    in_specs = [row(a) for a in ins[:5]] + [full(a) for a in ins[5:]]
    return pl.pallas_call(
        functools.partial(_merge_kernel, n_groups=w["n_groups"], per_group=w["per_group"]),
        grid=(N // tm,), in_specs=in_specs,
        out_specs=[pl.BlockSpec((tm, D), lambda i: (i, 0)), pl.BlockSpec((tm, D), lambda i: (i, 0)),
                   pl.BlockSpec((tm, LANES), lambda i: (i, 0))],
        out_shape=[jax.ShapeDtypeStruct((N, D), F32), jax.ShapeDtypeStruct((N, D), BF16),
                   jax.ShapeDtypeStruct((N, LANES), F32)],
        compiler_params=_cparams(("parallel",)), name="merge_route",
    )(*ins)


def _moe_kernel(h_ref, hn_ref, comb_ref, wg_ref, wu_ref, wd_ref, y_ref, *, n_groups):
    e = pl.program_id(1)

    @pl.when(e == 0)
    def _():
        y_ref[...] = h_ref[...]

    hn = hn_ref[...]
    comb = comb_ref[...]
    lane = lax.broadcasted_iota(jnp.int32, comb.shape, 1)
    ce = jnp.sum(jnp.where(lane == e + n_groups, comb, 0.0), axis=-1, keepdims=True)
    a = _dot(hn, wg_ref[...])
    b = _dot(hn, wu_ref[...])
    hh = a * jax.nn.sigmoid(a) * b * ce
    y_ref[...] += _dot(hh.astype(BF16), wd_ref[...])


def _moe(h, hn, comb, w):
    N, D = h.shape
    E, _, Fd = w["wgate"].shape
    tm = next(t for t in (1024, 512, 256, 128, 64, 32, 16) if N % t == 0)
    return pl.pallas_call(
        functools.partial(_moe_kernel, n_groups=w["n_groups"]),
        grid=(N // tm, E),
        in_specs=[pl.BlockSpec((tm, D), lambda i, e: (i, 0)),
                  pl.BlockSpec((tm, D), lambda i, e: (i, 0)),
                  pl.BlockSpec((tm, LANES), lambda i, e: (i, 0)),
                  pl.BlockSpec((None, D, Fd), lambda i, e: (e, 0, 0)),
                  pl.BlockSpec((None, D, Fd), lambda i, e: (e, 0, 0)),
                  pl.BlockSpec((None, Fd, D), lambda i, e: (e, 0, 0))],
        out_specs=pl.BlockSpec((tm, D), lambda i, e: (i, 0)),
        out_shape=jax.ShapeDtypeStruct((N, D), F32),
        compiler_params=_cparams(("parallel", "arbitrary")), name="moe_experts",
    )(h, hn, comb, w["wgate"], w["wup"], w["wdown"])


def _prep_weights(norm1_g, w_in, qg, kg, w_sbb, w_dsab, w_out, norm2_g, w_group, w_router, w_gate, w_up, w_down,
                  dims):
    sb_w, dsa_w, kv_w, idx_w = dims
    D = w_in.shape[0]
    o_dv = 3 * sb_w + dsa_w + kv_w
    o_ik = o_dv + kv_w + idx_w
    o_iw = o_ik + HEAD_DIM
    o_g = o_iw + IDX_HEADS
    w_ik = w_in[:, o_ik:o_iw]
    n_groups = w_group.shape[1]
    n_experts = w_router.shape[1]
    assert n_groups + n_experts <= LANES
    wr = jnp.concatenate([w_group, w_router, jnp.zeros((D, LANES - n_groups - n_experts), F32)], axis=1)
    wr_hi, wr_lo = _split_bf16(wr)
    tile2 = lambda g: jnp.tile(g.astype(F32), LANES // HEAD_DIM)[None, :]
    head_id = jnp.arange(LANES) // HEAD_DIM
    return dict(
        g1=norm1_g.astype(F32)[None, :],
        wa=jnp.concatenate([w_in[:, :o_iw], w_ik], axis=1).astype(BF16),
        wg=w_in[:, o_g:].astype(BF16),
        wiwt=w_in[:, o_iw:o_g].T.astype(BF16),
        wdvt=w_in[:, o_dv:o_dv + kv_w].T.astype(BF16),
        qg=tile2(qg), kg=tile2(kg),
        gmat=((head_id[:, None] == head_id[None, :]).astype(F32) / HEAD_DIM).astype(BF16),
        wsb=w_sbb.astype(BF16), wdsa=w_dsab.astype(BF16), wout=w_out.astype(BF16),
        g2=norm2_g.astype(F32)[None, :], wr_hi=wr_hi, wr_lo=wr_lo,
        wgate=w_gate.astype(BF16), wup=w_up.astype(BF16), wdown=w_down.astype(BF16),
        n_groups=n_groups, per_group=n_experts // n_groups)


def _tail(x2d, osb, odsa, gsb, gdsa, w):
    h, hn, comb = _merge(x2d, osb, odsa, gsb, gdsa, w)
    return _moe(h, hn, comb, w)


def kernel(x_prompt, x_sample, cache_sb_k, cache_sb_v, cache_dsa_k, cache_dsa_v, cache_idx_k, meta_tokens, norm1_g, w_in, dsa_q_norm_g, dsa_k_norm_g, w_sb_branch, w_dsa_branch, w_out, norm2_g, w_group, w_router, w_gate, w_up, w_down):
    B, S, D = x_prompt.shape
    Bs, T, _ = x_sample.shape
    depth, _, past, sb_heads, hd = cache_sb_k.shape
    kv_heads = cache_dsa_k.shape[3]
    n_meta = meta_tokens.shape[0]
    assert depth == 1 and hd == HEAD_DIM and cache_idx_k.shape[-1] == HEAD_DIM
    assert past % SLAB == 0 and T <= BLK and S % CHUNK == 0
    sb_w = sb_heads * HEAD_DIM
    dsa_w = w_dsa_branch.shape[1]
    kv_w = kv_heads * HEAD_DIM
    idx_w = IDX_HEADS * HEAD_DIM
    assert kv_w == LANES and sb_w % LANES == 0 and dsa_w % LANES == 0
    dims = (sb_w, dsa_w, kv_w, idx_w)
    w = _prep_weights(norm1_g[0], w_in[0], dsa_q_norm_g[0], dsa_k_norm_g[0], w_sb_branch[0], w_dsa_branch[0],
                      w_out[0], norm2_g[0], w_group[0], w_router[0], w_gate[0], w_up[0], w_down[0], dims)

    Lr = n_meta + S
    Lp = -(-Lr // SLAB) * SLAB
    meta = jnp.broadcast_to(meta_tokens[None].astype(x_prompt.dtype), (B, n_meta, D))
    xp = jnp.concatenate([meta, x_prompt, jnp.zeros((B, Lp - Lr, D), x_prompt.dtype)], axis=1)
    pr = _project(xp, jnp.arange(Lp, dtype=jnp.int32), w, dims)
    o_sb = _sb_attention(pr["sqb"], pr["skb"], pr["svb"], 0)
    o_dsa = _dsa_attention(pr["dqb"], pr["iqb"], pr["iwt"], pr["dkb"], pr["dvt"], pr["ikb"], q0=0,
                           chunk_off=n_meta, n_valid=Lr, n_sel=min(TOPK_MAX, S // 4), kv_heads=kv_heads)
    flat = lambda a: a.reshape(B * Lp, a.shape[-1])
    yp = _tail(flat(xp), flat(o_sb), flat(o_dsa), flat(pr["gsb"]), flat(pr["gdsa"]), w)
    y_prompt = yp.reshape(B, Lp, D)[:, n_meta:Lr]

    xs = jnp.concatenate([x_sample, jnp.zeros((Bs, SLAB - T, D), x_sample.dtype)], axis=1)
    sr = _project(xs, past + jnp.arange(SLAB, dtype=jnp.int32), w, dims)
    cat = lambda c, new: jnp.concatenate([c.astype(BF16), new], axis=1)
    k_sb = cat(cache_sb_k[0].reshape(Bs, past, sb_w), sr["skb"])
    v_sb = cat(cache_sb_v[0].reshape(Bs, past, sb_w), sr["svb"])
    k_ds = cat(cache_dsa_k[0].reshape(Bs, past, kv_w), sr["dkb"])
    k_ix = cat(jnp.tile(cache_idx_k[0], (1, 1, LANES // HEAD_DIM)), sr["ikb"])
    vt_c = cache_dsa_v[0].reshape(Bs, past // SLAB, SLAB, kv_w).transpose(0, 1, 3, 2).astype(BF16)
    vt_ds = jnp.concatenate([vt_c, sr["dvt"]], axis=1)
    qb = past // SLAB
    o_sb_s = _sb_attention(sr["sqb"], k_sb, v_sb, qb)
    o_dsa_s = _dsa_attention(sr["dqb"], sr["iqb"], sr["iwt"], k_ds, vt_ds, k_ix, q0=qb,
                             chunk_off=0, n_valid=past + T, n_sel=min(TOPK_MAX, (past + T) // 4), kv_heads=kv_heads)
    real = lambda a: a[:, :T].reshape(Bs * T, a.shape[-1])
    ys = _tail(real(xs), real(o_sb_s), real(o_dsa_s), real(sr["gsb"]), real(sr["gdsa"]), w)
    y_sample = ys.reshape(Bs, T, D)

    heads = lambda a, L, n: a[:, :L].reshape(1, a.shape[0], L, n, HEAD_DIM)
    idx = lambda a, L: a[:, :L][None]
    return (y_prompt, y_sample,
            heads(pr["skf"], Lr, sb_heads), heads(pr["svf"], Lr, sb_heads),
            heads(pr["dkf"], Lr, kv_heads), heads(pr["dvf"], Lr, kv_heads), idx(pr["ikf"], Lr),
            heads(sr["skf"], T, sb_heads), heads(sr["svf"], T, sb_heads),
            heads(sr["dkf"], T, kv_heads), heads(sr["dvf"], T, kv_heads), idx(sr["ikf"], T))
```

```python
import functools

import jax
import jax.numpy as jnp
from jax import lax
from jax.experimental import pallas as pl
from jax.experimental.pallas import tpu as pltpu

F32 = jnp.float32
BF16 = jnp.bfloat16

CHUNK = 64
TOPK_MAX = 256
ROPE_THETA = 10000.0
EPS = 1e-6
HEAD_DIM = 64
IDX_HEADS = 8
EXPERT_TOPK = 2

LANES = 128
BLK = LANES
SLAB = 2 * BLK
VMEM_LIMIT = 56 * 1024 * 1024
INT_MIN = -(2 ** 31)
NEG_BIG = -1e30
LOG2E = 1.4426950408889634
SB_UNDERFLOW = -151.0


def _cparams(sem):
    return pltpu.CompilerParams(dimension_semantics=sem, vmem_limit_bytes=VMEM_LIMIT)


def _dot(a, b):
    return jnp.dot(a, b, preferred_element_type=F32)


def _dot_nt(a, b):
    return lax.dot_general(a, b, (((1,), (1,)), ((), ())), preferred_element_type=F32)


def _split_bf16(x):
    hi = x.astype(BF16)
    lo = (x - hi.astype(F32)).astype(BF16)
    return hi, lo


def _proj_kernel(x_ref, g1_ref, wa_ref, wg_ref, wiwt_ref, wdvt_ref, cos_ref, sin_ref,
                 qg_ref, kg_ref, gmat_ref,
                 sqb_ref, skf_ref, svf_ref, skb_ref, svb_ref, dqb_ref, dkf_ref, dvf_ref,
                 dkb_ref, dvt_ref, iqb_ref, ikf_ref, ikb_ref, iwt_ref, gsb_ref, gdsa_ref,
                 *, sb_w, dsa_w, kv_w, idx_w):
    x = x_ref[...]
    ms = jnp.mean(x * x, axis=-1, keepdims=True)
    xn = x * lax.rsqrt(ms + EPS) * g1_ref[...]
    xb = xn.astype(BF16)
    cos = cos_ref[...]
    sin = sin_ref[...]
    gmat = gmat_ref[...]
    lane = lax.broadcasted_iota(jnp.int32, cos.shape, 1)
    first_half = (lane % HEAD_DIM) < (HEAD_DIM // 2)

    def mm(lo, width):
        return _dot(xb, wa_ref[:, lo:lo + width])

    def head_norm(y, gain):
        hi, lo = _split_bf16(y * y)
        m = _dot(hi, gmat) + _dot(lo, gmat)
        return y * lax.rsqrt(m + EPS) * gain

    def rope(y):
        swapped = jnp.where(first_half, pltpu.roll(y, LANES - HEAD_DIM // 2, 1),
                            pltpu.roll(y, HEAD_DIM // 2, 1))
        return y * cos + swapped * sin

    scale = HEAD_DIM ** -0.5
    scale2 = scale * LOG2E
    off = 0
    sqb_ref[...] = (mm(off, sb_w) * scale2).astype(BF16)
    off += sb_w
    sk = mm(off, sb_w)
    skf_ref[...] = sk
    skb_ref[...] = sk.astype(BF16)
    off += sb_w
    sv = mm(off, sb_w)
    svf_ref[...] = sv
    svb_ref[...] = sv.astype(BF16)
    off += sb_w
    for c in range(dsa_w // LANES):
        y = rope(head_norm(mm(off + c * LANES, LANES), qg_ref[...]))
        dqb_ref[:, c * LANES:(c + 1) * LANES] = (y * scale2).astype(BF16)
    off += dsa_w
    for c in range(kv_w // LANES):
        y = rope(head_norm(mm(off + c * LANES, LANES), kg_ref[...]))
        dkf_ref[:, c * LANES:(c + 1) * LANES] = y
        dkb_ref[:, c * LANES:(c + 1) * LANES] = y.astype(BF16)
    off += kv_w
    dvf_ref[...] = mm(off, kv_w)
    off += kv_w
    for c in range(idx_w // LANES):
        y = rope(mm(off + c * LANES, LANES))
        iqb_ref[:, c * LANES:(c + 1) * LANES] = (y * scale).astype(BF16)
    off += idx_w
    y = rope(mm(off, LANES))
    ikf_ref[...] = y[:, :HEAD_DIM]
    ikb_ref[...] = y.astype(BF16)
    iwt_ref[...] = _dot_nt(wiwt_ref[...], xb)
    dvt_ref[...] = _dot_nt(wdvt_ref[...], xb).astype(BF16)
    d = gsb_ref.shape[-1]
    gsb_ref[...] = jax.nn.sigmoid(_dot(xb, wg_ref[:, :d]))
    gdsa_ref[...] = jax.nn.sigmoid(_dot(xb, wg_ref[:, d:]))


def _rope_tables(pos):
    half = HEAD_DIM // 2
    freqs = ROPE_THETA ** (-jnp.arange(half, dtype=F32) / half)
    ang = pos.astype(F32)[:, None] * freqs[None, :]
    cos, sin = jnp.cos(ang), jnp.sin(ang)
    cos_t = jnp.tile(cos, (1, LANES // half))
    sin_t = jnp.tile(jnp.concatenate([-sin, sin], axis=1), (1, LANES // HEAD_DIM))
    return cos_t, sin_t


def _project(xpad, pos, w, dims):
    B, L, D = xpad.shape
    sb_w, dsa_w, kv_w, idx_w = dims
    tm = SLAB
    cos_t, sin_t = _rope_tables(pos)
    grid = (B, L // tm)
    row = lambda width: pl.BlockSpec((None, tm, width), lambda b, i: (b, i, 0))
    full = lambda a: pl.BlockSpec(a.shape, lambda b, i: (0,) * a.ndim)
    tab = pl.BlockSpec((tm, LANES), lambda b, i: (i, 0))
    out_shapes = dict(
        sqb=(BF16, sb_w), skf=(F32, sb_w), svf=(F32, sb_w), skb=(BF16, sb_w), svb=(BF16, sb_w),
        dqb=(BF16, dsa_w), dkf=(F32, kv_w), dvf=(F32, kv_w), dkb=(BF16, kv_w))
    names = ["sqb", "skf", "svf", "skb", "svb", "dqb", "dkf", "dvf", "dkb", "dvt", "iqb", "ikf",
             "ikb", "iwt", "gsb", "gdsa"]
    shapes, specs = [], []
    for n in names:
        if n in out_shapes:
            dt, width = out_shapes[n]
        elif n == "dvt":
            shapes.append(jax.ShapeDtypeStruct((B, L // SLAB, kv_w, SLAB), BF16))
            specs.append(pl.BlockSpec((None, None, kv_w, SLAB), lambda b, i: (b, i, 0, 0)))
            continue
        elif n == "iqb":
            dt, width = BF16, idx_w
        elif n == "ikf":
            dt, width = F32, HEAD_DIM
        elif n == "ikb":
            dt, width = BF16, LANES
        elif n == "iwt":
            shapes.append(jax.ShapeDtypeStruct((B, IDX_HEADS, L), F32))
            specs.append(pl.BlockSpec((None, IDX_HEADS, tm), lambda b, i: (b, 0, i)))
            continue
        else:
            dt, width = F32, D
        shapes.append(jax.ShapeDtypeStruct((B, L, width), dt))
        specs.append(row(width))
    ins = [xpad, w["g1"], w["wa"], w["wg"], w["wiwt"], w["wdvt"], cos_t, sin_t, w["qg"], w["kg"], w["gmat"]]
    in_specs = [row(D), full(w["g1"]), full(w["wa"]), full(w["wg"]), full(w["wiwt"]), full(w["wdvt"]),
                tab, tab, full(w["qg"]), full(w["kg"]), full(w["gmat"])]
    outs = pl.pallas_call(
        functools.partial(_proj_kernel, sb_w=sb_w, dsa_w=dsa_w, kv_w=kv_w, idx_w=idx_w),
        grid=grid, in_specs=in_specs, out_specs=specs, out_shape=shapes,
        compiler_params=_cparams(("parallel", "parallel")), name="proj",
    )(*ins)
    return dict(zip(names, outs))


def _sb_kernel(q_ref, k_ref, v_ref, uo_ref, o_ref, *, q0):
    qi = q0 + pl.program_id(2)
    q = q_ref[...]
    uo = uo_ref[...]
    n_pair = LANES // HEAD_DIM
    causal = (lax.broadcasted_iota(jnp.int32, (SLAB, SLAB), 1) < lax.broadcasted_iota(jnp.int32, (SLAB, SLAB), 0))
    v_lane_head = lax.broadcasted_iota(jnp.int32, (SLAB, LANES), 1) // HEAD_DIM

    def fold(j, cs, acc, diagonal):
        start = pl.multiple_of(j * SLAB, SLAB)
        kslab = k_ref[pl.ds(start, SLAB), :]
        vslab = v_ref[pl.ds(start, SLAB), :]
        cs_out = []
        for hh in range(n_pair):
            hs = slice(hh * HEAD_DIM, (hh + 1) * HEAD_DIM)
            z = _dot_nt(q[:, hs], kslab[:, hs])
            ls_pos = jnp.minimum(z, 0.0) - jnp.log2(1.0 + jnp.exp2(-jnp.abs(z)))
            ls_neg = ls_pos - z
            if diagonal:
                ls_neg = jnp.where(causal, ls_neg, 0.0)
            hi, lo = _split_bf16(ls_neg)
            r = _dot(hi, uo) + _dot(lo, uo)
            c = cs[hh]
            wgt = jnp.exp2(ls_pos + r[:, :SLAB] + jnp.concatenate([c] * (SLAB // BLK), axis=1))
            if diagonal:
                wgt = jnp.where(causal, wgt, 0.0)
            vh = jnp.where(v_lane_head == hh, vslab, jnp.zeros_like(vslab))
            acc = acc + _dot(wgt.astype(BF16), vh)
            cs_out.append(c + r[:, SLAB:])
        cmax = functools.reduce(jnp.maximum, [jnp.max(c) for c in cs_out])
        return tuple(cs_out), acc, cmax

    zeros = jnp.zeros((SLAB, BLK), F32)
    cs, acc, cmax = fold(qi, (zeros,) * n_pair, jnp.zeros((SLAB, LANES), F32), True)

    def body(carry):
        j, cs, acc, _ = carry
        cs, acc, cmax = fold(j, cs, acc, False)
        return j - 1, cs, acc, cmax

    def cond(carry):
        j, _, _, cmax = carry
        return jnp.logical_and(j >= 0, cmax > SB_UNDERFLOW)

    _, _, acc, _ = lax.while_loop(cond, body, (qi - 1, cs, acc, cmax))
    o_ref[...] = acc.astype(o_ref.dtype)


def _sb_attention(q, k, v, q0):
    B, Lq, W = q.shape
    Lk = k.shape[1]
    uo = jnp.concatenate([jnp.tril(jnp.ones((SLAB, SLAB), F32), -1), jnp.ones((SLAB, BLK), F32)], axis=1).astype(BF16)
    grid = (B, W // LANES, Lq // SLAB)
    return pl.pallas_call(
        functools.partial(_sb_kernel, q0=q0),
        grid=grid,
        in_specs=[pl.BlockSpec((None, SLAB, LANES), lambda b, h, i: (b, i, h)),
                  pl.BlockSpec((None, Lk, LANES), lambda b, h, i: (b, 0, h)),
                  pl.BlockSpec((None, Lk, LANES), lambda b, h, i: (b, 0, h)),
                  pl.BlockSpec(uo.shape, lambda b, h, i: (0, 0))],
        out_specs=pl.BlockSpec((None, SLAB, LANES), lambda b, h, i: (b, i, h)),
        out_shape=jax.ShapeDtypeStruct((B, Lq, W), BF16),
        compiler_params=_cparams(("parallel", "parallel", "arbitrary")), name="sb_attn",
    )(q, k, v, uo)


def _dsa_kernel(q_ref, iq_ref, iwt_ref, k_ref, vt_ref, ik_ref, lt_ref, o_ref,
                keys_ref, hi16_ref, lo16_ref, acc_ref, *, q0, nslab_total, chunk_off, n_valid, n_sel, n_heads, kv_heads):
    qi = q0 + pl.program_id(1)
    nslab = jnp.minimum(qi + 2, nslab_total)
    group = n_heads // kv_heads
    idx_w_scale = IDX_HEADS ** -0.5
    tq = SLAB

    qpos = qi * tq + lax.broadcasted_iota(jnp.int32, (1, tq), 1)
    chunk_end = chunk_off + CHUNK * (jnp.right_shift(qpos - chunk_off, CHUNK.bit_length() - 1) + 1)
    key_limit = jnp.minimum(jnp.where(qpos < chunk_off, chunk_off, chunk_end), n_valid)
    slab_row = lax.broadcasted_iota(jnp.int32, (SLAB, tq), 0)

    iwt = iwt_ref[...]
    by_head = lambda x, heads: jnp.concatenate([x[:, h * HEAD_DIM:(h + 1) * HEAD_DIM] for h in heads], axis=0)
    iq_rows = by_head(iq_ref[...], range(IDX_HEADS))

    def score_slab(j, _):
        start = pl.multiple_of(j * SLAB, SLAB)
        s = _dot_nt(ik_ref[pl.ds(start, SLAB), :][:, :HEAD_DIM], iq_rows)
        acc = jnp.zeros((SLAB, tq), F32)
        for h in range(IDX_HEADS):
            acc = acc + jnp.maximum(s[:, h * tq:(h + 1) * tq], 0.0) * iwt[h:h + 1, :]
        bits = lax.bitcast_convert_type(acc * idx_w_scale, jnp.int32)
        key = bits ^ ((bits >> 31) & 0x7FFFFFFF)
        key = jnp.where(slab_row < key_limit - start, key, INT_MIN)
        keys_ref[pl.ds(start, SLAB), :] = key
        hi16_ref[pl.ds(start, SLAB), :] = (key >> 16).astype(jnp.int16)
        lo16_ref[pl.ds(start, SLAB), :] = ((key & 0xFFFF) - 2 ** 15).astype(jnp.int16)
        return 0

    lax.fori_loop(0, nslab // 2, lambda t, _: score_slab(2 * t + 1, score_slab(2 * t, 0)), 0)
    lax.fori_loop(2 * (nslab // 2), nslab, score_slab, 0)

    rows16 = 2 * 8
    one16 = jnp.ones((SLAB // rows16, rows16, BLK), jnp.int16)

    def tile16(c):
        word = (c & 0xFFFF) | (c << 16)
        return pltpu.bitcast(jnp.broadcast_to(word, (8, BLK)), jnp.int16)

    def select_threshold(lanes):
        def slab16(ref, j):
            start = pl.multiple_of(j * SLAB, SLAB)
            return ref[pl.ds(start, SLAB), lanes].reshape(SLAB // rows16, rows16, BLK)

        def count(ref, cand, strict):
            cand16 = tile16(cand)[None]

            def body(j, cnt):
                x = slab16(ref, j)
                hit = (x > cand16) if strict else (x >= cand16)
                return cnt + jnp.where(hit, one16, jnp.zeros_like(one16))
            cnt = lax.fori_loop(0, nslab, body, jnp.zeros_like(one16))
            per_row = functools.reduce(lambda a, b: a + b, [cnt[i] for i in range(SLAB // rows16)])
            return jnp.sum(per_row.astype(F32), axis=0, keepdims=True)

        def search16(ref, above):
            def step(it, t):
                cand = t + jnp.left_shift(jnp.int32(1), 15 - it)
                return jnp.where(above + count(ref, cand, False) >= n_sel, cand, t)
            return lax.fori_loop(0, 16, step, jnp.full((1, BLK), -(2 ** 15), jnp.int32))

        t_hi = search16(hi16_ref, 0.0)
        above_hi = count(hi16_ref, t_hi, True)
        t_hi16 = tile16(t_hi)[None]

        def park(j, _):
            start = pl.multiple_of(j * SLAB, SLAB)
            lo = jnp.where(slab16(hi16_ref, j) == t_hi16, slab16(lo16_ref, j), jnp.full_like(one16, -(2 ** 15)))
            lo16_ref[pl.ds(start, SLAB), lanes] = lo.reshape(SLAB, BLK)
            return 0

        lax.fori_loop(0, nslab, park, 0)
        t_lo = search16(lo16_ref, above_hi)
        thr = t_hi * (2 ** 16) + (t_lo + 2 ** 15)
        return thr, n_sel - (above_hi + count(lo16_ref, t_lo, True))

    halves = [select_threshold(slice(c * BLK, (c + 1) * BLK)) for c in range(tq // BLK)]
    thr = jnp.concatenate([t for t, _ in halves], axis=1)
    room = jnp.concatenate([r for _, r in halves], axis=1)

    acc_ref[...] = jnp.zeros(acc_ref.shape, F32)
    q = q_ref[...]
    q_rows = [by_head(q, range(kvh * group, (kvh + 1) * group)) for kvh in range(kv_heads)]
    lt = lt_ref[...]

    def attend_slab(j, carry):
        eq_seen, m_all, l_all = carry
        start = pl.multiple_of(j * SLAB, SLAB)
        kk = keys_ref[pl.ds(start, SLAB), :]
        eq = kk == thr
        prefix = _dot(lt, jnp.where(eq, 1.0, 0.0).astype(BF16)) + eq_seen
        sel = jnp.logical_or(kk > thr, jnp.logical_and(eq, prefix <= room))
        sel = jnp.logical_and(sel, kk != INT_MIN)
        bias = jnp.where(sel, 0.0, NEG_BIG)
        kj = k_ref[pl.ds(start, SLAB), :]
        vtj = vt_ref[j]
        m_out, l_out = [], []
        for kvh in range(kv_heads):
            s_all = _dot_nt(kj[:, kvh * HEAD_DIM:(kvh + 1) * HEAD_DIM], q_rows[kvh])
            ps, alphas = [], []
            for g in range(group):
                h = kvh * group + g
                s = s_all[:, g * tq:(g + 1) * tq] + bias
                m_new = jnp.maximum(m_all[h], jnp.max(s, axis=0, keepdims=True))
                p = jnp.exp2(s - m_new)
                alpha = jnp.exp2(m_all[h] - m_new)
                l_out.append(alpha * l_all[h] + jnp.sum(p, axis=0, keepdims=True))
                m_out.append(m_new)
                ps.append(p.astype(BF16))
                alphas.append(alpha)
            pv = _dot(vtj[kvh * HEAD_DIM:(kvh + 1) * HEAD_DIM, :], jnp.concatenate(ps, axis=1))
            acc_ref[kvh] = jnp.concatenate(alphas, axis=1) * acc_ref[kvh] + pv
        return prefix[SLAB - 1:SLAB, :], tuple(m_out), tuple(l_out)

    init = (jnp.zeros((1, tq), F32), (jnp.full((1, tq), NEG_BIG, F32),) * n_heads,
            (jnp.zeros((1, tq), F32),) * n_heads)
    carry = lax.fori_loop(0, nslab // 2, lambda t, c: attend_slab(2 * t + 1, attend_slab(2 * t, c)), init)
    _, _, l_all = lax.fori_loop(2 * (nslab // 2), nslab, attend_slab, carry)

    outs = []
    for kvh in range(kv_heads):
        acc = acc_ref[kvh]
        for g in range(group):
            outs.append(acc[:, g * tq:(g + 1) * tq] / l_all[kvh * group + g])
    o_ref[...] = jnp.concatenate(outs, axis=0).T.astype(o_ref.dtype)


def _dsa_attention(q, iq, iwt, k, vt, ik, *, q0, chunk_off, n_valid, n_sel, kv_heads):
    B, Lq, W = q.shape
    Lk = k.shape[1]
    n_heads = W // HEAD_DIM
    lt = jnp.tril(jnp.ones((SLAB, SLAB), F32)).astype(BF16)
    kern = functools.partial(_dsa_kernel, q0=q0, nslab_total=Lk // SLAB, chunk_off=chunk_off, n_valid=n_valid,
                             n_sel=n_sel, n_heads=n_heads, kv_heads=kv_heads)
    return pl.pallas_call(
        kern,
        grid=(B, Lq // SLAB),
        in_specs=[pl.BlockSpec((None, SLAB, W), lambda b, i: (b, i, 0)),
                  pl.BlockSpec((None, SLAB, iq.shape[2]), lambda b, i: (b, i, 0)),
                  pl.BlockSpec((None, IDX_HEADS, SLAB), lambda b, i: (b, 0, i)),
                  pl.BlockSpec((None, Lk, k.shape[2]), lambda b, i: (b, 0, 0)),
                  pl.BlockSpec((None,) + vt.shape[1:], lambda b, i: (b, 0, 0, 0)),
                  pl.BlockSpec((None, Lk, ik.shape[2]), lambda b, i: (b, 0, 0)),
                  pl.BlockSpec(lt.shape, lambda b, i: (0, 0))],
        out_specs=pl.BlockSpec((None, SLAB, W), lambda b, i: (b, i, 0)),
        out_shape=jax.ShapeDtypeStruct((B, Lq, W), BF16),
        scratch_shapes=[pltpu.VMEM((Lk, SLAB), jnp.int32),
                        pltpu.VMEM((Lk, SLAB), jnp.int16), pltpu.VMEM((Lk, SLAB), jnp.int16),
                        pltpu.VMEM((kv_heads, HEAD_DIM, (n_heads // kv_heads) * SLAB), F32)],
        compiler_params=_cparams(("parallel", "arbitrary")), name="dsa_attn",
    )(q, iq, iwt, k, vt, ik, lt)


def _merge_kernel(x_ref, osb_ref, odsa_ref, gsb_ref, gdsa_ref, wsb_ref, wdsa_ref, wout_ref, g2_ref,
                  wr_hi_ref, wr_lo_ref, h_ref, hn_ref, comb_ref, *, n_groups, per_group):
    merged = gsb_ref[...] * _dot(osb_ref[...], wsb_ref[...]) + gdsa_ref[...] * _dot(odsa_ref[...], wdsa_ref[...])
    h = x_ref[...] + _dot(merged.astype(BF16), wout_ref[...])
    h_ref[...] = h
    ms = jnp.mean(h * h, axis=-1, keepdims=True)
    hn = h * lax.rsqrt(ms + EPS) * g2_ref[...]
    hn_ref[...] = hn.astype(BF16)

    hi, lo = _split_bf16(hn)
    logits = _dot(hi, wr_hi_ref[...]) + _dot(lo, wr_hi_ref[...]) + _dot(hi, wr_lo_ref[...])
    lane = lax.broadcasted_iota(jnp.int32, logits.shape, 1)
    big = jnp.int32(LANES)
    neg_inf = -jnp.inf

    def first_argmax(vals):
        top = jnp.max(vals, axis=-1, keepdims=True)
        idx = jnp.min(jnp.where(vals == top, lane, big), axis=-1, keepdims=True)
        return top, idx

    is_group = lane < n_groups
    gl = jnp.where(is_group, logits, neg_inf)
    ge = jnp.exp(gl - jnp.max(gl, axis=-1, keepdims=True))
    probs = jnp.where(is_group, ge / jnp.sum(ge, axis=-1, keepdims=True), neg_inf)
    gp, gi = first_argmax(probs)

    expert = lane - n_groups
    in_group = jnp.logical_and(expert >= gi * per_group, expert < (gi + 1) * per_group)
    vals = jnp.where(in_group, logits, neg_inf)
    ev0, i0 = first_argmax(vals)
    ev1, i1 = first_argmax(jnp.where(lane == i0, neg_inf, vals))
    e1 = jnp.exp(ev1 - ev0)
    w0 = gp / (1.0 + e1)
    w1 = gp * e1 / (1.0 + e1)
    comb_ref[...] = jnp.where(lane == i0, w0, jnp.where(lane == i1, w1, 0.0))


def _merge(x, osb, odsa, gsb, gdsa, w):
    N, D = x.shape
    tm = next(t for t in (512, 256, 128, 64, 32, 16) if N % t == 0)
    row = lambda a: pl.BlockSpec((tm, a.shape[1]), lambda i: (i, 0))
    full = lambda a: pl.BlockSpec(a.shape, lambda i: (0,) * a.ndim)
    ins = [x, osb, odsa, gsb, gdsa, w["wsb"], w["wdsa"], w["wout"], w["g2"], w["wr_hi"], w["wr_lo"]]
    in_specs = [row(a) for a in ins[:5]] + [full(a) for a in ins[5:]]
    return pl.pallas_call(
        functools.partial(_merge_kernel, n_groups=w["n_groups"], per_group=w["per_group"]),
        grid=(N // tm,), in_specs=in_specs,
        out_specs=[pl.BlockSpec((tm, D), lambda i: (i, 0)), pl.BlockSpec((tm, D), lambda i: (i, 0)),
                   pl.BlockSpec((tm, LANES), lambda i: (i, 0))],
        out_shape=[jax.ShapeDtypeStruct((N, D), F32), jax.ShapeDtypeStruct((N, D), BF16),
                   jax.ShapeDtypeStruct((N, LANES), F32)],
        compiler_params=_cparams(("parallel",)), name="merge_route",
    )(*ins)


def _moe_kernel(h_ref, hn_ref, comb_ref, wg_ref, wu_ref, wd_ref, y_ref, *, n_groups):
    e = pl.program_id(1)

    @pl.when(e == 0)
    def _():
        y_ref[...] = h_ref[...]

    hn = hn_ref[...]
    comb = comb_ref[...]
    lane = lax.broadcasted_iota(jnp.int32, comb.shape, 1)
    ce = jnp.sum(jnp.where(lane == e + n_groups, comb, 0.0), axis=-1, keepdims=True)
    a = _dot(hn, wg_ref[...])
    b = _dot(hn, wu_ref[...])
    hh = a * jax.nn.sigmoid(a) * b * ce
    y_ref[...] += _dot(hh.astype(BF16), wd_ref[...])


def _moe(h, hn, comb, w):
    N, D = h.shape
    E, _, Fd = w["wgate"].shape
    tm = next(t for t in (1024, 512, 256, 128, 64, 32, 16) if N % t == 0)
    return pl.pallas_call(
        functools.partial(_moe_kernel, n_groups=w["n_groups"]),
        grid=(N // tm, E),
        in_specs=[pl.BlockSpec((tm, D), lambda i, e: (i, 0)),
                  pl.BlockSpec((tm, D), lambda i, e: (i, 0)),
                  pl.BlockSpec((tm, LANES), lambda i, e: (i, 0)),
                  pl.BlockSpec((None, D, Fd), lambda i, e: (e, 0, 0)),
                  pl.BlockSpec((None, D, Fd), lambda i, e: (e, 0, 0)),
                  pl.BlockSpec((None, Fd, D), lambda i, e: (e, 0, 0))],
        out_specs=pl.BlockSpec((tm, D), lambda i, e: (i, 0)),
        out_shape=jax.ShapeDtypeStruct((N, D), F32),
        compiler_params=_cparams(("parallel", "arbitrary")), name="moe_experts",
    )(h, hn, comb, w["wgate"], w["wup"], w["wdown"])


def _prep_weights(norm1_g, w_in, qg, kg, w_sbb, w_dsab, w_out, norm2_g, w_group, w_router, w_gate, w_up, w_down,
                  dims):
    sb_w, dsa_w, kv_w, idx_w = dims
    D = w_in.shape[0]
    o_dv = 3 * sb_w + dsa_w + kv_w
    o_ik = o_dv + kv_w + idx_w
    o_iw = o_ik + HEAD_DIM
    o_g = o_iw + IDX_HEADS
    w_ik = w_in[:, o_ik:o_iw]
    n_groups = w_group.shape[1]
    n_experts = w_router.shape[1]
    assert n_groups + n_experts <= LANES
    wr = jnp.concatenate([w_group, w_router, jnp.zeros((D, LANES - n_groups - n_experts), F32)], axis=1)
    wr_hi, wr_lo = _split_bf16(wr)
    tile2 = lambda g: jnp.tile(g.astype(F32), LANES // HEAD_DIM)[None, :]
    head_id = jnp.arange(LANES) // HEAD_DIM
    return dict(
        g1=norm1_g.astype(F32)[None, :],
        wa=jnp.concatenate([w_in[:, :o_iw], w_ik], axis=1).astype(BF16),
        wg=w_in[:, o_g:].astype(BF16),
        wiwt=w_in[:, o_iw:o_g].T.astype(BF16),
        wdvt=w_in[:, o_dv:o_dv + kv_w].T.astype(BF16),
        qg=tile2(qg), kg=tile2(kg),
        gmat=((head_id[:, None] == head_id[None, :]).astype(F32) / HEAD_DIM).astype(BF16),
        wsb=w_sbb.astype(BF16), wdsa=w_dsab.astype(BF16), wout=w_out.astype(BF16),
        g2=norm2_g.astype(F32)[None, :], wr_hi=wr_hi, wr_lo=wr_lo,
        wgate=w_gate.astype(BF16), wup=w_up.astype(BF16), wdown=w_down.astype(BF16),
        n_groups=n_groups, per_group=n_experts // n_groups)


def _tail(x2d, osb, odsa, gsb, gdsa, w):
    h, hn, comb = _merge(x2d, osb, odsa, gsb, gdsa, w)
    return _moe(h, hn, comb, w)


def kernel(x_prompt, x_sample, cache_sb_k, cache_sb_v, cache_dsa_k, cache_dsa_v, cache_idx_k, meta_tokens, norm1_g, w_in, dsa_q_norm_g, dsa_k_norm_g, w_sb_branch, w_dsa_branch, w_out, norm2_g, w_group, w_router, w_gate, w_up, w_down):
    B, S, D = x_prompt.shape
    Bs, T, _ = x_sample.shape
    depth, _, past, sb_heads, hd = cache_sb_k.shape
    kv_heads = cache_dsa_k.shape[3]
    n_meta = meta_tokens.shape[0]
    assert depth == 1 and hd == HEAD_DIM and cache_idx_k.shape[-1] == HEAD_DIM
    assert past % SLAB == 0 and T <= BLK and S % CHUNK == 0
    sb_w = sb_heads * HEAD_DIM
    dsa_w = w_dsa_branch.shape[1]
    kv_w = kv_heads * HEAD_DIM
    idx_w = IDX_HEADS * HEAD_DIM
    assert kv_w == LANES and sb_w % LANES == 0 and dsa_w % LANES == 0
    dims = (sb_w, dsa_w, kv_w, idx_w)
    w = _prep_weights(norm1_g[0], w_in[0], dsa_q_norm_g[0], dsa_k_norm_g[0], w_sb_branch[0], w_dsa_branch[0],
                      w_out[0], norm2_g[0], w_group[0], w_router[0], w_gate[0], w_up[0], w_down[0], dims)

    Lr = n_meta + S
    Lp = -(-Lr // SLAB) * SLAB
    meta = jnp.broadcast_to(meta_tokens[None].astype(x_prompt.dtype), (B, n_meta, D))
    xp = jnp.concatenate([meta, x_prompt, jnp.zeros((B, Lp - Lr, D), x_prompt.dtype)], axis=1)
    pr = _project(xp, jnp.arange(Lp, dtype=jnp.int32), w, dims)
    o_sb = _sb_attention(pr["sqb"], pr["skb"], pr["svb"], 0)
    o_dsa = _dsa_attention(pr["dqb"], pr["iqb"], pr["iwt"], pr["dkb"], pr["dvt"], pr["ikb"], q0=0,
                           chunk_off=n_meta, n_valid=Lr, n_sel=min(TOPK_MAX, S // 4), kv_heads=kv_heads)
    flat = lambda a: a.reshape(B * Lp, a.shape[-1])
    yp = _tail(flat(xp), flat(o_sb), flat(o_dsa), flat(pr["gsb"]), flat(pr["gdsa"]), w)
    y_prompt = yp.reshape(B, Lp, D)[:, n_meta:Lr]

    xs = jnp.concatenate([x_sample, jnp.zeros((Bs, SLAB - T, D), x_sample.dtype)], axis=1)
    sr = _project(xs, past + jnp.arange(SLAB, dtype=jnp.int32), w, dims)
    cat = lambda c, new: jnp.concatenate([c.astype(BF16), new], axis=1)
    k_sb = cat(cache_sb_k[0].reshape(Bs, past, sb_w), sr["skb"])
    v_sb = cat(cache_sb_v[0].reshape(Bs, past, sb_w), sr["svb"])
    k_ds = cat(cache_dsa_k[0].reshape(Bs, past, kv_w), sr["dkb"])
    k_ix = cat(jnp.tile(cache_idx_k[0], (1, 1, LANES // HEAD_DIM)), sr["ikb"])
    vt_c = cache_dsa_v[0].reshape(Bs, past // SLAB, SLAB, kv_w).transpose(0, 1, 3, 2).astype(BF16)
    vt_ds = jnp.concatenate([vt_c, sr["dvt"]], axis=1)
    qb = past // SLAB
    o_sb_s = _sb_attention(sr["sqb"], k_sb, v_sb, qb)
    o_dsa_s = _dsa_attention(sr["dqb"], sr["iqb"], sr["iwt"], k_ds, vt_ds, k_ix, q0=qb,
                             chunk_off=0, n_valid=past + T, n_sel=min(TOPK_MAX, (past + T) // 4), kv_heads=kv_heads)
    real = lambda a: a[:, :T].reshape(Bs * T, a.shape[-1])
    ys = _tail(real(xs), real(o_sb_s), real(o_dsa_s), real(sr["gsb"]), real(sr["gdsa"]), w)
    y_sample = ys.reshape(Bs, T, D)

    heads = lambda a, L, n: a[:, :L].reshape(1, a.shape[0], L, n, HEAD_DIM)
    idx = lambda a, L: a[:, :L][None]
    return (y_prompt, y_sample,
            heads(pr["skf"], Lr, sb_heads), heads(pr["svf"], Lr, sb_heads),
            heads(pr["dkf"], Lr, kv_heads), heads(pr["dvf"], Lr, kv_heads), idx(pr["ikf"], Lr),
            heads(sr["skf"], T, sb_heads), heads(sr["svf"], T, sb_heads),
            heads(sr["dkf"], T, kv_heads), heads(sr["dvf"], T, kv_heads), idx(sr["ikf"], T))
```

```python
import functools

import jax
import jax.numpy as jnp
from jax import lax
from jax.experimental import pallas as pl
from jax.experimental.pallas import tpu as pltpu

F32 = jnp.float32
BF16 = jnp.bfloat16

CHUNK = 64
TOPK_MAX = 256
ROPE_THETA = 10000.0
EPS = 1e-6
HEAD_DIM = 64
IDX_HEADS = 8
EXPERT_TOPK = 2

LANES = 128
BLK = LANES
SLAB = 2 * BLK
SB_HEADS_PER_STEP = 4
MOE_EXPERTS_PER_STEP = 4
ROW_ALIGN = 16
VMEM_LIMIT = 56 * 1024 * 1024
INT_MIN = -(2 ** 31)
NEG_BIG = -1e30
LOG2E = 1.4426950408889634
SB_UNDERFLOW = -151.0


def _cparams(sem):
    return pltpu.CompilerParams(dimension_semantics=sem, vmem_limit_bytes=VMEM_LIMIT)


def _dot(a, b):
    return jnp.dot(a, b, preferred_element_type=F32)


def _dot_nt(a, b):
    return lax.dot_general(a, b, (((1,), (1,)), ((), ())), preferred_element_type=F32)


def _split_bf16(x):
    hi = x.astype(BF16)
    lo = (x - hi.astype(F32)).astype(BF16)
    return hi, lo


def _proj_kernel(x_ref, g1_ref, wa_ref, wg_ref, wiwt_ref, wdvt_ref, cos_ref, sin_ref,
                 qg_ref, kg_ref, gmat_ref,
                 sqb_ref, skf_ref, svf_ref, skb_ref, svb_ref, dqb_ref, dkf_ref, dvf_ref,
                 dkb_ref, dvt_ref, iqb_ref, ikf_ref, ikb_ref, iwt_ref, gsb_ref, gdsa_ref,
                 *, sb_w, dsa_w, kv_w, idx_w):
    x = x_ref[...]
    ms = jnp.mean(x * x, axis=-1, keepdims=True)
    xn = x * lax.rsqrt(ms + EPS) * g1_ref[...]
    xb = xn.astype(BF16)
    cos = cos_ref[...]
    sin = sin_ref[...]
    gmat = gmat_ref[...]
    lane = lax.broadcasted_iota(jnp.int32, cos.shape, 1)
    first_half = (lane % HEAD_DIM) < (HEAD_DIM // 2)

    def mm(lo, width):
        return _dot(xb, wa_ref[:, lo:lo + width])

    def head_norm(y, gain):
        hi, lo = _split_bf16(y * y)
        m = _dot(hi, gmat) + _dot(lo, gmat)
        return y * lax.rsqrt(m + EPS) * gain

    def rope(y):
        swapped = jnp.where(first_half, pltpu.roll(y, LANES - HEAD_DIM // 2, 1),
                            pltpu.roll(y, HEAD_DIM // 2, 1))
        return y * cos + swapped * sin

    scale = HEAD_DIM ** -0.5
    scale2 = scale * LOG2E
    off = 0
    sqb_ref[...] = (mm(off, sb_w) * scale2).astype(BF16)
    off += sb_w
    n_out = skf_ref.shape[0]
    sk = mm(off, sb_w)
    skf_ref[...] = sk[:n_out]
    skb_ref[...] = sk.astype(BF16)
    off += sb_w
    sv = mm(off, sb_w)
    svf_ref[...] = sv[:n_out]
    svb_ref[...] = sv.astype(BF16)
    off += sb_w
    for c in range(dsa_w // LANES):
        y = rope(head_norm(mm(off + c * LANES, LANES), qg_ref[...]))
        dqb_ref[:, c * LANES:(c + 1) * LANES] = (y * scale2).astype(BF16)
    off += dsa_w
    for c in range(kv_w // LANES):
        y = rope(head_norm(mm(off + c * LANES, LANES), kg_ref[...]))
        dkf_ref[:, c * LANES:(c + 1) * LANES] = y[:n_out]
        dkb_ref[:, c * LANES:(c + 1) * LANES] = y.astype(BF16)
    off += kv_w
    dvf_ref[...] = mm(off, kv_w)[:n_out]
    off += kv_w
    for c in range(idx_w // LANES):
        y = rope(mm(off + c * LANES, LANES))
        iqb_ref[:, c * LANES:(c + 1) * LANES] = (y * scale).astype(BF16)
    off += idx_w
    y = rope(mm(off, LANES))
    ikf_ref[...] = y[:n_out, :HEAD_DIM]
    ikb_ref[...] = y.astype(BF16)
    iwt_ref[...] = _dot_nt(wiwt_ref[...], xb)
    dvt_ref[...] = _dot_nt(wdvt_ref[...], xb).astype(BF16)
    d = gsb_ref.shape[-1]
    gsb_ref[...] = jax.nn.sigmoid(_dot(xb, wg_ref[:, :d]))
    gdsa_ref[...] = jax.nn.sigmoid(_dot(xb, wg_ref[:, d:]))


def _rope_tables(pos):
    half = HEAD_DIM // 2
    freqs = ROPE_THETA ** (-jnp.arange(half, dtype=F32) / half)
    ang = pos.astype(F32)[:, None] * freqs[None, :]
    cos, sin = jnp.cos(ang), jnp.sin(ang)
    cos_t = jnp.tile(cos, (1, LANES // half))
    sin_t = jnp.tile(jnp.concatenate([-sin, sin], axis=1), (1, LANES // HEAD_DIM))
    return cos_t, sin_t


def _project(xpad, pos, w, dims, n_real):
    B, L, D = xpad.shape
    sb_w, dsa_w, kv_w, idx_w = dims
    tm = SLAB
    cos_t, sin_t = _rope_tables(pos)
    grid = (B, L // tm)
    row = lambda width: pl.BlockSpec((None, tm, width), lambda b, i: (b, i, 0))
    full = lambda a: pl.BlockSpec(a.shape, lambda b, i: (0,) * a.ndim)
    tab = pl.BlockSpec((tm, LANES), lambda b, i: (i, 0))
    out_shapes = dict(
        sqb=(BF16, sb_w), skf=(F32, sb_w), svf=(F32, sb_w), skb=(BF16, sb_w), svb=(BF16, sb_w),
        dqb=(BF16, dsa_w), dkf=(F32, kv_w), dvf=(F32, kv_w), dkb=(BF16, kv_w))
    names = ["sqb", "skf", "svf", "skb", "svb", "dqb", "dkf", "dvf", "dkb", "dvt", "iqb", "ikf",
             "ikb", "iwt", "gsb", "gdsa"]
    shapes, specs = [], []
    f32_rows = min(tm, n_real)
    for n in names:
        if n in out_shapes:
            dt, width = out_shapes[n]
        elif n == "dvt":
            shapes.append(jax.ShapeDtypeStruct((B, L // SLAB, kv_w, SLAB), BF16))
            specs.append(pl.BlockSpec((None, None, kv_w, SLAB), lambda b, i: (b, i, 0, 0)))
            continue
        elif n == "iqb":
            dt, width = BF16, idx_w
        elif n == "ikf":
            dt, width = F32, HEAD_DIM
        elif n == "ikb":
            dt, width = BF16, LANES
        elif n == "iwt":
            shapes.append(jax.ShapeDtypeStruct((B, IDX_HEADS, L), F32))
            specs.append(pl.BlockSpec((None, IDX_HEADS, tm), lambda b, i: (b, 0, i)))
            continue
        else:
            dt, width = F32, D
        if n in ("skf", "svf", "dkf", "dvf", "ikf"):
            shapes.append(jax.ShapeDtypeStruct((B, n_real, width), dt))
            specs.append(pl.BlockSpec((None, f32_rows, width), lambda b, i: (b, i, 0)))
            continue
        shapes.append(jax.ShapeDtypeStruct((B, L, width), dt))
        specs.append(row(width))
    ins = [xpad, w["g1"], w["wa"], w["wg"], w["wiwt"], w["wdvt"], cos_t, sin_t, w["qg"], w["kg"], w["gmat"]]
    in_specs = [row(D), full(w["g1"]), full(w["wa"]), full(w["wg"]), full(w["wiwt"]), full(w["wdvt"]),
                tab, tab, full(w["qg"]), full(w["kg"]), full(w["gmat"])]
    outs = pl.pallas_call(
        functools.partial(_proj_kernel, sb_w=sb_w, dsa_w=dsa_w, kv_w=kv_w, idx_w=idx_w),
        grid=grid, in_specs=in_specs, out_specs=specs, out_shape=shapes,
        compiler_params=_cparams(("parallel", "parallel")), name="proj",
    )(*ins)
    return dict(zip(names, outs))


def _sb_kernel(q_ref, k_ref, v_ref, uo_ref, o_ref, *, q0):
    qi = q0 + pl.program_id(2)
    q = q_ref[...]
    uo = uo_ref[...]
    n_heads = q.shape[1] // HEAD_DIM
    per_slab = LANES // HEAD_DIM
    causal = (lax.broadcasted_iota(jnp.int32, (SLAB, SLAB), 1) < lax.broadcasted_iota(jnp.int32, (SLAB, SLAB), 0))
    v_lane_head = lax.broadcasted_iota(jnp.int32, (SLAB, LANES), 1) // HEAD_DIM

    def fold(j, cs, accs, diagonal):
        start = pl.multiple_of(j * SLAB, SLAB)
        kslab = k_ref[pl.ds(start, SLAB), :]
        vslab = v_ref[pl.ds(start, SLAB), :]
        cs_out, accs = [], list(accs)
        for h in range(n_heads):
            hs = slice(h * HEAD_DIM, (h + 1) * HEAD_DIM)
            z = _dot_nt(q[:, hs], kslab[:, hs])
            ls_pos = jnp.minimum(z, 0.0) - jnp.log2(1.0 + jnp.exp2(-jnp.abs(z)))
            ls_neg = ls_pos - z
            if diagonal:
                ls_neg = jnp.where(causal, ls_neg, 0.0)
            hi, lo = _split_bf16(ls_neg)
            r = _dot(hi, uo) + _dot(lo, uo)
            wgt = jnp.exp2(ls_pos + r[:, :SLAB] + jnp.concatenate([cs[h]] * (SLAB // BLK), axis=1))
            if diagonal:
                wgt = jnp.where(causal, wgt, 0.0)
            vs = vslab[:, (h // per_slab) * LANES:(h // per_slab + 1) * LANES]
            vh = jnp.where(v_lane_head == h % per_slab, vs, jnp.zeros_like(vs))
            accs[h // per_slab] = accs[h // per_slab] + _dot(wgt.astype(BF16), vh)
            cs_out.append(cs[h] + r[:, SLAB:])
        cmax = functools.reduce(jnp.maximum, [jnp.max(c) for c in cs_out])
        return tuple(cs_out), tuple(accs), cmax

    zeros = jnp.zeros((SLAB, BLK), F32)
    cs, accs, cmax = fold(qi, (zeros,) * n_heads, (jnp.zeros((SLAB, LANES), F32),) * (n_heads // per_slab), True)

    def body(carry):
        j, cs, accs, _ = carry
        cs, accs, cmax = fold(j, cs, accs, False)
        return j - 1, cs, accs, cmax

    def cond(carry):
        j, _, _, cmax = carry
        return jnp.logical_and(j >= 0, cmax > SB_UNDERFLOW)

    _, _, accs, _ = lax.while_loop(cond, body, (qi - 1, cs, accs, cmax))
    o_ref[...] = jnp.concatenate(accs, axis=1).astype(o_ref.dtype)


def _sb_attention(q, k, v, q0):
    B, Lq, W = q.shape
    Lk = k.shape[1]
    uo = jnp.concatenate([jnp.tril(jnp.ones((SLAB, SLAB), F32), -1), jnp.ones((SLAB, BLK), F32)], axis=1).astype(BF16)
    wb = SB_HEADS_PER_STEP * HEAD_DIM
    grid = (B, W // wb, Lq // SLAB)
    return pl.pallas_call(
        functools.partial(_sb_kernel, q0=q0),
        grid=grid,
        in_specs=[pl.BlockSpec((None, SLAB, wb), lambda b, h, i: (b, i, h)),
                  pl.BlockSpec((None, Lk, wb), lambda b, h, i: (b, 0, h)),
                  pl.BlockSpec((None, Lk, wb), lambda b, h, i: (b, 0, h)),
                  pl.BlockSpec(uo.shape, lambda b, h, i: (0, 0))],
        out_specs=pl.BlockSpec((None, SLAB, wb), lambda b, h, i: (b, i, h)),
        out_shape=jax.ShapeDtypeStruct((B, Lq, W), BF16),
        compiler_params=_cparams(("parallel", "parallel", "arbitrary")), name="sb_attn",
    )(q, k, v, uo)


def _dsa_kernel(q_ref, iq_ref, iwt_ref, k_ref, vt_ref, ik_ref, lt_ref, o_ref,
                keys_ref, hi16_ref, lo16_ref, acc_ref, *, q_pos0, nslab_total, chunk_off, n_valid, n_sel, n_heads,
                kv_heads):
    tq = q_ref.shape[0]
    q_start = q_pos0 + pl.program_id(1) * tq
    nslab = jnp.minimum((q_start + tq + CHUNK - 1 + SLAB - 1) // SLAB, nslab_total)
    group = n_heads // kv_heads
    idx_w_scale = IDX_HEADS ** -0.5

    qpos = q_start + lax.broadcasted_iota(jnp.int32, (1, tq), 1)
    chunk_end = chunk_off + CHUNK * (jnp.right_shift(qpos - chunk_off, CHUNK.bit_length() - 1) + 1)
    key_limit = jnp.minimum(jnp.where(qpos < chunk_off, chunk_off, chunk_end), n_valid)
    slab_row = lax.broadcasted_iota(jnp.int32, (SLAB, tq), 0)

    iwt = iwt_ref[...]
    by_head = lambda x, heads: jnp.concatenate([x[:, h * HEAD_DIM:(h + 1) * HEAD_DIM] for h in heads], axis=0)
    iq_rows = by_head(iq_ref[...], range(IDX_HEADS))

    def score_slab(j, _):
        start = pl.multiple_of(j * SLAB, SLAB)
        s = _dot_nt(ik_ref[pl.ds(start, SLAB), :][:, :HEAD_DIM], iq_rows)
        acc = jnp.zeros((SLAB, tq), F32)
        for h in range(IDX_HEADS):
            acc = acc + jnp.maximum(s[:, h * tq:(h + 1) * tq], 0.0) * iwt[h:h + 1, :]
        bits = lax.bitcast_convert_type(acc * idx_w_scale, jnp.int32)
        key = bits ^ ((bits >> 31) & 0x7FFFFFFF)
        key = jnp.where(slab_row < key_limit - start, key, INT_MIN)
        keys_ref[pl.ds(start, SLAB), :] = key
        hi16_ref[pl.ds(start, SLAB), :] = (key >> 16).astype(jnp.int16)
        lo16_ref[pl.ds(start, SLAB), :] = ((key & 0xFFFF) - 2 ** 15).astype(jnp.int16)
        return 0

    lax.fori_loop(0, nslab // 2, lambda t, _: score_slab(2 * t + 1, score_slab(2 * t, 0)), 0)
    lax.fori_loop(2 * (nslab // 2), nslab, score_slab, 0)

    rows16 = 2 * 8
    one16 = jnp.ones((SLAB // rows16, rows16, BLK), jnp.int16)

    def tile16(c):
        word = (c & 0xFFFF) | (c << 16)
        return pltpu.bitcast(jnp.broadcast_to(word, (8, BLK)), jnp.int16)

    def select_threshold(lanes):
        def slab16(ref, j):
            start = pl.multiple_of(j * SLAB, SLAB)
            return ref[pl.ds(start, SLAB), lanes].reshape(SLAB // rows16, rows16, BLK)

        def count(ref, cand, strict):
            cand16 = tile16(cand)[None]

            def body(j, cnt):
                x = slab16(ref, j)
                hit = (x > cand16) if strict else (x >= cand16)
                return cnt + jnp.where(hit, one16, jnp.zeros_like(one16))
            cnt = lax.fori_loop(0, nslab, body, jnp.zeros_like(one16))
            per_row = functools.reduce(lambda a, b: a + b, [cnt[i] for i in range(SLAB // rows16)])
            return jnp.sum(per_row.astype(F32), axis=0, keepdims=True)

        def search16(ref, above):
            def step(it, t):
                cand = t + jnp.left_shift(jnp.int32(1), 15 - it)
                return jnp.where(above + count(ref, cand, False) >= n_sel, cand, t)
            return lax.fori_loop(0, 16, step, jnp.full((1, BLK), -(2 ** 15), jnp.int32))

        t_hi = search16(hi16_ref, 0.0)
        above_hi = count(hi16_ref, t_hi, True)
        t_hi16 = tile16(t_hi)[None]

        def park(j, _):
            start = pl.multiple_of(j * SLAB, SLAB)
            lo = jnp.where(slab16(hi16_ref, j) == t_hi16, slab16(lo16_ref, j), jnp.full_like(one16, -(2 ** 15)))
            lo16_ref[pl.ds(start, SLAB), lanes] = lo.reshape(SLAB, BLK)
            return 0

        lax.fori_loop(0, nslab, park, 0)
        t_lo = search16(lo16_ref, above_hi)
        thr = t_hi * (2 ** 16) + (t_lo + 2 ** 15)
        return thr, n_sel - (above_hi + count(lo16_ref, t_lo, True))

    halves = [select_threshold(slice(c * BLK, (c + 1) * BLK)) for c in range(tq // BLK)]
    thr = jnp.concatenate([t for t, _ in halves], axis=1)
    room = jnp.concatenate([r for _, r in halves], axis=1)

    acc_ref[...] = jnp.zeros(acc_ref.shape, F32)
    q = q_ref[...]
    q_rows = [by_head(q, range(kvh * group, (kvh + 1) * group)) for kvh in range(kv_heads)]
    lt = lt_ref[...]

    def attend_slab(j, carry):
        eq_seen, m_all, l_all = carry
        start = pl.multiple_of(j * SLAB, SLAB)
        kk = keys_ref[pl.ds(start, SLAB), :]
        eq = kk == thr
        prefix = _dot(lt, jnp.where(eq, 1.0, 0.0).astype(BF16)) + eq_seen
        sel = jnp.logical_or(kk > thr, jnp.logical_and(eq, prefix <= room))
        sel = jnp.logical_and(sel, kk != INT_MIN)
        bias = jnp.where(sel, 0.0, NEG_BIG)
        kj = k_ref[pl.ds(start, SLAB), :]
        vtj = vt_ref[j]
        m_out, l_out = [], []
        for kvh in range(kv_heads):
            s_all = _dot_nt(kj[:, kvh * HEAD_DIM:(kvh + 1) * HEAD_DIM], q_rows[kvh])
            ps, alphas = [], []
            for g in range(group):
                h = kvh * group + g
                s = s_all[:, g * tq:(g + 1) * tq] + bias
                m_new = jnp.maximum(m_all[h], jnp.max(s, axis=0, keepdims=True))
                p = jnp.exp2(s - m_new)
                alpha = jnp.exp2(m_all[h] - m_new)
                l_out.append(alpha * l_all[h] + jnp.sum(p, axis=0, keepdims=True))
                m_out.append(m_new)
                ps.append(p.astype(BF16))
                alphas.append(alpha)
            pv = _dot(vtj[kvh * HEAD_DIM:(kvh + 1) * HEAD_DIM, :], jnp.concatenate(ps, axis=1))
            acc_ref[kvh] = jnp.concatenate(alphas, axis=1) * acc_ref[kvh] + pv
        return prefix[SLAB - 1:SLAB, :], tuple(m_out), tuple(l_out)

    init = (jnp.zeros((1, tq), F32), (jnp.full((1, tq), NEG_BIG, F32),) * n_heads,
            (jnp.zeros((1, tq), F32),) * n_heads)
    carry = lax.fori_loop(0, nslab // 2, lambda t, c: attend_slab(2 * t + 1, attend_slab(2 * t, c)), init)
    _, _, l_all = lax.fori_loop(2 * (nslab // 2), nslab, attend_slab, carry)

    outs = []
    for kvh in range(kv_heads):
        acc = acc_ref[kvh]
        for g in range(group):
            outs.append(acc[:, g * tq:(g + 1) * tq] / l_all[kvh * group + g])
    o_ref[...] = jnp.concatenate(outs, axis=0).T.astype(o_ref.dtype)


def _dsa_attention(q, iq, iwt, k, vt, ik, *, tq, n_q, q_pos0, chunk_off, n_valid, n_sel, kv_heads):
    B, _, W = q.shape
    Lq = n_q * tq
    Lk = k.shape[1]
    n_heads = W // HEAD_DIM
    lt = jnp.tril(jnp.ones((SLAB, SLAB), F32)).astype(BF16)
    kern = functools.partial(_dsa_kernel, q_pos0=q_pos0, nslab_total=Lk // SLAB, chunk_off=chunk_off, n_valid=n_valid,
                             n_sel=n_sel, n_heads=n_heads, kv_heads=kv_heads)
    return pl.pallas_call(
        kern,
        grid=(B, n_q),
        in_specs=[pl.BlockSpec((None, tq, W), lambda b, i: (b, i, 0)),
                  pl.BlockSpec((None, tq, iq.shape[2]), lambda b, i: (b, i, 0)),
                  pl.BlockSpec((None, IDX_HEADS, tq), lambda b, i: (b, 0, i)),
                  pl.BlockSpec((None, Lk, k.shape[2]), lambda b, i: (b, 0, 0)),
                  pl.BlockSpec((None,) + vt.shape[1:], lambda b, i: (b, 0, 0, 0)),
                  pl.BlockSpec((None, Lk, ik.shape[2]), lambda b, i: (b, 0, 0)),
                  pl.BlockSpec(lt.shape, lambda b, i: (0, 0))],
        out_specs=pl.BlockSpec((None, tq, W), lambda b, i: (b, i, 0)),
        out_shape=jax.ShapeDtypeStruct((B, Lq, W), BF16),
        scratch_shapes=[pltpu.VMEM((Lk, tq), jnp.int32),
                        pltpu.VMEM((Lk, tq), jnp.int16), pltpu.VMEM((Lk, tq), jnp.int16),
                        pltpu.VMEM((kv_heads, HEAD_DIM, (n_heads // kv_heads) * tq), F32)],
        compiler_params=_cparams(("parallel", "arbitrary")), name="dsa_attn",
    )(q, iq, iwt, k, vt, ik, lt)


def _merge_kernel(x_ref, osb_ref, odsa_ref, gsb_ref, gdsa_ref, wsb_ref, wdsa_ref, wout_ref, g2_ref,
                  wr_hi_ref, wr_lo_ref, h_ref, hn_ref, comb_ref, *, n_groups, per_group):
    merged = gsb_ref[0] * _dot(osb_ref[0], wsb_ref[...]) + gdsa_ref[0] * _dot(odsa_ref[0], wdsa_ref[...])
    h = x_ref[...] + _dot(merged.astype(BF16), wout_ref[...])
    h_ref[...] = h
    ms = jnp.mean(h * h, axis=-1, keepdims=True)
    hn = h * lax.rsqrt(ms + EPS) * g2_ref[...]
    hn_ref[...] = hn.astype(BF16)

    hi, lo = _split_bf16(hn)
    logits = _dot(hi, wr_hi_ref[...]) + _dot(lo, wr_hi_ref[...]) + _dot(hi, wr_lo_ref[...])
    lane = lax.broadcasted_iota(jnp.int32, logits.shape, 1)
    big = jnp.int32(LANES)
    neg_inf = -jnp.inf

    def first_argmax(vals):
        top = jnp.max(vals, axis=-1, keepdims=True)
        idx = jnp.min(jnp.where(vals == top, lane, big), axis=-1, keepdims=True)
        return top, idx

    is_group = lane < n_groups
    gl = jnp.where(is_group, logits, neg_inf)
    ge = jnp.exp(gl - jnp.max(gl, axis=-1, keepdims=True))
    probs = jnp.where(is_group, ge / jnp.sum(ge, axis=-1, keepdims=True), neg_inf)
    gp, gi = first_argmax(probs)

    expert = lane - n_groups
    in_group = jnp.logical_and(expert >= gi * per_group, expert < (gi + 1) * per_group)
    vals = jnp.where(in_group, logits, neg_inf)
    ev0, i0 = first_argmax(vals)
    ev1, i1 = first_argmax(jnp.where(lane == i0, neg_inf, vals))
    e1 = jnp.exp(ev1 - ev0)
    w0 = gp / (1.0 + e1)
    w1 = gp * e1 / (1.0 + e1)
    comb_ref[...] = jnp.where(lane == i0, w0, jnp.where(lane == i1, w1, 0.0))


def _merge(x, osb, odsa, gsb, gdsa, w, row0):
    B, R, D = x.shape
    tm = next(t for t in (512, 256, 128, 64, 32, 16) if R % t == 0)
    nt = R // tm
    shifted = lambda a: pl.BlockSpec((pl.Element(1), pl.Element(tm), pl.Element(a.shape[2])),
                                     lambda b, i: (b, pl.multiple_of(row0 + i * tm, ROW_ALIGN), 0))
    full = lambda a: pl.BlockSpec(a.shape, lambda b, i: (0,) * a.ndim)
    out = lambda width: pl.BlockSpec((tm, width), lambda b, i: (b * nt + i, 0))
    weights = [w["wsb"], w["wdsa"], w["wout"], w["g2"], w["wr_hi"], w["wr_lo"]]
    in_specs = ([pl.BlockSpec((None, tm, D), lambda b, i: (b, i, 0))] + [shifted(a) for a in (osb, odsa, gsb, gdsa)]
                + [full(a) for a in weights])
    return pl.pallas_call(
        functools.partial(_merge_kernel, n_groups=w["n_groups"], per_group=w["per_group"]),
        grid=(B, nt), in_specs=in_specs,
        out_specs=[out(D), out(D), out(LANES)],
        out_shape=[jax.ShapeDtypeStruct((B * R, D), F32), jax.ShapeDtypeStruct((B * R, D), BF16),
                   jax.ShapeDtypeStruct((B * R, LANES), F32)],
        compiler_params=_cparams(("parallel", "parallel")), name="merge_route",
    )(x, osb, odsa, gsb, gdsa, *weights)


def _moe_kernel(h_ref, hn_ref, comb_ref, wg_ref, wu_ref, wd_ref, y_ref, hh_ref, *, n_groups):
    step = pl.program_id(1)
    n_exp, _, fd = wg_ref.shape

    @pl.when(step == 0)
    def _():
        y_ref[...] = h_ref[...]

    hn = hn_ref[...]
    comb = comb_ref[...]
    lane = lax.broadcasted_iota(jnp.int32, comb.shape, 1)
    for e in range(n_exp):
        ce = jnp.sum(jnp.where(lane == n_groups + step * n_exp + e, comb, 0.0), axis=-1, keepdims=True)
        a = _dot(hn, wg_ref[e])
        b = _dot(hn, wu_ref[e])
        hh_ref[:, e * fd:(e + 1) * fd] = (a * jax.nn.sigmoid(a) * b * ce).astype(BF16)
    y_ref[...] += _dot(hh_ref[...], wd_ref[...])


def _moe(h, hn, comb, w):
    N, D = h.shape
    E, _, Fd = w["wgate"].shape
    tm = next(t for t in (1024, 512, 256, 128, 64, 32, 16) if N % t == 0)
    ne = MOE_EXPERTS_PER_STEP
    assert E % ne == 0
    return pl.pallas_call(
        functools.partial(_moe_kernel, n_groups=w["n_groups"]),
        grid=(N // tm, E // ne),
        in_specs=[pl.BlockSpec((tm, D), lambda i, e: (i, 0)),
                  pl.BlockSpec((tm, D), lambda i, e: (i, 0)),
                  pl.BlockSpec((tm, LANES), lambda i, e: (i, 0)),
                  pl.BlockSpec((ne, D, Fd), lambda i, e: (e, 0, 0)),
                  pl.BlockSpec((ne, D, Fd), lambda i, e: (e, 0, 0)),
                  pl.BlockSpec((ne * Fd, D), lambda i, e: (e, 0))],
        out_specs=pl.BlockSpec((tm, D), lambda i, e: (i, 0)),
        out_shape=jax.ShapeDtypeStruct((N, D), F32),
        scratch_shapes=[pltpu.VMEM((tm, ne * Fd), BF16)],
        compiler_params=_cparams(("parallel", "arbitrary")), name="moe_experts",
    )(h, hn, comb, w["wgate"], w["wup"], w["wdown"].reshape(E * Fd, D))


def _prep_weights(norm1_g, w_in, qg, kg, w_sbb, w_dsab, w_out, norm2_g, w_group, w_router, w_gate, w_up, w_down,
                  dims):
    sb_w, dsa_w, kv_w, idx_w = dims
    D = w_in.shape[0]
    o_dv = 3 * sb_w + dsa_w + kv_w
    o_ik = o_dv + kv_w + idx_w
    o_iw = o_ik + HEAD_DIM
    o_g = o_iw + IDX_HEADS
    w_ik = w_in[:, o_ik:o_iw]
    n_groups = w_group.shape[1]
    n_experts = w_router.shape[1]
    assert n_groups + n_experts <= LANES
    wr = jnp.concatenate([w_group, w_router, jnp.zeros((D, LANES - n_groups - n_experts), F32)], axis=1)
    wr_hi, wr_lo = _split_bf16(wr)
    tile2 = lambda g: jnp.tile(g.astype(F32), LANES // HEAD_DIM)[None, :]
    head_id = jnp.arange(LANES) // HEAD_DIM
    return dict(
        g1=norm1_g.astype(F32)[None, :],
        wa=jnp.concatenate([w_in[:, :o_iw], w_ik], axis=1).astype(BF16),
        wg=w_in[:, o_g:].astype(BF16),
        wiwt=w_in[:, o_iw:o_g].T.astype(BF16),
        wdvt=w_in[:, o_dv:o_dv + kv_w].T.astype(BF16),
        qg=tile2(qg), kg=tile2(kg),
        gmat=((head_id[:, None] == head_id[None, :]).astype(F32) / HEAD_DIM).astype(BF16),
        wsb=w_sbb.astype(BF16), wdsa=w_dsab.astype(BF16), wout=w_out.astype(BF16),
        g2=norm2_g.astype(F32)[None, :], wr_hi=wr_hi, wr_lo=wr_lo,
        wgate=w_gate.astype(BF16), wup=w_up.astype(BF16), wdown=w_down.astype(BF16),
        n_groups=n_groups, per_group=n_experts // n_groups)


def _tail(x, osb, odsa, gsb, gdsa, w, row0):
    h, hn, comb = _merge(x, osb, odsa, gsb, gdsa, w, row0)
    return _moe(h, hn, comb, w).reshape(x.shape)


def kernel(x_prompt, x_sample, cache_sb_k, cache_sb_v, cache_dsa_k, cache_dsa_v, cache_idx_k, meta_tokens, norm1_g, w_in, dsa_q_norm_g, dsa_k_norm_g, w_sb_branch, w_dsa_branch, w_out, norm2_g, w_group, w_router, w_gate, w_up, w_down):
    B, S, D = x_prompt.shape
    Bs, T, _ = x_sample.shape
    depth, _, past, sb_heads, hd = cache_sb_k.shape
    kv_heads = cache_dsa_k.shape[3]
    n_meta = meta_tokens.shape[0]
    assert depth == 1 and hd == HEAD_DIM and cache_idx_k.shape[-1] == HEAD_DIM
    assert past % SLAB == 0 and T <= BLK and S % CHUNK == 0
    sb_w = sb_heads * HEAD_DIM
    dsa_w = w_dsa_branch.shape[1]
    kv_w = kv_heads * HEAD_DIM
    idx_w = IDX_HEADS * HEAD_DIM
    assert kv_w == LANES and sb_w % LANES == 0 and dsa_w % LANES == 0
    dims = (sb_w, dsa_w, kv_w, idx_w)
    w = _prep_weights(norm1_g[0], w_in[0], dsa_q_norm_g[0], dsa_k_norm_g[0], w_sb_branch[0], w_dsa_branch[0],
                      w_out[0], norm2_g[0], w_group[0], w_router[0], w_gate[0], w_up[0], w_down[0], dims)

    Lr = n_meta + S
    Lp = -(-Lr // SLAB) * SLAB
    meta = jnp.broadcast_to(meta_tokens[None].astype(x_prompt.dtype), (B, n_meta, D))
    xp = jnp.concatenate([meta, x_prompt, jnp.zeros((B, Lp - Lr, D), x_prompt.dtype)], axis=1)
    pr = _project(xp, jnp.arange(Lp, dtype=jnp.int32), w, dims, Lr)
    o_sb = _sb_attention(pr["sqb"], pr["skb"], pr["svb"], 0)
    o_dsa = _dsa_attention(pr["dqb"], pr["iqb"], pr["iwt"], pr["dkb"], pr["dvt"], pr["ikb"], tq=SLAB, n_q=Lp // SLAB,
                           q_pos0=0, chunk_off=n_meta, n_valid=Lr, n_sel=min(TOPK_MAX, S // 4), kv_heads=kv_heads)
    y_prompt = _tail(x_prompt, o_sb, o_dsa, pr["gsb"], pr["gdsa"], w, n_meta)

    xs = jnp.concatenate([x_sample, jnp.zeros((Bs, SLAB - T, D), x_sample.dtype)], axis=1)
    sr = _project(xs, past + jnp.arange(SLAB, dtype=jnp.int32), w, dims, T)
    cat = lambda c, new: jnp.concatenate([c.astype(BF16), new], axis=1)
    k_sb = cat(cache_sb_k[0].reshape(Bs, past, sb_w), sr["skb"])
    v_sb = cat(cache_sb_v[0].reshape(Bs, past, sb_w), sr["svb"])
    k_ds = cat(cache_dsa_k[0].reshape(Bs, past, kv_w), sr["dkb"])
    k_ix = cat(jnp.tile(cache_idx_k[0], (1, 1, LANES // HEAD_DIM)), sr["ikb"])
    vt_c = cache_dsa_v[0].reshape(Bs, past // SLAB, SLAB, kv_w).transpose(0, 1, 3, 2).astype(BF16)
    vt_ds = jnp.concatenate([vt_c, sr["dvt"]], axis=1)
    qb = past // SLAB
    o_sb_s = _sb_attention(sr["sqb"], k_sb, v_sb, qb)
    o_dsa_s = _dsa_attention(sr["dqb"], sr["iqb"], sr["iwt"], k_ds, vt_ds, k_ix, tq=BLK, n_q=1, q_pos0=past,
                             chunk_off=0, n_valid=past + T, n_sel=min(TOPK_MAX, (past + T) // 4), kv_heads=kv_heads)
    y_sample = _tail(x_sample, o_sb_s, o_dsa_s, sr["gsb"], sr["gdsa"], w, 0)

    heads = lambda a, n: a.reshape(1, a.shape[0], a.shape[1], n, HEAD_DIM)
    return (y_prompt, y_sample,
            heads(pr["skf"], sb_heads), heads(pr["svf"], sb_heads),
            heads(pr["dkf"], kv_heads), heads(pr["dvf"], kv_heads), pr["ikf"][None],
            heads(sr["skf"], sb_heads), heads(sr["svf"], sb_heads),
            heads(sr["dkf"], kv_heads), heads(sr["dvf"], kv_heads), sr["ikf"][None])
```

```python
import functools

import jax
import jax.numpy as jnp
from jax import lax
from jax.experimental import pallas as pl
from jax.experimental.pallas import tpu as pltpu

F32 = jnp.float32
BF16 = jnp.bfloat16

CHUNK = 64
TOPK_MAX = 256
ROPE_THETA = 10000.0
EPS = 1e-6
HEAD_DIM = 64
IDX_HEADS = 8
EXPERT_TOPK = 2

LANES = 128
BLK = LANES
SLAB = 2 * BLK
SB_HEADS_PER_STEP = 4
MOE_EXPERTS_PER_STEP = 4
ROW_ALIGN = 16
VMEM_LIMIT = 56 * 1024 * 1024
INT_MIN = -(2 ** 31)
NEG_BIG = -1e30
LOG2E = 1.4426950408889634
SB_UNDERFLOW = -151.0


def _cparams(sem):
    return pltpu.CompilerParams(dimension_semantics=sem, vmem_limit_bytes=VMEM_LIMIT)


def _dot(a, b):
    return jnp.dot(a, b, preferred_element_type=F32)


def _dot_nt(a, b):
    return lax.dot_general(a, b, (((1,), (1,)), ((), ())), preferred_element_type=F32)


def _split_bf16(x):
    hi = x.astype(BF16)
    lo = (x - hi.astype(F32)).astype(BF16)
    return hi, lo


def _proj_kernel(x_ref, g1_ref, wa_ref, wg_ref, wiwt_ref, wdvt_ref, cos_ref, sin_ref,
                 qg_ref, kg_ref, gmat_ref,
                 sqb_ref, skf_ref, svf_ref, skb_ref, svb_ref, dqb_ref, dkf_ref, dvf_ref,
                 dkb_ref, dvt_ref, iqb_ref, ikf_ref, ikb_ref, iwt_ref, gsb_ref, gdsa_ref,
                 *, sb_w, dsa_w, kv_w, idx_w):
    x = x_ref[...]
    ms = jnp.mean(x * x, axis=-1, keepdims=True)
    xn = x * lax.rsqrt(ms + EPS) * g1_ref[...]
    xb = xn.astype(BF16)
    cos = cos_ref[...]
    sin = sin_ref[...]
    gmat = gmat_ref[...]
    lane = lax.broadcasted_iota(jnp.int32, cos.shape, 1)
    first_half = (lane % HEAD_DIM) < (HEAD_DIM // 2)

    def mm(lo, width):
        return _dot(xb, wa_ref[:, lo:lo + width])

    def head_norm(y, gain):
        hi, lo = _split_bf16(y * y)
        m = _dot(hi, gmat) + _dot(lo, gmat)
        return y * lax.rsqrt(m + EPS) * gain

    def rope(y):
        swapped = jnp.where(first_half, pltpu.roll(y, LANES - HEAD_DIM // 2, 1),
                            pltpu.roll(y, HEAD_DIM // 2, 1))
        return y * cos + swapped * sin

    scale = HEAD_DIM ** -0.5
    scale2 = scale * LOG2E
    off = 0
    sqb_ref[...] = (mm(off, sb_w) * scale2).astype(BF16)
    off += sb_w
    n_out = skf_ref.shape[0]
    sk = mm(off, sb_w)
    skf_ref[...] = sk[:n_out]
    skb_ref[...] = sk.astype(BF16)
    off += sb_w
    sv = mm(off, sb_w)
    svf_ref[...] = sv[:n_out]
    svb_ref[...] = sv.astype(BF16)
    off += sb_w
    for c in range(dsa_w // LANES):
        y = rope(head_norm(mm(off + c * LANES, LANES), qg_ref[...]))
        dqb_ref[:, c * LANES:(c + 1) * LANES] = (y * scale2).astype(BF16)
    off += dsa_w
    for c in range(kv_w // LANES):
        y = rope(head_norm(mm(off + c * LANES, LANES), kg_ref[...]))
        dkf_ref[:, c * LANES:(c + 1) * LANES] = y[:n_out]
        dkb_ref[:, c * LANES:(c + 1) * LANES] = y.astype(BF16)
    off += kv_w
    dvf_ref[...] = mm(off, kv_w)[:n_out]
    off += kv_w
    for c in range(idx_w // LANES):
        y = rope(mm(off + c * LANES, LANES))
        iqb_ref[:, c * LANES:(c + 1) * LANES] = (y * scale).astype(BF16)
    off += idx_w
    y = rope(mm(off, LANES))
    ikf_ref[...] = y[:n_out, :HEAD_DIM]
    ikb_ref[...] = y.astype(BF16)
    iwt_ref[...] = _dot_nt(wiwt_ref[...], xb)
    dvt_ref[...] = _dot_nt(wdvt_ref[...], xb).astype(BF16)
    d = gsb_ref.shape[-1]
    gsb_ref[...] = jax.nn.sigmoid(_dot(xb, wg_ref[:, :d]))
    gdsa_ref[...] = jax.nn.sigmoid(_dot(xb, wg_ref[:, d:]))


def _rope_tables(pos):
    half = HEAD_DIM // 2
    freqs = ROPE_THETA ** (-jnp.arange(half, dtype=F32) / half)
    ang = pos.astype(F32)[:, None] * freqs[None, :]
    cos, sin = jnp.cos(ang), jnp.sin(ang)
    cos_t = jnp.tile(cos, (1, LANES // half))
    sin_t = jnp.tile(jnp.concatenate([-sin, sin], axis=1), (1, LANES // HEAD_DIM))
    return cos_t, sin_t


def _project(xpad, pos, w, dims, n_real):
    B, L, D = xpad.shape
    sb_w, dsa_w, kv_w, idx_w = dims
    tm = SLAB
    cos_t, sin_t = _rope_tables(pos)
    grid = (B, L // tm)
    row = lambda width: pl.BlockSpec((None, tm, width), lambda b, i: (b, i, 0))
    full = lambda a: pl.BlockSpec(a.shape, lambda b, i: (0,) * a.ndim)
    tab = pl.BlockSpec((tm, LANES), lambda b, i: (i, 0))
    out_shapes = dict(
        sqb=(BF16, sb_w), skf=(F32, sb_w), svf=(F32, sb_w), skb=(BF16, sb_w), svb=(BF16, sb_w),
        dqb=(BF16, dsa_w), dkf=(F32, kv_w), dvf=(F32, kv_w), dkb=(BF16, kv_w))
    names = ["sqb", "skf", "svf", "skb", "svb", "dqb", "dkf", "dvf", "dkb", "dvt", "iqb", "ikf",
             "ikb", "iwt", "gsb", "gdsa"]
    shapes, specs = [], []
    f32_rows = min(tm, n_real)
    for n in names:
        if n in out_shapes:
            dt, width = out_shapes[n]
        elif n == "dvt":
            shapes.append(jax.ShapeDtypeStruct((B, L // SLAB, kv_w, SLAB), BF16))
            specs.append(pl.BlockSpec((None, None, kv_w, SLAB), lambda b, i: (b, i, 0, 0)))
            continue
        elif n == "iqb":
            dt, width = BF16, idx_w
        elif n == "ikf":
            dt, width = F32, HEAD_DIM
        elif n == "ikb":
            dt, width = BF16, LANES
        elif n == "iwt":
            shapes.append(jax.ShapeDtypeStruct((B, IDX_HEADS, L), F32))
            specs.append(pl.BlockSpec((None, IDX_HEADS, tm), lambda b, i: (b, 0, i)))
            continue
        else:
            dt, width = F32, D
        if n in ("skf", "svf", "dkf", "dvf", "ikf"):
            shapes.append(jax.ShapeDtypeStruct((B, n_real, width), dt))
            specs.append(pl.BlockSpec((None, f32_rows, width), lambda b, i: (b, i, 0)))
            continue
        shapes.append(jax.ShapeDtypeStruct((B, L, width), dt))
        specs.append(row(width))
    ins = [xpad, w["g1"], w["wa"], w["wg"], w["wiwt"], w["wdvt"], cos_t, sin_t, w["qg"], w["kg"], w["gmat"]]
    in_specs = [row(D), full(w["g1"]), full(w["wa"]), full(w["wg"]), full(w["wiwt"]), full(w["wdvt"]),
                tab, tab, full(w["qg"]), full(w["kg"]), full(w["gmat"])]
    outs = pl.pallas_call(
        functools.partial(_proj_kernel, sb_w=sb_w, dsa_w=dsa_w, kv_w=kv_w, idx_w=idx_w),
        grid=grid, in_specs=in_specs, out_specs=specs, out_shape=shapes,
        compiler_params=_cparams(("parallel", "parallel")), name="proj",
    )(*ins)
    return dict(zip(names, outs))


def _sb_kernel(q_ref, k_ref, v_ref, uo_ref, o_ref, *, q0):
    qi = q0 + pl.program_id(2)
    q = q_ref[...]
    uo = uo_ref[...]
    n_heads = q.shape[1] // HEAD_DIM
    per_slab = LANES // HEAD_DIM
    causal = (lax.broadcasted_iota(jnp.int32, (SLAB, SLAB), 1) < lax.broadcasted_iota(jnp.int32, (SLAB, SLAB), 0))
    v_lane_head = lax.broadcasted_iota(jnp.int32, (SLAB, LANES), 1) // HEAD_DIM

    def fold(j, cs, accs, diagonal):
        start = pl.multiple_of(j * SLAB, SLAB)
        kslab = k_ref[pl.ds(start, SLAB), :]
        vslab = v_ref[pl.ds(start, SLAB), :]
        cs_out, accs = [], list(accs)
        for h in range(n_heads):
            hs = slice(h * HEAD_DIM, (h + 1) * HEAD_DIM)
            z = _dot_nt(q[:, hs], kslab[:, hs])
            ls_pos = jnp.minimum(z, 0.0) - jnp.log2(1.0 + jnp.exp2(-jnp.abs(z)))
            ls_neg = ls_pos - z
            if diagonal:
                ls_neg = jnp.where(causal, ls_neg, 0.0)
            hi, lo = _split_bf16(ls_neg)
            r = _dot(hi, uo) + _dot(lo, uo)
            wgt = jnp.exp2(ls_pos + r[:, :SLAB] + jnp.concatenate([cs[h]] * (SLAB // BLK), axis=1))
            if diagonal:
                wgt = jnp.where(causal, wgt, 0.0)
            vs = vslab[:, (h // per_slab) * LANES:(h // per_slab + 1) * LANES]
            vh = jnp.where(v_lane_head == h % per_slab, vs, jnp.zeros_like(vs))
            accs[h // per_slab] = accs[h // per_slab] + _dot(wgt.astype(BF16), vh)
            cs_out.append(cs[h] + r[:, SLAB:])
        cmax = functools.reduce(jnp.maximum, [jnp.max(c) for c in cs_out])
        return tuple(cs_out), tuple(accs), cmax

    zeros = jnp.zeros((SLAB, BLK), F32)
    cs, accs, cmax = fold(qi, (zeros,) * n_heads, (jnp.zeros((SLAB, LANES), F32),) * (n_heads // per_slab), True)

    def body(carry):
        j, cs, accs, _ = carry
        cs, accs, cmax = fold(j, cs, accs, False)
        return j - 1, cs, accs, cmax

    def cond(carry):
        j, _, _, cmax = carry
        return jnp.logical_and(j >= 0, cmax > SB_UNDERFLOW)

    _, _, accs, _ = lax.while_loop(cond, body, (qi - 1, cs, accs, cmax))
    o_ref[...] = jnp.concatenate(accs, axis=1).astype(o_ref.dtype)


def _sb_attention(q, k, v, q0):
    B, Lq, W = q.shape
    Lk = k.shape[1]
    uo = jnp.concatenate([jnp.tril(jnp.ones((SLAB, SLAB), F32), -1), jnp.ones((SLAB, BLK), F32)], axis=1).astype(BF16)
    wb = SB_HEADS_PER_STEP * HEAD_DIM
    grid = (B, W // wb, Lq // SLAB)
    return pl.pallas_call(
        functools.partial(_sb_kernel, q0=q0),
        grid=grid,
        in_specs=[pl.BlockSpec((None, SLAB, wb), lambda b, h, i: (b, i, h)),
                  pl.BlockSpec((None, Lk, wb), lambda b, h, i: (b, 0, h)),
                  pl.BlockSpec((None, Lk, wb), lambda b, h, i: (b, 0, h)),
                  pl.BlockSpec(uo.shape, lambda b, h, i: (0, 0))],
        out_specs=pl.BlockSpec((None, SLAB, wb), lambda b, h, i: (b, i, h)),
        out_shape=jax.ShapeDtypeStruct((B, Lq, W), BF16),
        compiler_params=_cparams(("parallel", "parallel", "arbitrary")), name="sb_attn",
    )(q, k, v, uo)


def _dsa_kernel(q_ref, iq_ref, iwt_ref, k_ref, vt_ref, ik_ref, lt_ref, o_ref,
                keys_ref, acc_ref, *, q_pos0, nslab_total, chunk_off, n_valid, n_sel, n_heads,
                kv_heads):
    tq = q_ref.shape[0]
    q_start = q_pos0 + pl.program_id(1) * tq
    nslab = jnp.minimum((q_start + tq + CHUNK - 1 + SLAB - 1) // SLAB, nslab_total)
    group = n_heads // kv_heads
    idx_w_scale = IDX_HEADS ** -0.5

    qpos = q_start + lax.broadcasted_iota(jnp.int32, (1, tq), 1)
    chunk_end = chunk_off + CHUNK * (jnp.right_shift(qpos - chunk_off, CHUNK.bit_length() - 1) + 1)
    key_limit = jnp.minimum(jnp.where(qpos < chunk_off, chunk_off, chunk_end), n_valid)
    slab_row = lax.broadcasted_iota(jnp.int32, (SLAB, tq), 0)

    iwt = iwt_ref[...]
    by_head = lambda x, heads: jnp.concatenate([x[:, h * HEAD_DIM:(h + 1) * HEAD_DIM] for h in heads], axis=0)
    iq_rows = by_head(iq_ref[...], range(IDX_HEADS))

    def score_slab(j, _):
        start = pl.multiple_of(j * SLAB, SLAB)
        s = _dot_nt(ik_ref[pl.ds(start, SLAB), :][:, :HEAD_DIM], iq_rows)
        acc = jnp.zeros((SLAB, tq), F32)
        for h in range(IDX_HEADS):
            acc = acc + jnp.maximum(s[:, h * tq:(h + 1) * tq], 0.0) * iwt[h:h + 1, :]
        bits = lax.bitcast_convert_type(acc * idx_w_scale, jnp.int32)
        key = bits ^ ((bits >> 31) & 0x7FFFFFFF)
        key = jnp.where(slab_row < key_limit - start, key, INT_MIN)
        keys_ref[pl.ds(start, SLAB), :] = key
        return 0

    lax.fori_loop(0, nslab // 2, lambda t, _: score_slab(2 * t + 1, score_slab(2 * t, 0)), 0)
    lax.fori_loop(2 * (nslab // 2), nslab, score_slab, 0)

    def select_threshold(lanes):
        def count_ge(cand):
            def body(j, cnt):
                start = pl.multiple_of(j * SLAB, SLAB)
                hit = jnp.where(keys_ref[pl.ds(start, SLAB), lanes] >= cand, 1.0, 0.0)
                return cnt + hit[:BLK] + hit[BLK:]
            cnt = lax.fori_loop(0, nslab, body, jnp.zeros((BLK, BLK), F32))
            return jnp.sum(cnt, axis=0, keepdims=True)

        def search(it, thr):
            cand = thr + jnp.left_shift(jnp.int32(1), 31 - it)
            return jnp.where(count_ge(cand) >= n_sel, cand, thr)

        thr = lax.fori_loop(0, 32, search, jnp.full((1, BLK), INT_MIN, jnp.int32))
        return thr, n_sel - count_ge(thr + 1)

    halves = [select_threshold(slice(c * BLK, (c + 1) * BLK)) for c in range(tq // BLK)]
    thr = jnp.concatenate([t for t, _ in halves], axis=1)
    room = jnp.concatenate([r for _, r in halves], axis=1)

    acc_ref[...] = jnp.zeros(acc_ref.shape, F32)
    q = q_ref[...]
    q_rows = [by_head(q, range(kvh * group, (kvh + 1) * group)) for kvh in range(kv_heads)]
    lt = lt_ref[...]

    def attend_slab(j, carry):
        eq_seen, m_all, l_all = carry
        start = pl.multiple_of(j * SLAB, SLAB)
        kk = keys_ref[pl.ds(start, SLAB), :]
        eq = kk == thr
        prefix = _dot(lt, jnp.where(eq, 1.0, 0.0).astype(BF16)) + eq_seen
        sel = jnp.logical_or(kk > thr, jnp.logical_and(eq, prefix <= room))
        sel = jnp.logical_and(sel, kk != INT_MIN)
        bias = jnp.where(sel, 0.0, NEG_BIG)
        kj = k_ref[pl.ds(start, SLAB), :]
        vtj = vt_ref[j]
        m_out, l_out = [], []
        for kvh in range(kv_heads):
            s_all = _dot_nt(kj[:, kvh * HEAD_DIM:(kvh + 1) * HEAD_DIM], q_rows[kvh])
            ps, alphas = [], []
            for g in range(group):
                h = kvh * group + g
                s = s_all[:, g * tq:(g + 1) * tq] + bias
                m_new = jnp.maximum(m_all[h], jnp.max(s, axis=0, keepdims=True))
                p = jnp.exp2(s - m_new)
                alpha = jnp.exp2(m_all[h] - m_new)
                l_out.append(alpha * l_all[h] + jnp.sum(p, axis=0, keepdims=True))
                m_out.append(m_new)
                ps.append(p.astype(BF16))
                alphas.append(alpha)
            pv = _dot(vtj[kvh * HEAD_DIM:(kvh + 1) * HEAD_DIM, :], jnp.concatenate(ps, axis=1))
            acc_ref[kvh] = jnp.concatenate(alphas, axis=1) * acc_ref[kvh] + pv
        return prefix[SLAB - 1:SLAB, :], tuple(m_out), tuple(l_out)

    init = (jnp.zeros((1, tq), F32), (jnp.full((1, tq), NEG_BIG, F32),) * n_heads,
            (jnp.zeros((1, tq), F32),) * n_heads)
    carry = lax.fori_loop(0, nslab // 2, lambda t, c: attend_slab(2 * t + 1, attend_slab(2 * t, c)), init)
    _, _, l_all = lax.fori_loop(2 * (nslab // 2), nslab, attend_slab, carry)

    outs = []
    for kvh in range(kv_heads):
        acc = acc_ref[kvh]
        for g in range(group):
            outs.append(acc[:, g * tq:(g + 1) * tq] / l_all[kvh * group + g])
    o_ref[...] = jnp.concatenate(outs, axis=0).T.astype(o_ref.dtype)


def _dsa_attention(q, iq, iwt, k, vt, ik, *, tq, n_q, q_pos0, chunk_off, n_valid, n_sel, kv_heads):
    B, _, W = q.shape
    Lq = n_q * tq
    Lk = k.shape[1]
    n_heads = W // HEAD_DIM
    lt = jnp.tril(jnp.ones((SLAB, SLAB), F32)).astype(BF16)
    kern = functools.partial(_dsa_kernel, q_pos0=q_pos0, nslab_total=Lk // SLAB, chunk_off=chunk_off, n_valid=n_valid,
                             n_sel=n_sel, n_heads=n_heads, kv_heads=kv_heads)
    return pl.pallas_call(
        kern,
        grid=(B, n_q),
        in_specs=[pl.BlockSpec((None, tq, W), lambda b, i: (b, i, 0)),
                  pl.BlockSpec((None, tq, iq.shape[2]), lambda b, i: (b, i, 0)),
                  pl.BlockSpec((None, IDX_HEADS, tq), lambda b, i: (b, 0, i)),
                  pl.BlockSpec((None, Lk, k.shape[2]), lambda b, i: (b, 0, 0)),
                  pl.BlockSpec((None,) + vt.shape[1:], lambda b, i: (b, 0, 0, 0)),
                  pl.BlockSpec((None, Lk, ik.shape[2]), lambda b, i: (b, 0, 0)),
                  pl.BlockSpec(lt.shape, lambda b, i: (0, 0))],
        out_specs=pl.BlockSpec((None, tq, W), lambda b, i: (b, i, 0)),
        out_shape=jax.ShapeDtypeStruct((B, Lq, W), BF16),
        scratch_shapes=[pltpu.VMEM((Lk, tq), jnp.int32),
                        pltpu.VMEM((kv_heads, HEAD_DIM, (n_heads // kv_heads) * tq), F32)],
        compiler_params=_cparams(("parallel", "arbitrary")), name="dsa_attn",
    )(q, iq, iwt, k, vt, ik, lt)


def _merge_kernel(x_ref, osb_ref, odsa_ref, gsb_ref, gdsa_ref, wsb_ref, wdsa_ref, wout_ref, g2_ref,
                  wr_hi_ref, wr_lo_ref, h_ref, hn_ref, comb_ref, *, n_groups, per_group):
    merged = gsb_ref[0] * _dot(osb_ref[0], wsb_ref[...]) + gdsa_ref[0] * _dot(odsa_ref[0], wdsa_ref[...])
    h = x_ref[...] + _dot(merged.astype(BF16), wout_ref[...])
    h_ref[...] = h
    ms = jnp.mean(h * h, axis=-1, keepdims=True)
    hn = h * lax.rsqrt(ms + EPS) * g2_ref[...]
    hn_ref[...] = hn.astype(BF16)

    hi, lo = _split_bf16(hn)
    logits = _dot(hi, wr_hi_ref[...]) + _dot(lo, wr_hi_ref[...]) + _dot(hi, wr_lo_ref[...])
    lane = lax.broadcasted_iota(jnp.int32, logits.shape, 1)
    big = jnp.int32(LANES)
    neg_inf = -jnp.inf

    def first_argmax(vals):
        top = jnp.max(vals, axis=-1, keepdims=True)
        idx = jnp.min(jnp.where(vals == top, lane, big), axis=-1, keepdims=True)
        return top, idx

    is_group = lane < n_groups
    gl = jnp.where(is_group, logits, neg_inf)
    ge = jnp.exp(gl - jnp.max(gl, axis=-1, keepdims=True))
    probs = jnp.where(is_group, ge / jnp.sum(ge, axis=-1, keepdims=True), neg_inf)
    gp, gi = first_argmax(probs)

    expert = lane - n_groups
    in_group = jnp.logical_and(expert >= gi * per_group, expert < (gi + 1) * per_group)
    vals = jnp.where(in_group, logits, neg_inf)
    ev0, i0 = first_argmax(vals)
    ev1, i1 = first_argmax(jnp.where(lane == i0, neg_inf, vals))
    e1 = jnp.exp(ev1 - ev0)
    w0 = gp / (1.0 + e1)
    w1 = gp * e1 / (1.0 + e1)
    comb_ref[...] = jnp.where(lane == i0, w0, jnp.where(lane == i1, w1, 0.0))


def _merge(x, osb, odsa, gsb, gdsa, w, row0):
    B, R, D = x.shape
    tm = next(t for t in (512, 256, 128, 64, 32, 16) if R % t == 0)
    nt = R // tm
    shifted = lambda a: pl.BlockSpec((pl.Element(1), pl.Element(tm), pl.Element(a.shape[2])),
                                     lambda b, i: (b, pl.multiple_of(row0 + i * tm, ROW_ALIGN), 0))
    full = lambda a: pl.BlockSpec(a.shape, lambda b, i: (0,) * a.ndim)
    out = lambda width: pl.BlockSpec((tm, width), lambda b, i: (b * nt + i, 0))
    weights = [w["wsb"], w["wdsa"], w["wout"], w["g2"], w["wr_hi"], w["wr_lo"]]
    in_specs = ([pl.BlockSpec((None, tm, D), lambda b, i: (b, i, 0))] + [shifted(a) for a in (osb, odsa, gsb, gdsa)]
                + [full(a) for a in weights])
    return pl.pallas_call(
        functools.partial(_merge_kernel, n_groups=w["n_groups"], per_group=w["per_group"]),
        grid=(B, nt), in_specs=in_specs,
        out_specs=[out(D), out(D), out(LANES)],
        out_shape=[jax.ShapeDtypeStruct((B * R, D), F32), jax.ShapeDtypeStruct((B * R, D), BF16),
                   jax.ShapeDtypeStruct((B * R, LANES), F32)],
        compiler_params=_cparams(("parallel", "parallel")), name="merge_route",
    )(x, osb, odsa, gsb, gdsa, *weights)


def _moe_kernel(h_ref, hn_ref, comb_ref, wg_ref, wu_ref, wd_ref, y_ref, hh_ref, *, n_groups):
    step = pl.program_id(1)
    n_exp, _, fd = wg_ref.shape

    @pl.when(step == 0)
    def _():
        y_ref[...] = h_ref[...]

    hn = hn_ref[...]
    comb = comb_ref[...]
    lane = lax.broadcasted_iota(jnp.int32, comb.shape, 1)
    for e in range(n_exp):
        ce = jnp.sum(jnp.where(lane == n_groups + step * n_exp + e, comb, 0.0), axis=-1, keepdims=True)
        a = _dot(hn, wg_ref[e])
        b = _dot(hn, wu_ref[e])
        hh_ref[:, e * fd:(e + 1) * fd] = (a * jax.nn.sigmoid(a) * b * ce).astype(BF16)
    y_ref[...] += _dot(hh_ref[...], wd_ref[...])


def _moe(h, hn, comb, w):
    N, D = h.shape
    E, _, Fd = w["wgate"].shape
    tm = next(t for t in (1024, 512, 256, 128, 64, 32, 16) if N % t == 0)
    ne = MOE_EXPERTS_PER_STEP
    assert E % ne == 0
    return pl.pallas_call(
        functools.partial(_moe_kernel, n_groups=w["n_groups"]),
        grid=(N // tm, E // ne),
        in_specs=[pl.BlockSpec((tm, D), lambda i, e: (i, 0)),
                  pl.BlockSpec((tm, D), lambda i, e: (i, 0)),
                  pl.BlockSpec((tm, LANES), lambda i, e: (i, 0)),
                  pl.BlockSpec((ne, D, Fd), lambda i, e: (e, 0, 0)),
                  pl.BlockSpec((ne, D, Fd), lambda i, e: (e, 0, 0)),
                  pl.BlockSpec((ne * Fd, D), lambda i, e: (e, 0))],
        out_specs=pl.BlockSpec((tm, D), lambda i, e: (i, 0)),
        out_shape=jax.ShapeDtypeStruct((N, D), F32),
        scratch_shapes=[pltpu.VMEM((tm, ne * Fd), BF16)],
        compiler_params=_cparams(("parallel", "arbitrary")), name="moe_experts",
    )(h, hn, comb, w["wgate"], w["wup"], w["wdown"].reshape(E * Fd, D))


def _prep_weights(norm1_g, w_in, qg, kg, w_sbb, w_dsab, w_out, norm2_g, w_group, w_router, w_gate, w_up, w_down,
                  dims):
    sb_w, dsa_w, kv_w, idx_w = dims
    D = w_in.shape[0]
    o_dv = 3 * sb_w + dsa_w + kv_w
    o_ik = o_dv + kv_w + idx_w
    o_iw = o_ik + HEAD_DIM
    o_g = o_iw + IDX_HEADS
    w_ik = w_in[:, o_ik:o_iw]
    n_groups = w_group.shape[1]
    n_experts = w_router.shape[1]
    assert n_groups + n_experts <= LANES
    wr = jnp.concatenate([w_group, w_router, jnp.zeros((D, LANES - n_groups - n_experts), F32)], axis=1)
    wr_hi, wr_lo = _split_bf16(wr)
    tile2 = lambda g: jnp.tile(g.astype(F32), LANES // HEAD_DIM)[None, :]
    head_id = jnp.arange(LANES) // HEAD_DIM
    return dict(
        g1=norm1_g.astype(F32)[None, :],
        wa=jnp.concatenate([w_in[:, :o_iw], w_ik], axis=1).astype(BF16),
        wg=w_in[:, o_g:].astype(BF16),
        wiwt=w_in[:, o_iw:o_g].T.astype(BF16),
        wdvt=w_in[:, o_dv:o_dv + kv_w].T.astype(BF16),
        qg=tile2(qg), kg=tile2(kg),
        gmat=((head_id[:, None] == head_id[None, :]).astype(F32) / HEAD_DIM).astype(BF16),
        wsb=w_sbb.astype(BF16), wdsa=w_dsab.astype(BF16), wout=w_out.astype(BF16),
        g2=norm2_g.astype(F32)[None, :], wr_hi=wr_hi, wr_lo=wr_lo,
        wgate=w_gate.astype(BF16), wup=w_up.astype(BF16), wdown=w_down.astype(BF16),
        n_groups=n_groups, per_group=n_experts // n_groups)


def _tail(x, osb, odsa, gsb, gdsa, w, row0):
    h, hn, comb = _merge(x, osb, odsa, gsb, gdsa, w, row0)
    return _moe(h, hn, comb, w).reshape(x.shape)


def kernel(x_prompt, x_sample, cache_sb_k, cache_sb_v, cache_dsa_k, cache_dsa_v, cache_idx_k, meta_tokens, norm1_g, w_in, dsa_q_norm_g, dsa_k_norm_g, w_sb_branch, w_dsa_branch, w_out, norm2_g, w_group, w_router, w_gate, w_up, w_down):
    B, S, D = x_prompt.shape
    Bs, T, _ = x_sample.shape
    depth, _, past, sb_heads, hd = cache_sb_k.shape
    kv_heads = cache_dsa_k.shape[3]
    n_meta = meta_tokens.shape[0]
    assert depth == 1 and hd == HEAD_DIM and cache_idx_k.shape[-1] == HEAD_DIM
    assert past % SLAB == 0 and T <= BLK and S % CHUNK == 0
    sb_w = sb_heads * HEAD_DIM
    dsa_w = w_dsa_branch.shape[1]
    kv_w = kv_heads * HEAD_DIM
    idx_w = IDX_HEADS * HEAD_DIM
    assert kv_w == LANES and sb_w % LANES == 0 and dsa_w % LANES == 0
    dims = (sb_w, dsa_w, kv_w, idx_w)
    w = _prep_weights(norm1_g[0], w_in[0], dsa_q_norm_g[0], dsa_k_norm_g[0], w_sb_branch[0], w_dsa_branch[0],
                      w_out[0], norm2_g[0], w_group[0], w_router[0], w_gate[0], w_up[0], w_down[0], dims)

    Lr = n_meta + S
    Lp = -(-Lr // SLAB) * SLAB
    meta = jnp.broadcast_to(meta_tokens[None].astype(x_prompt.dtype), (B, n_meta, D))
    xp = jnp.concatenate([meta, x_prompt, jnp.zeros((B, Lp - Lr, D), x_prompt.dtype)], axis=1)
    pr = _project(xp, jnp.arange(Lp, dtype=jnp.int32), w, dims, Lr)
    o_sb = _sb_attention(pr["sqb"], pr["skb"], pr["svb"], 0)
    o_dsa = _dsa_attention(pr["dqb"], pr["iqb"], pr["iwt"], pr["dkb"], pr["dvt"], pr["ikb"], tq=SLAB, n_q=Lp // SLAB,
                           q_pos0=0, chunk_off=n_meta, n_valid=Lr, n_sel=min(TOPK_MAX, S // 4), kv_heads=kv_heads)
    y_prompt = _tail(x_prompt, o_sb, o_dsa, pr["gsb"], pr["gdsa"], w, n_meta)

    xs = jnp.concatenate([x_sample, jnp.zeros((Bs, SLAB - T, D), x_sample.dtype)], axis=1)
    sr = _project(xs, past + jnp.arange(SLAB, dtype=jnp.int32), w, dims, T)
    cat = lambda c, new: jnp.concatenate([c.astype(BF16), new], axis=1)
    k_sb = cat(cache_sb_k[0].reshape(Bs, past, sb_w), sr["skb"])
    v_sb = cat(cache_sb_v[0].reshape(Bs, past, sb_w), sr["svb"])
    k_ds = cat(cache_dsa_k[0].reshape(Bs, past, kv_w), sr["dkb"])
    k_ix = cat(jnp.tile(cache_idx_k[0], (1, 1, LANES // HEAD_DIM)), sr["ikb"])
    vt_c = cache_dsa_v[0].reshape(Bs, past // SLAB, SLAB, kv_w).transpose(0, 1, 3, 2).astype(BF16)
    vt_ds = jnp.concatenate([vt_c, sr["dvt"]], axis=1)
    qb = past // SLAB
    o_sb_s = _sb_attention(sr["sqb"], k_sb, v_sb, qb)
    o_dsa_s = _dsa_attention(sr["dqb"], sr["iqb"], sr["iwt"], k_ds, vt_ds, k_ix, tq=BLK, n_q=1, q_pos0=past,
                             chunk_off=0, n_valid=past + T, n_sel=min(TOPK_MAX, (past + T) // 4), kv_heads=kv_heads)
    y_sample = _tail(x_sample, o_sb_s, o_dsa_s, sr["gsb"], sr["gdsa"], w, 0)

    heads = lambda a, n: a.reshape(1, a.shape[0], a.shape[1], n, HEAD_DIM)
    return (y_prompt, y_sample,
            heads(pr["skf"], sb_heads), heads(pr["svf"], sb_heads),
            heads(pr["dkf"], kv_heads), heads(pr["dvf"], kv_heads), pr["ikf"][None],
            heads(sr["skf"], sb_heads), heads(sr["svf"], sb_heads),
            heads(sr["dkf"], kv_heads), heads(sr["dvf"], kv_heads), sr["ikf"][None])
```

```python
import functools

import jax
import jax.numpy as jnp
from jax import lax
from jax.experimental import pallas as pl
from jax.experimental.pallas import tpu as pltpu

F32 = jnp.float32
BF16 = jnp.bfloat16

CHUNK = 64
TOPK_MAX = 256
ROPE_THETA = 10000.0
EPS = 1e-6
HEAD_DIM = 64
IDX_HEADS = 8
EXPERT_TOPK = 2

LANES = 128
BLK = LANES
SLAB = 2 * BLK
SB_HEADS_PER_STEP = 4
MOE_EXPERTS_PER_STEP = 4
ROW_ALIGN = 16
VMEM_LIMIT = 56 * 1024 * 1024
INT_MIN = -(2 ** 31)
NEG_BIG = -1e30
LOG2E = 1.4426950408889634
SB_UNDERFLOW = -151.0


def _cparams(sem):
    return pltpu.CompilerParams(dimension_semantics=sem, vmem_limit_bytes=VMEM_LIMIT)


def _dot(a, b):
    return jnp.dot(a, b, preferred_element_type=F32)


def _dot_nt(a, b):
    return lax.dot_general(a, b, (((1,), (1,)), ((), ())), preferred_element_type=F32)


def _split_bf16(x):
    hi = x.astype(BF16)
    lo = (x - hi.astype(F32)).astype(BF16)
    return hi, lo


def _proj_kernel(x_ref, g1_ref, wa_ref, wg_ref, wiw_ref, wdvt_ref, cos_ref, sin_ref,
                 qg_ref, kg_ref, gmat_ref,
                 sqb_ref, skf_ref, svf_ref, skb_ref, svb_ref, dqb_ref, dkf_ref, dvf_ref,
                 dkb_ref, dvt_ref, iqb_ref, ikf_ref, ikb_ref, iw_ref, gsb_ref, gdsa_ref,
                 *, sb_w, dsa_w, kv_w, idx_w):
    x = x_ref[...]
    ms = jnp.mean(x * x, axis=-1, keepdims=True)
    xn = x * lax.rsqrt(ms + EPS) * g1_ref[...]
    xb = xn.astype(BF16)
    cos = cos_ref[...]
    sin = sin_ref[...]
    gmat = gmat_ref[...]
    lane = lax.broadcasted_iota(jnp.int32, cos.shape, 1)
    first_half = (lane % HEAD_DIM) < (HEAD_DIM // 2)

    def mm(lo, width):
        return _dot(xb, wa_ref[:, lo:lo + width])

    def head_norm(y, gain):
        hi, lo = _split_bf16(y * y)
        m = _dot(hi, gmat) + _dot(lo, gmat)
        return y * lax.rsqrt(m + EPS) * gain

    def rope(y):
        swapped = jnp.where(first_half, pltpu.roll(y, LANES - HEAD_DIM // 2, 1),
                            pltpu.roll(y, HEAD_DIM // 2, 1))
        return y * cos + swapped * sin

    scale = HEAD_DIM ** -0.5
    scale2 = scale * LOG2E
    off = 0
    sqb_ref[...] = (mm(off, sb_w) * scale2).astype(BF16)
    off += sb_w
    n_out = skf_ref.shape[0]
    sk = mm(off, sb_w)
    skf_ref[...] = sk[:n_out]
    skb_ref[...] = sk.astype(BF16)
    off += sb_w
    sv = mm(off, sb_w)
    svf_ref[...] = sv[:n_out]
    svb_ref[...] = sv.astype(BF16)
    off += sb_w
    for c in range(dsa_w // LANES):
        y = rope(head_norm(mm(off + c * LANES, LANES), qg_ref[...]))
        dqb_ref[:, c * LANES:(c + 1) * LANES] = (y * scale2).astype(BF16)
    off += dsa_w
    for c in range(kv_w // LANES):
        y = rope(head_norm(mm(off + c * LANES, LANES), kg_ref[...]))
        dkf_ref[:, c * LANES:(c + 1) * LANES] = y[:n_out]
        dkb_ref[:, c * LANES:(c + 1) * LANES] = y.astype(BF16)
    off += kv_w
    dvf_ref[...] = mm(off, kv_w)[:n_out]
    off += kv_w
    for c in range(idx_w // LANES):
        y = rope(mm(off + c * LANES, LANES))
        iqb_ref[:, c * LANES:(c + 1) * LANES] = (y * scale).astype(BF16)
    off += idx_w
    y = rope(mm(off, LANES))
    ikf_ref[...] = y[:n_out, :HEAD_DIM]
    ikb_ref[...] = y.astype(BF16)
    iw_ref[...] = _dot(xb, wiw_ref[...])
    dvt_ref[...] = _dot_nt(wdvt_ref[...], xb).astype(BF16)
    d = gsb_ref.shape[-1]
    gsb_ref[...] = jax.nn.sigmoid(_dot(xb, wg_ref[:, :d]))
    gdsa_ref[...] = jax.nn.sigmoid(_dot(xb, wg_ref[:, d:]))


def _rope_tables(pos):
    half = HEAD_DIM // 2
    freqs = ROPE_THETA ** (-jnp.arange(half, dtype=F32) / half)
    ang = pos.astype(F32)[:, None] * freqs[None, :]
    cos, sin = jnp.cos(ang), jnp.sin(ang)
    cos_t = jnp.tile(cos, (1, LANES // half))
    sin_t = jnp.tile(jnp.concatenate([-sin, sin], axis=1), (1, LANES // HEAD_DIM))
    return cos_t, sin_t


def _project(xpad, pos, w, dims, n_real):
    B, L, D = xpad.shape
    sb_w, dsa_w, kv_w, idx_w = dims
    tm = SLAB
    cos_t, sin_t = _rope_tables(pos)
    grid = (B, L // tm)
    row = lambda width: pl.BlockSpec((None, tm, width), lambda b, i: (b, i, 0))
    full = lambda a: pl.BlockSpec(a.shape, lambda b, i: (0,) * a.ndim)
    tab = pl.BlockSpec((tm, LANES), lambda b, i: (i, 0))
    out_shapes = dict(
        sqb=(BF16, sb_w), skf=(F32, sb_w), svf=(F32, sb_w), skb=(BF16, sb_w), svb=(BF16, sb_w),
        dqb=(BF16, dsa_w), dkf=(F32, kv_w), dvf=(F32, kv_w), dkb=(BF16, kv_w))
    names = ["sqb", "skf", "svf", "skb", "svb", "dqb", "dkf", "dvf", "dkb", "dvt", "iqb", "ikf",
             "ikb", "iw", "gsb", "gdsa"]
    shapes, specs = [], []
    f32_rows = min(tm, n_real)
    for n in names:
        if n in out_shapes:
            dt, width = out_shapes[n]
        elif n == "dvt":
            shapes.append(jax.ShapeDtypeStruct((B, L // SLAB, kv_w, SLAB), BF16))
            specs.append(pl.BlockSpec((None, None, kv_w, SLAB), lambda b, i: (b, i, 0, 0)))
            continue
        elif n == "iqb":
            dt, width = BF16, idx_w
        elif n == "ikf":
            dt, width = F32, HEAD_DIM
        elif n == "ikb":
            dt, width = BF16, LANES
        elif n == "iw":
            dt, width = F32, LANES
        else:
            dt, width = F32, D
        if n in ("skf", "svf", "dkf", "dvf", "ikf"):
            shapes.append(jax.ShapeDtypeStruct((B, n_real, width), dt))
            specs.append(pl.BlockSpec((None, f32_rows, width), lambda b, i: (b, i, 0)))
            continue
        shapes.append(jax.ShapeDtypeStruct((B, L, width), dt))
        specs.append(row(width))
    ins = [xpad, w["g1"], w["wa"], w["wg"], w["wiw"], w["wdvt"], cos_t, sin_t, w["qg"], w["kg"], w["gmat"]]
    in_specs = [row(D), full(w["g1"]), full(w["wa"]), full(w["wg"]), full(w["wiw"]), full(w["wdvt"]),
                tab, tab, full(w["qg"]), full(w["kg"]), full(w["gmat"])]
    outs = pl.pallas_call(
        functools.partial(_proj_kernel, sb_w=sb_w, dsa_w=dsa_w, kv_w=kv_w, idx_w=idx_w),
        grid=grid, in_specs=in_specs, out_specs=specs, out_shape=shapes,
        compiler_params=_cparams(("parallel", "parallel")), name="proj",
    )(*ins)
    return dict(zip(names, outs))


def _sb_kernel(q_ref, k_ref, v_ref, uo_ref, o_ref, *, q0):
    qi = q0 + pl.program_id(2)
    q = q_ref[...]
    uo = uo_ref[...]
    n_heads = q.shape[1] // HEAD_DIM
    per_slab = LANES // HEAD_DIM
    causal = (lax.broadcasted_iota(jnp.int32, (SLAB, SLAB), 1) < lax.broadcasted_iota(jnp.int32, (SLAB, SLAB), 0))
    v_lane_head = lax.broadcasted_iota(jnp.int32, (SLAB, LANES), 1) // HEAD_DIM

    def fold(j, cs, accs, diagonal):
        start = pl.multiple_of(j * SLAB, SLAB)
        kslab = k_ref[pl.ds(start, SLAB), :]
        vslab = v_ref[pl.ds(start, SLAB), :]
        cs_out, accs = [], list(accs)
        for h in range(n_heads):
            hs = slice(h * HEAD_DIM, (h + 1) * HEAD_DIM)
            z = _dot_nt(q[:, hs], kslab[:, hs])
            ls_pos = jnp.minimum(z, 0.0) - jnp.log2(1.0 + jnp.exp2(-jnp.abs(z)))
            ls_neg = ls_pos - z
            if diagonal:
                ls_neg = jnp.where(causal, ls_neg, 0.0)
            hi, lo = _split_bf16(ls_neg)
            r = _dot(hi, uo) + _dot(lo, uo)
            wgt = jnp.exp2(ls_pos + r[:, :SLAB] + jnp.concatenate([cs[h]] * (SLAB // BLK), axis=1))
            if diagonal:
                wgt = jnp.where(causal, wgt, 0.0)
            vs = vslab[:, (h // per_slab) * LANES:(h // per_slab + 1) * LANES]
            vh = jnp.where(v_lane_head == h % per_slab, vs, jnp.zeros_like(vs))
            accs[h // per_slab] = accs[h // per_slab] + _dot(wgt.astype(BF16), vh)
            cs_out.append(cs[h] + r[:, SLAB:])
        cmax = functools.reduce(jnp.maximum, [jnp.max(c) for c in cs_out])
        return tuple(cs_out), tuple(accs), cmax

    zeros = jnp.zeros((SLAB, BLK), F32)
    cs, accs, cmax = fold(qi, (zeros,) * n_heads, (jnp.zeros((SLAB, LANES), F32),) * (n_heads // per_slab), True)

    def body(carry):
        j, cs, accs, _ = carry
        cs, accs, cmax = fold(j, cs, accs, False)
        return j - 1, cs, accs, cmax

    def cond(carry):
        j, _, _, cmax = carry
        return jnp.logical_and(j >= 0, cmax > SB_UNDERFLOW)

    _, _, accs, _ = lax.while_loop(cond, body, (qi - 1, cs, accs, cmax))
    o_ref[...] = jnp.concatenate(accs, axis=1).astype(o_ref.dtype)


def _sb_attention(q, k, v, q0):
    B, Lq, W = q.shape
    Lk = k.shape[1]
    uo = jnp.concatenate([jnp.tril(jnp.ones((SLAB, SLAB), F32), -1), jnp.ones((SLAB, BLK), F32)], axis=1).astype(BF16)
    wb = SB_HEADS_PER_STEP * HEAD_DIM
    grid = (B, W // wb, Lq // SLAB)
    return pl.pallas_call(
        functools.partial(_sb_kernel, q0=q0),
        grid=grid,
        in_specs=[pl.BlockSpec((None, SLAB, wb), lambda b, h, i: (b, i, h)),
                  pl.BlockSpec((None, Lk, wb), lambda b, h, i: (b, 0, h)),
                  pl.BlockSpec((None, Lk, wb), lambda b, h, i: (b, 0, h)),
                  pl.BlockSpec(uo.shape, lambda b, h, i: (0, 0))],
        out_specs=pl.BlockSpec((None, SLAB, wb), lambda b, h, i: (b, i, h)),
        out_shape=jax.ShapeDtypeStruct((B, Lq, W), BF16),
        compiler_params=_cparams(("parallel", "parallel", "arbitrary")), name="sb_attn",
    )(q, k, v, uo)


def _dsa_kernel(q_ref, iq_ref, iw_ref, k_ref, vt_ref, ik_ref, lt_ref, o_ref,
                keys_ref, acc_ref, *, q_pos0, nslab_total, chunk_off, n_valid, n_sel, n_heads, kv_heads):
    tq = q_ref.shape[1]
    q_start = q_pos0 + pl.program_id(1) * tq
    nslab = jnp.minimum((q_start + tq) // SLAB + 1, nslab_total)
    group = n_heads // kv_heads
    idx_w_scale = IDX_HEADS ** -0.5

    qpos = q_start + lax.broadcasted_iota(jnp.int32, (1, tq), 1)
    chunk_end = chunk_off + CHUNK * (jnp.right_shift(qpos - chunk_off, CHUNK.bit_length() - 1) + 1)
    key_limit = jnp.minimum(jnp.where(qpos < chunk_off, chunk_off, chunk_end), n_valid)
    slab_row = lax.broadcasted_iota(jnp.int32, (SLAB, tq), 0)

    iwt = iw_ref[0].T
    by_head = lambda x, heads: jnp.concatenate([x[:, h * HEAD_DIM:(h + 1) * HEAD_DIM] for h in heads], axis=0)
    iq_rows = by_head(iq_ref[0], range(IDX_HEADS))

    def score_keys(ik_rows, pos0):
        rows = ik_rows.shape[0]
        s = _dot_nt(ik_rows[:, :HEAD_DIM], iq_rows)
        acc = jnp.zeros((rows, tq), F32)
        for h in range(IDX_HEADS):
            acc = acc + jnp.maximum(s[:, h * tq:(h + 1) * tq], 0.0) * iwt[h:h + 1, :]
        bits = lax.bitcast_convert_type(acc * idx_w_scale, jnp.int32)
        key = bits ^ ((bits >> 31) & 0x7FFFFFFF)
        return jnp.where(slab_row[:rows] < key_limit - pos0, key, INT_MIN)

    def score_slab(j, _):
        start = pl.multiple_of(j * SLAB, SLAB)
        keys_ref[pl.ds(start, SLAB), :] = score_keys(ik_ref[pl.ds(start, SLAB), :], start)
        return 0

    lax.fori_loop(0, nslab // 2, lambda t, _: score_slab(2 * t + 1, score_slab(2 * t, 0)), 0)
    lax.fori_loop(2 * (nslab // 2), nslab, score_slab, 0)

    def select_threshold(lanes):
        def count_ge(cand):
            def body(j, cnt):
                start = pl.multiple_of(j * SLAB, SLAB)
                hit = jnp.where(keys_ref[pl.ds(start, SLAB), lanes] >= cand, 1.0, 0.0)
                return cnt + hit[:BLK] + hit[BLK:]
            cnt = lax.fori_loop(0, nslab, body, jnp.zeros((BLK, BLK), F32))
            return jnp.sum(cnt, axis=0, keepdims=True)

        def search(it, thr):
            cand = thr + jnp.left_shift(jnp.int32(1), 31 - it)
            return jnp.where(count_ge(cand) >= n_sel, cand, thr)

        thr = lax.fori_loop(0, 32, search, jnp.full((1, BLK), INT_MIN, jnp.int32))
        return thr, n_sel - count_ge(thr + 1)

    halves = [select_threshold(slice(c * BLK, (c + 1) * BLK)) for c in range(tq // BLK)]
    thr = jnp.concatenate([t for t, _ in halves], axis=1)
    room = jnp.concatenate([r for _, r in halves], axis=1)

    acc_ref[...] = jnp.zeros(acc_ref.shape, F32)
    q = q_ref[0]
    q_rows = [by_head(q, range(kvh * group, (kvh + 1) * group)) for kvh in range(kv_heads)]
    lt = lt_ref[...]

    def attend(kk, kj, vtj, carry):
        eq_seen, m_all, l_all = carry
        rows = kk.shape[0]
        eq = kk == thr
        prefix = _dot(lt[:rows, :rows], jnp.where(eq, 1.0, 0.0).astype(BF16)) + eq_seen
        sel = jnp.logical_or(kk > thr, jnp.logical_and(eq, prefix <= room))
        sel = jnp.logical_and(sel, kk != INT_MIN)
        bias = jnp.where(sel, 0.0, NEG_BIG)
        m_out, l_out = [], []
        for kvh in range(kv_heads):
            s_all = _dot_nt(kj[:, kvh * HEAD_DIM:(kvh + 1) * HEAD_DIM], q_rows[kvh])
            ps, alphas = [], []
            for g in range(group):
                h = kvh * group + g
                s = s_all[:, g * tq:(g + 1) * tq] + bias
                m_new = jnp.maximum(m_all[h], jnp.max(s, axis=0, keepdims=True))
                p = jnp.exp2(s - m_new)
                alpha = jnp.exp2(m_all[h] - m_new)
                l_out.append(alpha * l_all[h] + jnp.sum(p, axis=0, keepdims=True))
                m_out.append(m_new)
                ps.append(p.astype(BF16))
                alphas.append(alpha)
            pv = _dot(vtj[kvh * HEAD_DIM:(kvh + 1) * HEAD_DIM, :], jnp.concatenate(ps, axis=1))
            acc_ref[kvh] = jnp.concatenate(alphas, axis=1) * acc_ref[kvh] + pv
        return prefix[rows - 1:rows, :], tuple(m_out), tuple(l_out)

    def attend_slab(j, carry):
        start = pl.multiple_of(j * SLAB, SLAB)
        return attend(keys_ref[pl.ds(start, SLAB), :], k_ref[pl.ds(start, SLAB), :], vt_ref[j], carry)

    init = (jnp.zeros((1, tq), F32), (jnp.full((1, tq), NEG_BIG, F32),) * n_heads,
            (jnp.zeros((1, tq), F32),) * n_heads)
    carry = lax.fori_loop(0, nslab // 2, lambda t, c: attend_slab(2 * t + 1, attend_slab(2 * t, c)), init)
    _, _, l_all = lax.fori_loop(2 * (nslab // 2), nslab, attend_slab, carry)

    outs = []
    for kvh in range(kv_heads):
        acc = acc_ref[kvh]
        for g in range(group):
            outs.append(acc[:, g * tq:(g + 1) * tq] / l_all[kvh * group + g])
    o_ref[...] = jnp.concatenate(outs, axis=0).T.astype(o_ref.dtype)


def _dsa_attention(q, iq, iw, k, vt, ik, *, tq, n_q, q_row0, q_pos0, chunk_off, n_valid, n_sel, kv_heads):
    B, _, W = q.shape
    Lk = k.shape[1]
    n_heads = W // HEAD_DIM
    lt = jnp.tril(jnp.ones((SLAB, SLAB), F32)).astype(BF16)
    kern = functools.partial(_dsa_kernel, q_pos0=q_pos0, nslab_total=Lk // SLAB,
                             chunk_off=chunk_off, n_valid=n_valid, n_sel=n_sel, n_heads=n_heads, kv_heads=kv_heads)
    rows = lambda a: pl.BlockSpec((pl.Element(1), pl.Element(tq), pl.Element(a.shape[2])),
                                  lambda b, i: (b, pl.multiple_of(q_row0 + i * tq, ROW_ALIGN), 0))
    return pl.pallas_call(
        kern,
        grid=(B, n_q),
        in_specs=[rows(q), rows(iq), rows(iw),
                  pl.BlockSpec((None, Lk, k.shape[2]), lambda b, i: (b, 0, 0)),
                  pl.BlockSpec((None,) + vt.shape[1:], lambda b, i: (b, 0, 0, 0)),
                  pl.BlockSpec((None, Lk, ik.shape[2]), lambda b, i: (b, 0, 0)),
                  pl.BlockSpec(lt.shape, lambda b, i: (0, 0))],
        out_specs=pl.BlockSpec((None, tq, W), lambda b, i: (b, i, 0)),
        out_shape=jax.ShapeDtypeStruct((B, n_q * tq, W), BF16),
        scratch_shapes=[pltpu.VMEM((Lk, tq), jnp.int32),
                        pltpu.VMEM((kv_heads, HEAD_DIM, (n_heads // kv_heads) * tq), F32)],
        compiler_params=_cparams(("parallel", "arbitrary")), name="dsa_attn",
    )(q, iq, iw, k, vt, ik, lt)


def _merge_kernel(x_ref, osb_ref, odsa_ref, gsb_ref, gdsa_ref, wsb_ref, wdsa_ref, wout_ref, g2_ref,
                  wr_hi_ref, wr_lo_ref, h_ref, hn_ref, comb_ref, *, n_groups, per_group):
    merged = gsb_ref[0] * _dot(osb_ref[0], wsb_ref[...]) + gdsa_ref[0] * _dot(odsa_ref[0], wdsa_ref[...])
    h = x_ref[...] + _dot(merged.astype(BF16), wout_ref[...])
    h_ref[...] = h
    ms = jnp.mean(h * h, axis=-1, keepdims=True)
    hn = h * lax.rsqrt(ms + EPS) * g2_ref[...]
    hn_ref[...] = hn.astype(BF16)

    hi, lo = _split_bf16(hn)
    logits = _dot(hi, wr_hi_ref[...]) + _dot(lo, wr_hi_ref[...]) + _dot(hi, wr_lo_ref[...])
    lane = lax.broadcasted_iota(jnp.int32, logits.shape, 1)
    big = jnp.int32(LANES)
    neg_inf = -jnp.inf

    def first_argmax(vals):
        top = jnp.max(vals, axis=-1, keepdims=True)
        idx = jnp.min(jnp.where(vals == top, lane, big), axis=-1, keepdims=True)
        return top, idx

    is_group = lane < n_groups
    gl = jnp.where(is_group, logits, neg_inf)
    ge = jnp.exp(gl - jnp.max(gl, axis=-1, keepdims=True))
    probs = jnp.where(is_group, ge / jnp.sum(ge, axis=-1, keepdims=True), neg_inf)
    gp, gi = first_argmax(probs)

    expert = lane - n_groups
    in_group = jnp.logical_and(expert >= gi * per_group, expert < (gi + 1) * per_group)
    vals = jnp.where(in_group, logits, neg_inf)
    ev0, i0 = first_argmax(vals)
    ev1, i1 = first_argmax(jnp.where(lane == i0, neg_inf, vals))
    e1 = jnp.exp(ev1 - ev0)
    w0 = gp / (1.0 + e1)
    w1 = gp * e1 / (1.0 + e1)
    comb_ref[...] = jnp.where(lane == i0, w0, jnp.where(lane == i1, w1, 0.0))


def _merge(x, osb, odsa, gsb, gdsa, w, row0s):
    B, R, D = x.shape
    tm = next(t for t in (512, 256, 128, 64, 32, 16) if R % t == 0)
    nt = R // tm
    shifted = lambda a, row0: pl.BlockSpec((pl.Element(1), pl.Element(tm), pl.Element(a.shape[2])),
                                           lambda b, i: (b, pl.multiple_of(row0 + i * tm, ROW_ALIGN), 0))
    full = lambda a: pl.BlockSpec(a.shape, lambda b, i: (0,) * a.ndim)
    out = lambda width: pl.BlockSpec((tm, width), lambda b, i: (b * nt + i, 0))
    weights = [w["wsb"], w["wdsa"], w["wout"], w["g2"], w["wr_hi"], w["wr_lo"]]
    in_specs = ([pl.BlockSpec((None, tm, D), lambda b, i: (b, i, 0))] + [shifted(a, r) for a, r in zip((osb, odsa, gsb, gdsa), row0s)]
                + [full(a) for a in weights])
    return pl.pallas_call(
        functools.partial(_merge_kernel, n_groups=w["n_groups"], per_group=w["per_group"]),
        grid=(B, nt), in_specs=in_specs,
        out_specs=[out(D), out(D), out(LANES)],
        out_shape=[jax.ShapeDtypeStruct((B * R, D), F32), jax.ShapeDtypeStruct((B * R, D), BF16),
                   jax.ShapeDtypeStruct((B * R, LANES), F32)],
        compiler_params=_cparams(("parallel", "parallel")), name="merge_route",
    )(x, osb, odsa, gsb, gdsa, *weights)


def _moe_kernel(h_ref, hn_ref, comb_ref, wg_ref, wu_ref, wd_ref, y_ref, hh_ref, *, n_groups):
    step = pl.program_id(1)
    n_exp, _, fd = wg_ref.shape

    @pl.when(step == 0)
    def _():
        y_ref[...] = h_ref[...]

    hn = hn_ref[...]
    comb = comb_ref[...]
    lane = lax.broadcasted_iota(jnp.int32, comb.shape, 1)
    for e in range(n_exp):
        ce = jnp.sum(jnp.where(lane == n_groups + step * n_exp + e, comb, 0.0), axis=-1, keepdims=True)
        a = _dot(hn, wg_ref[e])
        b = _dot(hn, wu_ref[e])
        hh_ref[:, e * fd:(e + 1) * fd] = (a * jax.nn.sigmoid(a) * b * ce).astype(BF16)
    y_ref[...] += _dot(hh_ref[...], wd_ref[...])


def _moe(h, hn, comb, w):
    N, D = h.shape
    E, _, Fd = w["wgate"].shape
    tm = next(t for t in (1024, 512, 256, 128, 64, 32, 16) if N % t == 0)
    ne = MOE_EXPERTS_PER_STEP
    assert E % ne == 0
    return pl.pallas_call(
        functools.partial(_moe_kernel, n_groups=w["n_groups"]),
        grid=(N // tm, E // ne),
        in_specs=[pl.BlockSpec((tm, D), lambda i, e: (i, 0)),
                  pl.BlockSpec((tm, D), lambda i, e: (i, 0)),
                  pl.BlockSpec((tm, LANES), lambda i, e: (i, 0)),
                  pl.BlockSpec((ne, D, Fd), lambda i, e: (e, 0, 0)),
                  pl.BlockSpec((ne, D, Fd), lambda i, e: (e, 0, 0)),
                  pl.BlockSpec((ne * Fd, D), lambda i, e: (e, 0))],
        out_specs=pl.BlockSpec((tm, D), lambda i, e: (i, 0)),
        out_shape=jax.ShapeDtypeStruct((N, D), F32),
        scratch_shapes=[pltpu.VMEM((tm, ne * Fd), BF16)],
        compiler_params=_cparams(("parallel", "arbitrary")), name="moe_experts",
    )(h, hn, comb, w["wgate"], w["wup"], w["wdown"].reshape(E * Fd, D))


def _prep_weights(norm1_g, w_in, qg, kg, w_sbb, w_dsab, w_out, norm2_g, w_group, w_router, w_gate, w_up, w_down,
                  dims):
    sb_w, dsa_w, kv_w, idx_w = dims
    D = w_in.shape[0]
    o_dv = 3 * sb_w + dsa_w + kv_w
    o_ik = o_dv + kv_w + idx_w
    o_iw = o_ik + HEAD_DIM
    o_g = o_iw + IDX_HEADS
    w_ik = w_in[:, o_ik:o_iw]
    n_groups = w_group.shape[1]
    n_experts = w_router.shape[1]
    assert n_groups + n_experts <= LANES
    wr = jnp.concatenate([w_group, w_router, jnp.zeros((D, LANES - n_groups - n_experts), F32)], axis=1)
    wr_hi, wr_lo = _split_bf16(wr)
    tile2 = lambda g: jnp.tile(g.astype(F32), LANES // HEAD_DIM)[None, :]
    head_id = jnp.arange(LANES) // HEAD_DIM
    return dict(
        g1=norm1_g.astype(F32)[None, :],
        wa=jnp.concatenate([w_in[:, :o_iw], w_ik], axis=1).astype(BF16),
        wg=w_in[:, o_g:].astype(BF16),
        wiw=jnp.pad(w_in[:, o_iw:o_g], ((0, 0), (0, LANES - IDX_HEADS))).astype(BF16),
        wdvt=w_in[:, o_dv:o_dv + kv_w].T.astype(BF16),
        qg=tile2(qg), kg=tile2(kg),
        gmat=((head_id[:, None] == head_id[None, :]).astype(F32) / HEAD_DIM).astype(BF16),
        wsb=w_sbb.astype(BF16), wdsa=w_dsab.astype(BF16), wout=w_out.astype(BF16),
        g2=norm2_g.astype(F32)[None, :], wr_hi=wr_hi, wr_lo=wr_lo,
        wgate=w_gate.astype(BF16), wup=w_up.astype(BF16), wdown=w_down.astype(BF16),
        n_groups=n_groups, per_group=n_experts // n_groups)


def _tail(x, osb, odsa, gsb, gdsa, w, row0s):
    h, hn, comb = _merge(x, osb, odsa, gsb, gdsa, w, row0s)
    return _moe(h, hn, comb, w).reshape(x.shape)


def kernel(x_prompt, x_sample, cache_sb_k, cache_sb_v, cache_dsa_k, cache_dsa_v, cache_idx_k, meta_tokens, norm1_g, w_in, dsa_q_norm_g, dsa_k_norm_g, w_sb_branch, w_dsa_branch, w_out, norm2_g, w_group, w_router, w_gate, w_up, w_down):
    B, S, D = x_prompt.shape
    Bs, T, _ = x_sample.shape
    depth, _, past, sb_heads, hd = cache_sb_k.shape
    kv_heads = cache_dsa_k.shape[3]
    n_meta = meta_tokens.shape[0]
    assert depth == 1 and hd == HEAD_DIM and cache_idx_k.shape[-1] == HEAD_DIM
    assert past % SLAB == 0 and T <= BLK and S % CHUNK == 0
    sb_w = sb_heads * HEAD_DIM
    dsa_w = w_dsa_branch.shape[1]
    kv_w = kv_heads * HEAD_DIM
    idx_w = IDX_HEADS * HEAD_DIM
    assert kv_w == LANES and sb_w % LANES == 0 and dsa_w % LANES == 0
    dims = (sb_w, dsa_w, kv_w, idx_w)
    w = _prep_weights(norm1_g[0], w_in[0], dsa_q_norm_g[0], dsa_k_norm_g[0], w_sb_branch[0], w_dsa_branch[0],
                      w_out[0], norm2_g[0], w_group[0], w_router[0], w_gate[0], w_up[0], w_down[0], dims)

    Lr = n_meta + S
    Lp = -(-Lr // SLAB) * SLAB
    meta = jnp.broadcast_to(meta_tokens[None].astype(x_prompt.dtype), (B, n_meta, D))
    xp = jnp.concatenate([meta, x_prompt, jnp.zeros((B, Lp - Lr, D), x_prompt.dtype)], axis=1)
    pr = _project(xp, jnp.arange(Lp, dtype=jnp.int32), w, dims, Lr)
    o_sb = _sb_attention(pr["sqb"], pr["skb"], pr["svb"], 0)
    assert S % SLAB == 0 and n_meta % ROW_ALIGN == 0 and n_meta < CHUNK
    o_dsa = _dsa_attention(pr["dqb"], pr["iqb"], pr["iw"], pr["dkb"], pr["dvt"], pr["ikb"], tq=SLAB, n_q=S // SLAB,
                           q_row0=n_meta, q_pos0=n_meta, chunk_off=n_meta, n_valid=Lr,
                           n_sel=min(TOPK_MAX, S // 4), kv_heads=kv_heads)
    y_prompt = _tail(x_prompt, o_sb, o_dsa, pr["gsb"], pr["gdsa"], w, (n_meta, 0, n_meta, n_meta))

    xs = jnp.concatenate([x_sample, jnp.zeros((Bs, SLAB - T, D), x_sample.dtype)], axis=1)
    sr = _project(xs, past + jnp.arange(SLAB, dtype=jnp.int32), w, dims, T)
    cat = lambda c, new: jnp.concatenate([c.astype(BF16), new], axis=1)
    k_sb = cat(cache_sb_k[0].reshape(Bs, past, sb_w), sr["skb"])
    v_sb = cat(cache_sb_v[0].reshape(Bs, past, sb_w), sr["svb"])
    k_ds = cat(cache_dsa_k[0].reshape(Bs, past, kv_w), sr["dkb"])
    k_ix = cat(jnp.tile(cache_idx_k[0], (1, 1, LANES // HEAD_DIM)), sr["ikb"])
    vt_c = cache_dsa_v[0].reshape(Bs, past // SLAB, SLAB, kv_w).transpose(0, 1, 3, 2).astype(BF16)
    vt_ds = jnp.concatenate([vt_c, sr["dvt"]], axis=1)
    qb = past // SLAB
    o_sb_s = _sb_attention(sr["sqb"], k_sb, v_sb, qb)
    o_dsa_s = _dsa_attention(sr["dqb"], sr["iqb"], sr["iw"], k_ds, vt_ds, k_ix, tq=BLK, n_q=1, q_row0=0, q_pos0=past,
                             chunk_off=0, n_valid=past + T, n_sel=min(TOPK_MAX, (past + T) // 4),
                             kv_heads=kv_heads)
    y_sample = _tail(x_sample, o_sb_s, o_dsa_s, sr["gsb"], sr["gdsa"], w, (0, 0, 0, 0))

    heads = lambda a, n: a.reshape(1, a.shape[0], a.shape[1], n, HEAD_DIM)
    return (y_prompt, y_sample,
            heads(pr["skf"], sb_heads), heads(pr["svf"], sb_heads),
            heads(pr["dkf"], kv_heads), heads(pr["dvf"], kv_heads), pr["ikf"][None],
            heads(sr["skf"], sb_heads), heads(sr["svf"], sb_heads),
            heads(sr["dkf"], kv_heads), heads(sr["dvf"], kv_heads), sr["ikf"][None])
```

```python
import functools

import jax
import jax.numpy as jnp
from jax import lax
from jax.experimental import pallas as pl
from jax.experimental.pallas import tpu as pltpu

F32 = jnp.float32
BF16 = jnp.bfloat16

CHUNK = 64
TOPK_MAX = 256
ROPE_THETA = 10000.0
EPS = 1e-6
HEAD_DIM = 64
IDX_HEADS = 8
EXPERT_TOPK = 2

LANES = 128
BLK = LANES
SLAB = 2 * BLK
SB_HEADS_PER_STEP = 4
MOE_EXPERTS_PER_STEP = 4
ROW_ALIGN = 16
VMEM_LIMIT = 56 * 1024 * 1024
INT_MIN = -(2 ** 31)
NEG_BIG = -1e30
LOG2E = 1.4426950408889634
SB_UNDERFLOW = -151.0


def _cparams(sem):
    return pltpu.CompilerParams(dimension_semantics=sem, vmem_limit_bytes=VMEM_LIMIT)


def _dot(a, b):
    return jnp.dot(a, b, preferred_element_type=F32)


def _dot_nt(a, b):
    return lax.dot_general(a, b, (((1,), (1,)), ((), ())), preferred_element_type=F32)


def _split_bf16(x):
    hi = x.astype(BF16)
    lo = (x - hi.astype(F32)).astype(BF16)
    return hi, lo


def _proj_kernel(x_ref, g1_ref, wa_ref, wg_ref, wiw_ref, wdvt_ref, cos_ref, sin_ref,
                 qg_ref, kg_ref, gmat_ref,
                 sqb_ref, skf_ref, svf_ref, skb_ref, svb_ref, dqb_ref, dkf_ref, dvf_ref,
                 dkb_ref, dvt_ref, iqb_ref, ikf_ref, ikb_ref, iw_ref, gsb_ref, gdsa_ref,
                 *, sb_w, dsa_w, kv_w, idx_w):
    x = x_ref[...]
    ms = jnp.mean(x * x, axis=-1, keepdims=True)
    xn = x * lax.rsqrt(ms + EPS) * g1_ref[...]
    xb = xn.astype(BF16)
    cos = cos_ref[...]
    sin = sin_ref[...]
    gmat = gmat_ref[...]
    lane = lax.broadcasted_iota(jnp.int32, cos.shape, 1)
    first_half = (lane % HEAD_DIM) < (HEAD_DIM // 2)

    def mm(lo, width):
        return _dot(xb, wa_ref[:, lo:lo + width])

    def head_norm(y, gain):
        hi, lo = _split_bf16(y * y)
        m = _dot(hi, gmat) + _dot(lo, gmat)
        return y * lax.rsqrt(m + EPS) * gain

    def rope(y):
        swapped = jnp.where(first_half, pltpu.roll(y, LANES - HEAD_DIM // 2, 1),
                            pltpu.roll(y, HEAD_DIM // 2, 1))
        return y * cos + swapped * sin

    scale = HEAD_DIM ** -0.5
    scale2 = scale * LOG2E
    off = 0
    sqb_ref[...] = (mm(off, sb_w) * scale2).astype(BF16)
    off += sb_w
    n_out = skf_ref.shape[0]
    sk = mm(off, sb_w)
    skf_ref[...] = sk[:n_out]
    skb_ref[...] = sk.astype(BF16)
    off += sb_w
    sv = mm(off, sb_w)
    svf_ref[...] = sv[:n_out]
    svb_ref[...] = sv.astype(BF16)
    off += sb_w
    for c in range(dsa_w // LANES):
        y = rope(head_norm(mm(off + c * LANES, LANES), qg_ref[...]))
        dqb_ref[:, c * LANES:(c + 1) * LANES] = (y * scale2).astype(BF16)
    off += dsa_w
    for c in range(kv_w // LANES):
        y = rope(head_norm(mm(off + c * LANES, LANES), kg_ref[...]))
        dkf_ref[:, c * LANES:(c + 1) * LANES] = y[:n_out]
        dkb_ref[:, c * LANES:(c + 1) * LANES] = y.astype(BF16)
    off += kv_w
    dvf_ref[...] = mm(off, kv_w)[:n_out]
    off += kv_w
    for c in range(idx_w // LANES):
        y = rope(mm(off + c * LANES, LANES))
        iqb_ref[:, c * LANES:(c + 1) * LANES] = (y * scale).astype(BF16)
    off += idx_w
    y = rope(mm(off, LANES))
    ikf_ref[...] = y[:n_out, :HEAD_DIM]
    ikb_ref[...] = y.astype(BF16)
    iw_ref[...] = _dot(xb, wiw_ref[...])
    dvt_ref[...] = _dot_nt(wdvt_ref[...], xb).astype(BF16)
    d = gsb_ref.shape[-1]
    gsb_ref[...] = jax.nn.sigmoid(_dot(xb, wg_ref[:, :d]))
    gdsa_ref[...] = jax.nn.sigmoid(_dot(xb, wg_ref[:, d:]))


def _rope_tables(pos):
    half = HEAD_DIM // 2
    freqs = ROPE_THETA ** (-jnp.arange(half, dtype=F32) / half)
    ang = pos.astype(F32)[:, None] * freqs[None, :]
    cos, sin = jnp.cos(ang), jnp.sin(ang)
    cos_t = jnp.tile(cos, (1, LANES // half))
    sin_t = jnp.tile(jnp.concatenate([-sin, sin], axis=1), (1, LANES // HEAD_DIM))
    return cos_t, sin_t


def _project(xpad, pos, w, dims, n_real):
    B, L, D = xpad.shape
    sb_w, dsa_w, kv_w, idx_w = dims
    tm = SLAB
    cos_t, sin_t = _rope_tables(pos)
    grid = (B, L // tm)
    row = lambda width: pl.BlockSpec((None, tm, width), lambda b, i: (b, i, 0))
    full = lambda a: pl.BlockSpec(a.shape, lambda b, i: (0,) * a.ndim)
    tab = pl.BlockSpec((tm, LANES), lambda b, i: (i, 0))
    out_shapes = dict(
        sqb=(BF16, sb_w), skf=(F32, sb_w), svf=(F32, sb_w), skb=(BF16, sb_w), svb=(BF16, sb_w),
        dqb=(BF16, dsa_w), dkf=(F32, kv_w), dvf=(F32, kv_w), dkb=(BF16, kv_w))
    names = ["sqb", "skf", "svf", "skb", "svb", "dqb", "dkf", "dvf", "dkb", "dvt", "iqb", "ikf",
             "ikb", "iw", "gsb", "gdsa"]
    shapes, specs = [], []
    f32_rows = min(tm, n_real)
    for n in names:
        if n in out_shapes:
            dt, width = out_shapes[n]
        elif n == "dvt":
            shapes.append(jax.ShapeDtypeStruct((B, L // SLAB, kv_w, SLAB), BF16))
            specs.append(pl.BlockSpec((None, None, kv_w, SLAB), lambda b, i: (b, i, 0, 0)))
            continue
        elif n == "iqb":
            dt, width = BF16, idx_w
        elif n == "ikf":
            dt, width = F32, HEAD_DIM
        elif n == "ikb":
            dt, width = BF16, LANES
        elif n == "iw":
            dt, width = F32, LANES
        else:
            dt, width = F32, D
        if n in ("skf", "svf", "dkf", "dvf", "ikf"):
            shapes.append(jax.ShapeDtypeStruct((B, n_real, width), dt))
            specs.append(pl.BlockSpec((None, f32_rows, width), lambda b, i: (b, i, 0)))
            continue
        shapes.append(jax.ShapeDtypeStruct((B, L, width), dt))
        specs.append(row(width))
    ins = [xpad, w["g1"], w["wa"], w["wg"], w["wiw"], w["wdvt"], cos_t, sin_t, w["qg"], w["kg"], w["gmat"]]
    in_specs = [row(D), full(w["g1"]), full(w["wa"]), full(w["wg"]), full(w["wiw"]), full(w["wdvt"]),
                tab, tab, full(w["qg"]), full(w["kg"]), full(w["gmat"])]
    outs = pl.pallas_call(
        functools.partial(_proj_kernel, sb_w=sb_w, dsa_w=dsa_w, kv_w=kv_w, idx_w=idx_w),
        grid=grid, in_specs=in_specs, out_specs=specs, out_shape=shapes,
        compiler_params=_cparams(("parallel", "parallel")), name="proj",
    )(*ins)
    return dict(zip(names, outs))


def _sb_kernel(q_ref, k_ref, v_ref, uo_ref, o_ref, *, q0):
    qi = q0 + pl.program_id(2)
    q = q_ref[...]
    uo = uo_ref[...]
    n_heads = q.shape[1] // HEAD_DIM
    per_slab = LANES // HEAD_DIM
    causal = (lax.broadcasted_iota(jnp.int32, (SLAB, SLAB), 1) < lax.broadcasted_iota(jnp.int32, (SLAB, SLAB), 0))
    v_lane_head = lax.broadcasted_iota(jnp.int32, (SLAB, LANES), 1) // HEAD_DIM

    def fold(j, cs, accs, diagonal):
        start = pl.multiple_of(j * SLAB, SLAB)
        kslab = k_ref[pl.ds(start, SLAB), :]
        vslab = v_ref[pl.ds(start, SLAB), :]
        cs_out, accs = [], list(accs)
        for h in range(n_heads):
            hs = slice(h * HEAD_DIM, (h + 1) * HEAD_DIM)
            z = _dot_nt(q[:, hs], kslab[:, hs])
            ls_pos = jnp.minimum(z, 0.0) - jnp.log2(1.0 + jnp.exp2(-jnp.abs(z)))
            ls_neg = ls_pos - z
            if diagonal:
                ls_neg = jnp.where(causal, ls_neg, 0.0)
            hi, lo = _split_bf16(ls_neg)
            r = _dot(hi, uo) + _dot(lo, uo)
            wgt = jnp.exp2(ls_pos + r[:, :SLAB] + jnp.concatenate([cs[h]] * (SLAB // BLK), axis=1))
            if diagonal:
                wgt = jnp.where(causal, wgt, 0.0)
            vs = vslab[:, (h // per_slab) * LANES:(h // per_slab + 1) * LANES]
            vh = jnp.where(v_lane_head == h % per_slab, vs, jnp.zeros_like(vs))
            accs[h // per_slab] = accs[h // per_slab] + _dot(wgt.astype(BF16), vh)
            cs_out.append(cs[h] + r[:, SLAB:])
        cmax = functools.reduce(jnp.maximum, [jnp.max(c) for c in cs_out])
        return tuple(cs_out), tuple(accs), cmax

    zeros = jnp.zeros((SLAB, BLK), F32)
    cs, accs, cmax = fold(qi, (zeros,) * n_heads, (jnp.zeros((SLAB, LANES), F32),) * (n_heads // per_slab), True)

    def body(carry):
        j, cs, accs, _ = carry
        cs, accs, cmax = fold(j, cs, accs, False)
        return j - 1, cs, accs, cmax

    def cond(carry):
        j, _, _, cmax = carry
        return jnp.logical_and(j >= 0, cmax > SB_UNDERFLOW)

    _, _, accs, _ = lax.while_loop(cond, body, (qi - 1, cs, accs, cmax))
    o_ref[...] = jnp.concatenate(accs, axis=1).astype(o_ref.dtype)


def _sb_attention(q, k, v, q0):
    B, Lq, W = q.shape
    Lk = k.shape[1]
    uo = jnp.concatenate([jnp.tril(jnp.ones((SLAB, SLAB), F32), -1), jnp.ones((SLAB, BLK), F32)], axis=1).astype(BF16)
    wb = SB_HEADS_PER_STEP * HEAD_DIM
    grid = (B, W // wb, Lq // SLAB)
    return pl.pallas_call(
        functools.partial(_sb_kernel, q0=q0),
        grid=grid,
        in_specs=[pl.BlockSpec((None, SLAB, wb), lambda b, h, i: (b, i, h)),
                  pl.BlockSpec((None, Lk, wb), lambda b, h, i: (b, 0, h)),
                  pl.BlockSpec((None, Lk, wb), lambda b, h, i: (b, 0, h)),
                  pl.BlockSpec(uo.shape, lambda b, h, i: (0, 0))],
        out_specs=pl.BlockSpec((None, SLAB, wb), lambda b, h, i: (b, i, h)),
        out_shape=jax.ShapeDtypeStruct((B, Lq, W), BF16),
        compiler_params=_cparams(("parallel", "parallel", "arbitrary")), name="sb_attn",
    )(q, k, v, uo)


def _dsa_kernel(q_ref, iq_ref, iw_ref, k_ref, vt_ref, ik_ref, lt_ref, o_ref,
                keys_ref, acc_ref, *, q_pos0, last_rows, nslab_total, chunk_off, n_valid, n_sel, n_heads, kv_heads):
    tq = q_ref.shape[1]
    q_start = q_pos0 + pl.program_id(1) * tq
    nslab = jnp.minimum((q_start + tq) // SLAB + 1, nslab_total)
    group = n_heads // kv_heads
    idx_w_scale = IDX_HEADS ** -0.5

    qpos = q_start + lax.broadcasted_iota(jnp.int32, (1, tq), 1)
    chunk_end = chunk_off + CHUNK * (jnp.right_shift(qpos - chunk_off, CHUNK.bit_length() - 1) + 1)
    key_limit = jnp.minimum(jnp.where(qpos < chunk_off, chunk_off, chunk_end), n_valid)
    slab_row = lax.broadcasted_iota(jnp.int32, (SLAB, tq), 0)

    iwt = iw_ref[0].T
    by_head = lambda x, heads: jnp.concatenate([x[:, h * HEAD_DIM:(h + 1) * HEAD_DIM] for h in heads], axis=0)
    iq_rows = by_head(iq_ref[0], range(IDX_HEADS))

    def score_keys(ik_rows, pos0):
        rows = ik_rows.shape[0]
        s = _dot_nt(ik_rows[:, :HEAD_DIM], iq_rows)
        acc = jnp.zeros((rows, tq), F32)
        for h in range(IDX_HEADS):
            acc = acc + jnp.maximum(s[:, h * tq:(h + 1) * tq], 0.0) * iwt[h:h + 1, :]
        bits = lax.bitcast_convert_type(acc * idx_w_scale, jnp.int32)
        key = bits ^ ((bits >> 31) & 0x7FFFFFFF)
        row = lax.broadcasted_iota(jnp.int32, (rows, tq), 0)
        return jnp.where(row < key_limit - pos0, key, INT_MIN)

    def score_slab(j, _):
        start = pl.multiple_of(j * SLAB, SLAB)
        keys_ref[pl.ds(start, SLAB), :] = score_keys(ik_ref[pl.ds(start, SLAB), :], start)
        return 0

    nwhole = nslab - 1
    lax.fori_loop(0, nwhole // 2, lambda t, _: score_slab(2 * t + 1, score_slab(2 * t, 0)), 0)
    lax.fori_loop(2 * (nwhole // 2), nwhole, score_slab, 0)
    last_start = pl.multiple_of(nwhole * SLAB, SLAB)
    keys_ref[pl.ds(last_start, last_rows), :] = score_keys(ik_ref[pl.ds(last_start, last_rows), :], last_start)

    def select_threshold(lanes):
        def count_ge(cand):
            def body(j, cnt):
                start = pl.multiple_of(j * SLAB, SLAB)
                hit = jnp.where(keys_ref[pl.ds(start, SLAB), lanes] >= cand, 1.0, 0.0)
                return cnt + hit[:BLK] + hit[BLK:]
            cnt = lax.fori_loop(0, nslab - 1, body, jnp.zeros((BLK, BLK), F32))
            last = pl.multiple_of((nslab - 1) * SLAB, SLAB)
            hit = jnp.where(keys_ref[pl.ds(last, last_rows), lanes] >= cand, 1.0, 0.0)
            return jnp.sum(cnt, axis=0, keepdims=True) + jnp.sum(hit, axis=0, keepdims=True)

        def search(it, thr):
            cand = thr + jnp.left_shift(jnp.int32(1), 31 - it)
            return jnp.where(count_ge(cand) >= n_sel, cand, thr)

        thr = lax.fori_loop(0, 32, search, jnp.full((1, BLK), INT_MIN, jnp.int32))
        return thr, n_sel - count_ge(thr + 1)

    halves = [select_threshold(slice(c * BLK, (c + 1) * BLK)) for c in range(tq // BLK)]
    thr = jnp.concatenate([t for t, _ in halves], axis=1)
    room = jnp.concatenate([r for _, r in halves], axis=1)

    acc_ref[...] = jnp.zeros(acc_ref.shape, F32)
    q = q_ref[0]
    q_rows = [by_head(q, range(kvh * group, (kvh + 1) * group)) for kvh in range(kv_heads)]
    lt = lt_ref[...]

    def attend(kk, kj, vtj, carry):
        eq_seen, m_all, l_all = carry
        rows = kk.shape[0]
        eq = kk == thr
        prefix = _dot(lt[:rows, :rows], jnp.where(eq, 1.0, 0.0).astype(BF16)) + eq_seen
        sel = jnp.logical_or(kk > thr, jnp.logical_and(eq, prefix <= room))
        sel = jnp.logical_and(sel, kk != INT_MIN)
        bias = jnp.where(sel, 0.0, NEG_BIG)
        m_out, l_out = [], []
        for kvh in range(kv_heads):
            s_all = _dot_nt(kj[:, kvh * HEAD_DIM:(kvh + 1) * HEAD_DIM], q_rows[kvh])
            ps, alphas = [], []
            for g in range(group):
                h = kvh * group + g
                s = s_all[:, g * tq:(g + 1) * tq] + bias
                m_new = jnp.maximum(m_all[h], jnp.max(s, axis=0, keepdims=True))
                p = jnp.exp2(s - m_new)
                alpha = jnp.exp2(m_all[h] - m_new)
                l_out.append(alpha * l_all[h] + jnp.sum(p, axis=0, keepdims=True))
                m_out.append(m_new)
                ps.append(p.astype(BF16))
                alphas.append(alpha)
            pv = _dot(vtj[kvh * HEAD_DIM:(kvh + 1) * HEAD_DIM, :], jnp.concatenate(ps, axis=1))
            acc_ref[kvh] = jnp.concatenate(alphas, axis=1) * acc_ref[kvh] + pv
        return prefix[rows - 1:rows, :], tuple(m_out), tuple(l_out)

    def attend_slab(j, carry):
        start = pl.multiple_of(j * SLAB, SLAB)
        return attend(keys_ref[pl.ds(start, SLAB), :], k_ref[pl.ds(start, SLAB), :], vt_ref[j], carry)

    init = (jnp.zeros((1, tq), F32), (jnp.full((1, tq), NEG_BIG, F32),) * n_heads,
            (jnp.zeros((1, tq), F32),) * n_heads)
    carry = lax.fori_loop(0, nwhole // 2, lambda t, c: attend_slab(2 * t + 1, attend_slab(2 * t, c)), init)
    carry = lax.fori_loop(2 * (nwhole // 2), nwhole, attend_slab, carry)
    _, _, l_all = attend(keys_ref[pl.ds(last_start, last_rows), :], k_ref[pl.ds(last_start, last_rows), :],
                         vt_ref[nwhole][:, :last_rows], carry)

    outs = []
    for kvh in range(kv_heads):
        acc = acc_ref[kvh]
        for g in range(group):
            outs.append(acc[:, g * tq:(g + 1) * tq] / l_all[kvh * group + g])
    o_ref[...] = jnp.concatenate(outs, axis=0).T.astype(o_ref.dtype)


def _dsa_attention(q, iq, iw, k, vt, ik, *, tq, n_q, q_row0, q_pos0, last_rows, chunk_off, n_valid, n_sel, kv_heads):
    B, _, W = q.shape
    Lk = k.shape[1]
    n_heads = W // HEAD_DIM
    lt = jnp.tril(jnp.ones((SLAB, SLAB), F32)).astype(BF16)
    kern = functools.partial(_dsa_kernel, q_pos0=q_pos0, last_rows=last_rows, nslab_total=Lk // SLAB,
                             chunk_off=chunk_off, n_valid=n_valid, n_sel=n_sel, n_heads=n_heads, kv_heads=kv_heads)
    rows = lambda a: pl.BlockSpec((pl.Element(1), pl.Element(tq), pl.Element(a.shape[2])),
                                  lambda b, i: (b, pl.multiple_of(q_row0 + i * tq, ROW_ALIGN), 0))
    return pl.pallas_call(
        kern,
        grid=(B, n_q),
        in_specs=[rows(q), rows(iq), rows(iw),
                  pl.BlockSpec((None, Lk, k.shape[2]), lambda b, i: (b, 0, 0)),
                  pl.BlockSpec((None,) + vt.shape[1:], lambda b, i: (b, 0, 0, 0)),
                  pl.BlockSpec((None, Lk, ik.shape[2]), lambda b, i: (b, 0, 0)),
                  pl.BlockSpec(lt.shape, lambda b, i: (0, 0))],
        out_specs=pl.BlockSpec((None, tq, W), lambda b, i: (b, i, 0)),
        out_shape=jax.ShapeDtypeStruct((B, n_q * tq, W), BF16),
        scratch_shapes=[pltpu.VMEM((Lk, tq), jnp.int32),
                        pltpu.VMEM((kv_heads, HEAD_DIM, (n_heads // kv_heads) * tq), F32)],
        compiler_params=_cparams(("parallel", "arbitrary")), name="dsa_attn",
    )(q, iq, iw, k, vt, ik, lt)


def _merge_kernel(x_ref, osb_ref, odsa_ref, gsb_ref, gdsa_ref, wsb_ref, wdsa_ref, wout_ref, g2_ref,
                  wr_hi_ref, wr_lo_ref, h_ref, hn_ref, comb_ref, *, n_groups, per_group):
    merged = gsb_ref[0] * _dot(osb_ref[0], wsb_ref[...]) + gdsa_ref[0] * _dot(odsa_ref[0], wdsa_ref[...])
    h = x_ref[...] + _dot(merged.astype(BF16), wout_ref[...])
    h_ref[...] = h
    ms = jnp.mean(h * h, axis=-1, keepdims=True)
    hn = h * lax.rsqrt(ms + EPS) * g2_ref[...]
    hn_ref[...] = hn.astype(BF16)

    hi, lo = _split_bf16(hn)
    logits = _dot(hi, wr_hi_ref[...]) + _dot(lo, wr_hi_ref[...]) + _dot(hi, wr_lo_ref[...])
    lane = lax.broadcasted_iota(jnp.int32, logits.shape, 1)
    big = jnp.int32(LANES)
    neg_inf = -jnp.inf

    def first_argmax(vals):
        top = jnp.max(vals, axis=-1, keepdims=True)
        idx = jnp.min(jnp.where(vals == top, lane, big), axis=-1, keepdims=True)
        return top, idx

    is_group = lane < n_groups
    gl = jnp.where(is_group, logits, neg_inf)
    ge = jnp.exp(gl - jnp.max(gl, axis=-1, keepdims=True))
    probs = jnp.where(is_group, ge / jnp.sum(ge, axis=-1, keepdims=True), neg_inf)
    gp, gi = first_argmax(probs)

    expert = lane - n_groups
    in_group = jnp.logical_and(expert >= gi * per_group, expert < (gi + 1) * per_group)
    vals = jnp.where(in_group, logits, neg_inf)
    ev0, i0 = first_argmax(vals)
    ev1, i1 = first_argmax(jnp.where(lane == i0, neg_inf, vals))
    e1 = jnp.exp(ev1 - ev0)
    w0 = gp / (1.0 + e1)
    w1 = gp * e1 / (1.0 + e1)
    comb_ref[...] = jnp.where(lane == i0, w0, jnp.where(lane == i1, w1, 0.0))


def _merge(x, osb, odsa, gsb, gdsa, w, row0s):
    B, R, D = x.shape
    tm = next(t for t in (512, 256, 128, 64, 32, 16) if R % t == 0)
    nt = R // tm
    shifted = lambda a, row0: pl.BlockSpec((pl.Element(1), pl.Element(tm), pl.Element(a.shape[2])),
                                           lambda b, i: (b, pl.multiple_of(row0 + i * tm, ROW_ALIGN), 0))
    full = lambda a: pl.BlockSpec(a.shape, lambda b, i: (0,) * a.ndim)
    out = lambda width: pl.BlockSpec((tm, width), lambda b, i: (b * nt + i, 0))
    weights = [w["wsb"], w["wdsa"], w["wout"], w["g2"], w["wr_hi"], w["wr_lo"]]
    in_specs = ([pl.BlockSpec((None, tm, D), lambda b, i: (b, i, 0))] + [shifted(a, r) for a, r in zip((osb, odsa, gsb, gdsa), row0s)]
                + [full(a) for a in weights])
    return pl.pallas_call(
        functools.partial(_merge_kernel, n_groups=w["n_groups"], per_group=w["per_group"]),
        grid=(B, nt), in_specs=in_specs,
        out_specs=[out(D), out(D), out(LANES)],
        out_shape=[jax.ShapeDtypeStruct((B * R, D), F32), jax.ShapeDtypeStruct((B * R, D), BF16),
                   jax.ShapeDtypeStruct((B * R, LANES), F32)],
        compiler_params=_cparams(("parallel", "parallel")), name="merge_route",
    )(x, osb, odsa, gsb, gdsa, *weights)


def _moe_kernel(h_ref, hn_ref, comb_ref, wg_ref, wu_ref, wd_ref, y_ref, hh_ref, *, n_groups):
    step = pl.program_id(1)
    n_exp, _, fd = wg_ref.shape

    @pl.when(step == 0)
    def _():
        y_ref[...] = h_ref[...]

    hn = hn_ref[...]
    comb = comb_ref[...]
    lane = lax.broadcasted_iota(jnp.int32, comb.shape, 1)
    for e in range(n_exp):
        ce = jnp.sum(jnp.where(lane == n_groups + step * n_exp + e, comb, 0.0), axis=-1, keepdims=True)
        a = _dot(hn, wg_ref[e])
        b = _dot(hn, wu_ref[e])
        hh_ref[:, e * fd:(e + 1) * fd] = (a * jax.nn.sigmoid(a) * b * ce).astype(BF16)
    y_ref[...] += _dot(hh_ref[...], wd_ref[...])


def _moe(h, hn, comb, w):
    N, D = h.shape
    E, _, Fd = w["wgate"].shape
    tm = next(t for t in (1024, 512, 256, 128, 64, 32, 16) if N % t == 0)
    ne = MOE_EXPERTS_PER_STEP
    assert E % ne == 0
    return pl.pallas_call(
        functools.partial(_moe_kernel, n_groups=w["n_groups"]),
        grid=(N // tm, E // ne),
        in_specs=[pl.BlockSpec((tm, D), lambda i, e: (i, 0)),
                  pl.BlockSpec((tm, D), lambda i, e: (i, 0)),
                  pl.BlockSpec((tm, LANES), lambda i, e: (i, 0)),
                  pl.BlockSpec((ne, D, Fd), lambda i, e: (e, 0, 0)),
                  pl.BlockSpec((ne, D, Fd), lambda i, e: (e, 0, 0)),
                  pl.BlockSpec((ne * Fd, D), lambda i, e: (e, 0))],
        out_specs=pl.BlockSpec((tm, D), lambda i, e: (i, 0)),
        out_shape=jax.ShapeDtypeStruct((N, D), F32),
        scratch_shapes=[pltpu.VMEM((tm, ne * Fd), BF16)],
        compiler_params=_cparams(("parallel", "arbitrary")), name="moe_experts",
    )(h, hn, comb, w["wgate"], w["wup"], w["wdown"].reshape(E * Fd, D))


def _prep_weights(norm1_g, w_in, qg, kg, w_sbb, w_dsab, w_out, norm2_g, w_group, w_router, w_gate, w_up, w_down,
                  dims):
    sb_w, dsa_w, kv_w, idx_w = dims
    D = w_in.shape[0]
    o_dv = 3 * sb_w + dsa_w + kv_w
    o_ik = o_dv + kv_w + idx_w
    o_iw = o_ik + HEAD_DIM
    o_g = o_iw + IDX_HEADS
    w_ik = w_in[:, o_ik:o_iw]
    n_groups = w_group.shape[1]
    n_experts = w_router.shape[1]
    assert n_groups + n_experts <= LANES
    wr = jnp.concatenate([w_group, w_router, jnp.zeros((D, LANES - n_groups - n_experts), F32)], axis=1)
    wr_hi, wr_lo = _split_bf16(wr)
    tile2 = lambda g: jnp.tile(g.astype(F32), LANES // HEAD_DIM)[None, :]
    head_id = jnp.arange(LANES) // HEAD_DIM
    return dict(
        g1=norm1_g.astype(F32)[None, :],
        wa=jnp.concatenate([w_in[:, :o_iw], w_ik], axis=1).astype(BF16),
        wg=w_in[:, o_g:].astype(BF16),
        wiw=jnp.pad(w_in[:, o_iw:o_g], ((0, 0), (0, LANES - IDX_HEADS))).astype(BF16),
        wdvt=w_in[:, o_dv:o_dv + kv_w].T.astype(BF16),
        qg=tile2(qg), kg=tile2(kg),
        gmat=((head_id[:, None] == head_id[None, :]).astype(F32) / HEAD_DIM).astype(BF16),
        wsb=w_sbb.astype(BF16), wdsa=w_dsab.astype(BF16), wout=w_out.astype(BF16),
        g2=norm2_g.astype(F32)[None, :], wr_hi=wr_hi, wr_lo=wr_lo,
        wgate=w_gate.astype(BF16), wup=w_up.astype(BF16), wdown=w_down.astype(BF16),
        n_groups=n_groups, per_group=n_experts // n_groups)


def _tail(x, osb, odsa, gsb, gdsa, w, row0s):
    h, hn, comb = _merge(x, osb, odsa, gsb, gdsa, w, row0s)
    return _moe(h, hn, comb, w).reshape(x.shape)


def kernel(x_prompt, x_sample, cache_sb_k, cache_sb_v, cache_dsa_k, cache_dsa_v, cache_idx_k, meta_tokens, norm1_g, w_in, dsa_q_norm_g, dsa_k_norm_g, w_sb_branch, w_dsa_branch, w_out, norm2_g, w_group, w_router, w_gate, w_up, w_down):
    B, S, D = x_prompt.shape
    Bs, T, _ = x_sample.shape
    depth, _, past, sb_heads, hd = cache_sb_k.shape
    kv_heads = cache_dsa_k.shape[3]
    n_meta = meta_tokens.shape[0]
    assert depth == 1 and hd == HEAD_DIM and cache_idx_k.shape[-1] == HEAD_DIM
    assert past % SLAB == 0 and T <= BLK and S % CHUNK == 0
    sb_w = sb_heads * HEAD_DIM
    dsa_w = w_dsa_branch.shape[1]
    kv_w = kv_heads * HEAD_DIM
    idx_w = IDX_HEADS * HEAD_DIM
    assert kv_w == LANES and sb_w % LANES == 0 and dsa_w % LANES == 0
    dims = (sb_w, dsa_w, kv_w, idx_w)
    w = _prep_weights(norm1_g[0], w_in[0], dsa_q_norm_g[0], dsa_k_norm_g[0], w_sb_branch[0], w_dsa_branch[0],
                      w_out[0], norm2_g[0], w_group[0], w_router[0], w_gate[0], w_up[0], w_down[0], dims)

    Lr = n_meta + S
    Lp = -(-Lr // SLAB) * SLAB
    meta = jnp.broadcast_to(meta_tokens[None].astype(x_prompt.dtype), (B, n_meta, D))
    xp = jnp.concatenate([meta, x_prompt, jnp.zeros((B, Lp - Lr, D), x_prompt.dtype)], axis=1)
    pr = _project(xp, jnp.arange(Lp, dtype=jnp.int32), w, dims, Lr)
    o_sb = _sb_attention(pr["sqb"], pr["skb"], pr["svb"], 0)
    assert S % SLAB == 0 and n_meta % ROW_ALIGN == 0 and n_meta < CHUNK
    o_dsa = _dsa_attention(pr["dqb"], pr["iqb"], pr["iw"], pr["dkb"], pr["dvt"], pr["ikb"], tq=SLAB, n_q=S // SLAB,
                           q_row0=n_meta, q_pos0=n_meta, last_rows=n_meta, chunk_off=n_meta, n_valid=Lr,
                           n_sel=min(TOPK_MAX, S // 4), kv_heads=kv_heads)
    y_prompt = _tail(x_prompt, o_sb, o_dsa, pr["gsb"], pr["gdsa"], w, (n_meta, 0, n_meta, n_meta))

    xs = jnp.concatenate([x_sample, jnp.zeros((Bs, SLAB - T, D), x_sample.dtype)], axis=1)
    sr = _project(xs, past + jnp.arange(SLAB, dtype=jnp.int32), w, dims, T)
    cat = lambda c, new: jnp.concatenate([c.astype(BF16), new], axis=1)
    k_sb = cat(cache_sb_k[0].reshape(Bs, past, sb_w), sr["skb"])
    v_sb = cat(cache_sb_v[0].reshape(Bs, past, sb_w), sr["svb"])
    k_ds = cat(cache_dsa_k[0].reshape(Bs, past, kv_w), sr["dkb"])
    k_ix = cat(jnp.tile(cache_idx_k[0], (1, 1, LANES // HEAD_DIM)), sr["ikb"])
    vt_c = cache_dsa_v[0].reshape(Bs, past // SLAB, SLAB, kv_w).transpose(0, 1, 3, 2).astype(BF16)
    vt_ds = jnp.concatenate([vt_c, sr["dvt"]], axis=1)
    qb = past // SLAB
    o_sb_s = _sb_attention(sr["sqb"], k_sb, v_sb, qb)
    o_dsa_s = _dsa_attention(sr["dqb"], sr["iqb"], sr["iw"], k_ds, vt_ds, k_ix, tq=BLK, n_q=1, q_row0=0, q_pos0=past,
                             last_rows=-(-T // ROW_ALIGN) * ROW_ALIGN, chunk_off=0, n_valid=past + T, n_sel=min(TOPK_MAX, (past + T) // 4),
                             kv_heads=kv_heads)
    y_sample = _tail(x_sample, o_sb_s, o_dsa_s, sr["gsb"], sr["gdsa"], w, (0, 0, 0, 0))

    heads = lambda a, n: a.reshape(1, a.shape[0], a.shape[1], n, HEAD_DIM)
    return (y_prompt, y_sample,
            heads(pr["skf"], sb_heads), heads(pr["svf"], sb_heads),
            heads(pr["dkf"], kv_heads), heads(pr["dvf"], kv_heads), pr["ikf"][None],
            heads(sr["skf"], sb_heads), heads(sr["svf"], sb_heads),
            heads(sr["dkf"], kv_heads), heads(sr["dvf"], kv_heads), sr["ikf"][None])
```

```python
import functools

import jax
import jax.numpy as jnp
from jax import lax
from jax.experimental import pallas as pl
from jax.experimental.pallas import tpu as pltpu

F32 = jnp.float32
BF16 = jnp.bfloat16

CHUNK = 64
TOPK_MAX = 256
ROPE_THETA = 10000.0
EPS = 1e-6
HEAD_DIM = 64
IDX_HEADS = 8
EXPERT_TOPK = 2

LANES = 128
BLK = LANES
SLAB = 2 * BLK
SB_HEADS_PER_STEP = 8
MOE_EXPERTS_PER_STEP = 4
ROW_ALIGN = 16
VMEM_LIMIT = 56 * 1024 * 1024
INT_MIN = -(2 ** 31)
NEG_BIG = -1e30
LOG2E = 1.4426950408889634
SB_UNDERFLOW = -151.0


def _cparams(sem):
    return pltpu.CompilerParams(dimension_semantics=sem, vmem_limit_bytes=VMEM_LIMIT)


def _dot(a, b):
    return jnp.dot(a, b, preferred_element_type=F32)


def _dot_nt(a, b):
    return lax.dot_general(a, b, (((1,), (1,)), ((), ())), preferred_element_type=F32)


def _split_bf16(x):
    hi = x.astype(BF16)
    lo = (x - hi.astype(F32)).astype(BF16)
    return hi, lo


def _proj_kernel(x_ref, g1_ref, wa_ref, wg_ref, wiw_ref, wdvt_ref, cos_ref, sin_ref,
                 qg_ref, kg_ref, gmat_ref,
                 sqb_ref, skf_ref, svf_ref, skb_ref, svb_ref, dqb_ref, dkf_ref, dvf_ref,
                 dkb_ref, dvt_ref, iqb_ref, ikf_ref, ikb_ref, iw_ref, gsb_ref, gdsa_ref,
                 *, sb_w, dsa_w, kv_w, idx_w):
    x = x_ref[...]
    ms = jnp.mean(x * x, axis=-1, keepdims=True)
    xn = x * lax.rsqrt(ms + EPS) * g1_ref[...]
    xb = xn.astype(BF16)
    cos = cos_ref[...]
    sin = sin_ref[...]
    gmat = gmat_ref[...]
    lane = lax.broadcasted_iota(jnp.int32, cos.shape, 1)
    first_half = (lane % HEAD_DIM) < (HEAD_DIM // 2)

    def mm(lo, width):
        return _dot(xb, wa_ref[:, lo:lo + width])

    def head_norm(y, gain):
        hi, lo = _split_bf16(y * y)
        m = _dot(hi, gmat) + _dot(lo, gmat)
        return y * lax.rsqrt(m + EPS) * gain

    def rope(y):
        swapped = jnp.where(first_half, pltpu.roll(y, LANES - HEAD_DIM // 2, 1),
                            pltpu.roll(y, HEAD_DIM // 2, 1))
        return y * cos + swapped * sin

    scale = HEAD_DIM ** -0.5
    scale2 = scale * LOG2E
    off = 0
    sqb_ref[...] = (mm(off, sb_w) * scale2).astype(BF16)
    off += sb_w
    n_out = skf_ref.shape[0]
    sk = mm(off, sb_w)
    skf_ref[...] = sk[:n_out]
    skb_ref[...] = sk.astype(BF16)
    off += sb_w
    sv = mm(off, sb_w)
    svf_ref[...] = sv[:n_out]
    svb_ref[...] = sv.astype(BF16)
    off += sb_w
    for c in range(dsa_w // LANES):
        y = rope(head_norm(mm(off + c * LANES, LANES), qg_ref[...]))
        dqb_ref[:, c * LANES:(c + 1) * LANES] = (y * scale2).astype(BF16)
    off += dsa_w
    for c in range(kv_w // LANES):
        y = rope(head_norm(mm(off + c * LANES, LANES), kg_ref[...]))
        dkf_ref[:, c * LANES:(c + 1) * LANES] = y[:n_out]
        dkb_ref[:, c * LANES:(c + 1) * LANES] = y.astype(BF16)
    off += kv_w
    dvf_ref[...] = mm(off, kv_w)[:n_out]
    off += kv_w
    for c in range(idx_w // LANES):
        y = rope(mm(off + c * LANES, LANES))
        iqb_ref[:, c * LANES:(c + 1) * LANES] = (y * scale).astype(BF16)
    off += idx_w
    y = rope(mm(off, LANES))
    ikf_ref[...] = y[:n_out, :HEAD_DIM]
    ikb_ref[...] = y.astype(BF16)
    iw_ref[...] = _dot(xb, wiw_ref[...])
    dvt_ref[...] = _dot_nt(wdvt_ref[...], xb).astype(BF16)
    d = gsb_ref.shape[-1]
    gsb_ref[...] = jax.nn.sigmoid(_dot(xb, wg_ref[:, :d]))
    gdsa_ref[...] = jax.nn.sigmoid(_dot(xb, wg_ref[:, d:]))


def _rope_tables(pos):
    half = HEAD_DIM // 2
    freqs = ROPE_THETA ** (-jnp.arange(half, dtype=F32) / half)
    ang = pos.astype(F32)[:, None] * freqs[None, :]
    cos, sin = jnp.cos(ang), jnp.sin(ang)
    cos_t = jnp.tile(cos, (1, LANES // half))
    sin_t = jnp.tile(jnp.concatenate([-sin, sin], axis=1), (1, LANES // HEAD_DIM))
    return cos_t, sin_t


def _project(xpad, pos, w, dims, n_real):
    B, L, D = xpad.shape
    sb_w, dsa_w, kv_w, idx_w = dims
    tm = SLAB
    cos_t, sin_t = _rope_tables(pos)
    grid = (B, L // tm)
    row = lambda width: pl.BlockSpec((None, tm, width), lambda b, i: (b, i, 0))
    full = lambda a: pl.BlockSpec(a.shape, lambda b, i: (0,) * a.ndim)
    tab = pl.BlockSpec((tm, LANES), lambda b, i: (i, 0))
    out_shapes = dict(
        sqb=(BF16, sb_w), skf=(F32, sb_w), svf=(F32, sb_w), skb=(BF16, sb_w), svb=(BF16, sb_w),
        dqb=(BF16, dsa_w), dkf=(F32, kv_w), dvf=(F32, kv_w), dkb=(BF16, kv_w))
    names = ["sqb", "skf", "svf", "skb", "svb", "dqb", "dkf", "dvf", "dkb", "dvt", "iqb", "ikf",
             "ikb", "iw", "gsb", "gdsa"]
    shapes, specs = [], []
    f32_rows = min(tm, n_real)
    for n in names:
        if n in out_shapes:
            dt, width = out_shapes[n]
        elif n == "dvt":
            shapes.append(jax.ShapeDtypeStruct((B, L // SLAB, kv_w, SLAB), BF16))
            specs.append(pl.BlockSpec((None, None, kv_w, SLAB), lambda b, i: (b, i, 0, 0)))
            continue
        elif n == "iqb":
            dt, width = BF16, idx_w
        elif n == "ikf":
            dt, width = F32, HEAD_DIM
        elif n == "ikb":
            dt, width = BF16, LANES
        elif n == "iw":
            dt, width = F32, LANES
        else:
            dt, width = F32, D
        if n in ("skf", "svf", "dkf", "dvf", "ikf"):
            shapes.append(jax.ShapeDtypeStruct((B, n_real, width), dt))
            specs.append(pl.BlockSpec((None, f32_rows, width), lambda b, i: (b, i, 0)))
            continue
        shapes.append(jax.ShapeDtypeStruct((B, L, width), dt))
        specs.append(row(width))
    ins = [xpad, w["g1"], w["wa"], w["wg"], w["wiw"], w["wdvt"], cos_t, sin_t, w["qg"], w["kg"], w["gmat"]]
    in_specs = [row(D), full(w["g1"]), full(w["wa"]), full(w["wg"]), full(w["wiw"]), full(w["wdvt"]),
                tab, tab, full(w["qg"]), full(w["kg"]), full(w["gmat"])]
    outs = pl.pallas_call(
        functools.partial(_proj_kernel, sb_w=sb_w, dsa_w=dsa_w, kv_w=kv_w, idx_w=idx_w),
        grid=grid, in_specs=in_specs, out_specs=specs, out_shape=shapes,
        compiler_params=_cparams(("parallel", "parallel")), name="proj",
    )(*ins)
    return dict(zip(names, outs))


def _sb_kernel(q_ref, k_ref, v_ref, uo_ref, o_ref, *, q0):
    qi = q0 + pl.program_id(2)
    q = q_ref[...]
    uo = uo_ref[...]
    n_heads = q.shape[1] // HEAD_DIM
    per_slab = LANES // HEAD_DIM
    causal = (lax.broadcasted_iota(jnp.int32, (SLAB, SLAB), 1) < lax.broadcasted_iota(jnp.int32, (SLAB, SLAB), 0))
    v_lane_head = lax.broadcasted_iota(jnp.int32, (SLAB, LANES), 1) // HEAD_DIM

    def fold(j, cs, accs, diagonal):
        start = pl.multiple_of(j * SLAB, SLAB)
        kslab = k_ref[pl.ds(start, SLAB), :]
        vslab = v_ref[pl.ds(start, SLAB), :]
        cs_out, accs = [], list(accs)
        for h in range(n_heads):
            hs = slice(h * HEAD_DIM, (h + 1) * HEAD_DIM)
            z = _dot_nt(q[:, hs], kslab[:, hs])
            ls_pos = jnp.minimum(z, 0.0) - jnp.log2(1.0 + jnp.exp2(-jnp.abs(z)))
            ls_neg = ls_pos - z
            if diagonal:
                ls_neg = jnp.where(causal, ls_neg, 0.0)
            hi, lo = _split_bf16(ls_neg)
            r = _dot(hi, uo) + _dot(lo, uo)
            wgt = jnp.exp2(ls_pos + r[:, :SLAB] + jnp.concatenate([cs[h]] * (SLAB // BLK), axis=1))
            if diagonal:
                wgt = jnp.where(causal, wgt, 0.0)
            vs = vslab[:, (h // per_slab) * LANES:(h // per_slab + 1) * LANES]
            vh = jnp.where(v_lane_head == h % per_slab, vs, jnp.zeros_like(vs))
            accs[h // per_slab] = accs[h // per_slab] + _dot(wgt.astype(BF16), vh)
            cs_out.append(cs[h] + r[:, SLAB:])
        cmax = functools.reduce(jnp.maximum, [jnp.max(c) for c in cs_out])
        return tuple(cs_out), tuple(accs), cmax

    zeros = jnp.zeros((SLAB, BLK), F32)
    cs, accs, cmax = fold(qi, (zeros,) * n_heads, (jnp.zeros((SLAB, LANES), F32),) * (n_heads // per_slab), True)

    def body(carry):
        j, cs, accs, _ = carry
        cs, accs, cmax = fold(j, cs, accs, False)
        return j - 1, cs, accs, cmax

    def cond(carry):
        j, _, _, cmax = carry
        return jnp.logical_and(j >= 0, cmax > SB_UNDERFLOW)

    _, _, accs, _ = lax.while_loop(cond, body, (qi - 1, cs, accs, cmax))
    o_ref[...] = jnp.concatenate(accs, axis=1).astype(o_ref.dtype)


def _sb_attention(q, k, v, q0):
    B, Lq, W = q.shape
    Lk = k.shape[1]
    uo = jnp.concatenate([jnp.tril(jnp.ones((SLAB, SLAB), F32), -1), jnp.ones((SLAB, BLK), F32)], axis=1).astype(BF16)
    wb = SB_HEADS_PER_STEP * HEAD_DIM
    grid = (B, W // wb, Lq // SLAB)
    return pl.pallas_call(
        functools.partial(_sb_kernel, q0=q0),
        grid=grid,
        in_specs=[pl.BlockSpec((None, SLAB, wb), lambda b, h, i: (b, i, h)),
                  pl.BlockSpec((None, Lk, wb), lambda b, h, i: (b, 0, h)),
                  pl.BlockSpec((None, Lk, wb), lambda b, h, i: (b, 0, h)),
                  pl.BlockSpec(uo.shape, lambda b, h, i: (0, 0))],
        out_specs=pl.BlockSpec((None, SLAB, wb), lambda b, h, i: (b, i, h)),
        out_shape=jax.ShapeDtypeStruct((B, Lq, W), BF16),
        compiler_params=_cparams(("parallel", "parallel", "arbitrary")), name="sb_attn",
    )(q, k, v, uo)


def _dsa_kernel(q_ref, iq_ref, iw_ref, k_ref, vt_ref, ik_ref, lt_ref, o_ref,
                keys_ref, acc_ref, *, q_pos0, last_rows, nslab_total, chunk_off, n_valid, n_sel, n_heads, kv_heads):
    tq = q_ref.shape[1]
    q_start = q_pos0 + pl.program_id(1) * tq
    nslab = jnp.minimum((q_start + tq) // SLAB + 1, nslab_total)
    group = n_heads // kv_heads
    idx_w_scale = IDX_HEADS ** -0.5

    qpos = q_start + lax.broadcasted_iota(jnp.int32, (1, tq), 1)
    chunk_end = chunk_off + CHUNK * (jnp.right_shift(qpos - chunk_off, CHUNK.bit_length() - 1) + 1)
    key_limit = jnp.minimum(jnp.where(qpos < chunk_off, chunk_off, chunk_end), n_valid)
    slab_row = lax.broadcasted_iota(jnp.int32, (SLAB, tq), 0)

    iwt = iw_ref[0].T
    by_head = lambda x, heads: jnp.concatenate([x[:, h * HEAD_DIM:(h + 1) * HEAD_DIM] for h in heads], axis=0)
    iq_rows = by_head(iq_ref[0], range(IDX_HEADS))

    def score_keys(ik_rows, pos0):
        rows = ik_rows.shape[0]
        s = _dot_nt(ik_rows[:, :HEAD_DIM], iq_rows)
        acc = jnp.zeros((rows, tq), F32)
        for h in range(IDX_HEADS):
            acc = acc + jnp.maximum(s[:, h * tq:(h + 1) * tq], 0.0) * iwt[h:h + 1, :]
        bits = lax.bitcast_convert_type(acc * idx_w_scale, jnp.int32)
        key = bits ^ ((bits >> 31) & 0x7FFFFFFF)
        row = lax.broadcasted_iota(jnp.int32, (rows, tq), 0)
        return jnp.where(row < key_limit - pos0, key, INT_MIN)

    def score_slab(j, _):
        start = pl.multiple_of(j * SLAB, SLAB)
        keys_ref[pl.ds(start, SLAB), :] = score_keys(ik_ref[pl.ds(start, SLAB), :], start)
        return 0

    nwhole = nslab - 1
    lax.fori_loop(0, nwhole // 2, lambda t, _: score_slab(2 * t + 1, score_slab(2 * t, 0)), 0)
    lax.fori_loop(2 * (nwhole // 2), nwhole, score_slab, 0)
    last_start = pl.multiple_of(nwhole * SLAB, SLAB)
    keys_ref[pl.ds(last_start, last_rows), :] = score_keys(ik_ref[pl.ds(last_start, last_rows), :], last_start)

    def select_threshold(lanes):
        def count_ge(cand):
            def body(j, cnt):
                start = pl.multiple_of(j * SLAB, SLAB)
                hit = jnp.where(keys_ref[pl.ds(start, SLAB), lanes] >= cand, 1.0, 0.0)
                return cnt + hit[:BLK] + hit[BLK:]
            def body4(t, c):
                for u in range(4):
                    c = body(4 * t + u, c)
                return c
            cnt = lax.fori_loop(0, nwhole // 4, body4, jnp.zeros((BLK, BLK), F32))
            cnt = lax.fori_loop(4 * (nwhole // 4), nwhole, body, cnt)
            last = pl.multiple_of((nslab - 1) * SLAB, SLAB)
            hit = jnp.where(keys_ref[pl.ds(last, last_rows), lanes] >= cand, 1.0, 0.0)
            return jnp.sum(cnt, axis=0, keepdims=True) + jnp.sum(hit, axis=0, keepdims=True)

        def search(it, thr):
            cand = thr + jnp.left_shift(jnp.int32(1), 31 - it)
            return jnp.where(count_ge(cand) >= n_sel, cand, thr)

        thr = lax.fori_loop(0, 32, search, jnp.full((1, BLK), INT_MIN, jnp.int32))
        return thr, n_sel - count_ge(thr + 1)

    halves = [select_threshold(slice(c * BLK, (c + 1) * BLK)) for c in range(tq // BLK)]
    thr = jnp.concatenate([t for t, _ in halves], axis=1)
    room = jnp.concatenate([r for _, r in halves], axis=1)

    acc_ref[...] = jnp.zeros(acc_ref.shape, F32)
    q = q_ref[0]
    q_rows = [by_head(q, range(kvh * group, (kvh + 1) * group)) for kvh in range(kv_heads)]
    lt = lt_ref[...]

    def attend(kk, kj, vtj, carry):
        eq_seen, m_all, l_all = carry
        rows = kk.shape[0]
        eq = kk == thr
        prefix = _dot(lt[:rows, :rows], jnp.where(eq, 1.0, 0.0).astype(BF16)) + eq_seen
        sel = jnp.logical_or(kk > thr, jnp.logical_and(eq, prefix <= room))
        sel = jnp.logical_and(sel, kk != INT_MIN)
        bias = jnp.where(sel, 0.0, NEG_BIG)
        m_out, l_out = [], []
        for kvh in range(kv_heads):
            s_all = _dot_nt(kj[:, kvh * HEAD_DIM:(kvh + 1) * HEAD_DIM], q_rows[kvh])
            ps, alphas = [], []
            for g in range(group):
                h = kvh * group + g
                s = s_all[:, g * tq:(g + 1) * tq] + bias
                m_new = jnp.maximum(m_all[h], jnp.max(s, axis=0, keepdims=True))
                p = jnp.exp2(s - m_new)
                alpha = jnp.exp2(m_all[h] - m_new)
                l_out.append(alpha * l_all[h] + jnp.sum(p, axis=0, keepdims=True))
                m_out.append(m_new)
                ps.append(p.astype(BF16))
                alphas.append(alpha)
            pv = _dot(vtj[kvh * HEAD_DIM:(kvh + 1) * HEAD_DIM, :], jnp.concatenate(ps, axis=1))
            acc_ref[kvh] = jnp.concatenate(alphas, axis=1) * acc_ref[kvh] + pv
        return prefix[rows - 1:rows, :], tuple(m_out), tuple(l_out)

    def attend_slab(j, carry):
        start = pl.multiple_of(j * SLAB, SLAB)
        return attend(keys_ref[pl.ds(start, SLAB), :], k_ref[pl.ds(start, SLAB), :], vt_ref[j], carry)

    init = (jnp.zeros((1, tq), F32), (jnp.full((1, tq), NEG_BIG, F32),) * n_heads,
            (jnp.zeros((1, tq), F32),) * n_heads)
    carry = lax.fori_loop(0, nwhole // 2, lambda t, c: attend_slab(2 * t + 1, attend_slab(2 * t, c)), init)
    carry = lax.fori_loop(2 * (nwhole // 2), nwhole, attend_slab, carry)
    _, _, l_all = attend(keys_ref[pl.ds(last_start, last_rows), :], k_ref[pl.ds(last_start, last_rows), :],
                         vt_ref[nwhole][:, :last_rows], carry)

    outs = []
    for kvh in range(kv_heads):
        acc = acc_ref[kvh]
        for g in range(group):
            outs.append(acc[:, g * tq:(g + 1) * tq] / l_all[kvh * group + g])
    o_ref[...] = jnp.concatenate(outs, axis=0).T.astype(o_ref.dtype)


def _dsa_attention(q, iq, iw, k, vt, ik, *, tq, n_q, q_row0, q_pos0, last_rows, chunk_off, n_valid, n_sel, kv_heads):
    B, _, W = q.shape
    Lk = k.shape[1]
    n_heads = W // HEAD_DIM
    lt = jnp.tril(jnp.ones((SLAB, SLAB), F32)).astype(BF16)
    kern = functools.partial(_dsa_kernel, q_pos0=q_pos0, last_rows=last_rows, nslab_total=Lk // SLAB,
                             chunk_off=chunk_off, n_valid=n_valid, n_sel=n_sel, n_heads=n_heads, kv_heads=kv_heads)
    rows = lambda a: pl.BlockSpec((pl.Element(1), pl.Element(tq), pl.Element(a.shape[2])),
                                  lambda b, i: (b, pl.multiple_of(q_row0 + i * tq, ROW_ALIGN), 0))
    return pl.pallas_call(
        kern,
        grid=(B, n_q),
        in_specs=[rows(q), rows(iq), rows(iw),
                  pl.BlockSpec((None, Lk, k.shape[2]), lambda b, i: (b, 0, 0)),
                  pl.BlockSpec((None,) + vt.shape[1:], lambda b, i: (b, 0, 0, 0)),
                  pl.BlockSpec((None, Lk, ik.shape[2]), lambda b, i: (b, 0, 0)),
                  pl.BlockSpec(lt.shape, lambda b, i: (0, 0))],
        out_specs=pl.BlockSpec((None, tq, W), lambda b, i: (b, i, 0)),
        out_shape=jax.ShapeDtypeStruct((B, n_q * tq, W), BF16),
        scratch_shapes=[pltpu.VMEM((Lk, tq), jnp.int32),
                        pltpu.VMEM((kv_heads, HEAD_DIM, (n_heads // kv_heads) * tq), F32)],
        compiler_params=_cparams(("parallel", "arbitrary")), name="dsa_attn",
    )(q, iq, iw, k, vt, ik, lt)


def _merge_kernel(x_ref, osb_ref, odsa_ref, gsb_ref, gdsa_ref, wsb_ref, wdsa_ref, wout_ref, g2_ref,
                  wr_hi_ref, wr_lo_ref, h_ref, hn_ref, comb_ref, *, n_groups, per_group):
    merged = gsb_ref[0] * _dot(osb_ref[0], wsb_ref[...]) + gdsa_ref[0] * _dot(odsa_ref[0], wdsa_ref[...])
    h = x_ref[...] + _dot(merged.astype(BF16), wout_ref[...])
    h_ref[...] = h
    ms = jnp.mean(h * h, axis=-1, keepdims=True)
    hn = h * lax.rsqrt(ms + EPS) * g2_ref[...]
    hn_ref[...] = hn.astype(BF16)

    hi, lo = _split_bf16(hn)
    logits = _dot(hi, wr_hi_ref[...]) + _dot(lo, wr_hi_ref[...]) + _dot(hi, wr_lo_ref[...])
    lane = lax.broadcasted_iota(jnp.int32, logits.shape, 1)
    big = jnp.int32(LANES)
    neg_inf = -jnp.inf

    def first_argmax(vals):
        top = jnp.max(vals, axis=-1, keepdims=True)
        idx = jnp.min(jnp.where(vals == top, lane, big), axis=-1, keepdims=True)
        return top, idx

    is_group = lane < n_groups
    gl = jnp.where(is_group, logits, neg_inf)
    ge = jnp.exp(gl - jnp.max(gl, axis=-1, keepdims=True))
    probs = jnp.where(is_group, ge / jnp.sum(ge, axis=-1, keepdims=True), neg_inf)
    gp, gi = first_argmax(probs)

    expert = lane - n_groups
    in_group = jnp.logical_and(expert >= gi * per_group, expert < (gi + 1) * per_group)
    vals = jnp.where(in_group, logits, neg_inf)
    ev0, i0 = first_argmax(vals)
    ev1, i1 = first_argmax(jnp.where(lane == i0, neg_inf, vals))
    e1 = jnp.exp(ev1 - ev0)
    w0 = gp / (1.0 + e1)
    w1 = gp * e1 / (1.0 + e1)
    comb_ref[...] = jnp.where(lane == i0, w0, jnp.where(lane == i1, w1, 0.0))


def _merge(x, osb, odsa, gsb, gdsa, w, row0s):
    B, R, D = x.shape
    tm = next(t for t in (512, 256, 128, 64, 32, 16) if R % t == 0)
    nt = R // tm
    shifted = lambda a, row0: pl.BlockSpec((pl.Element(1), pl.Element(tm), pl.Element(a.shape[2])),
                                           lambda b, i: (b, pl.multiple_of(row0 + i * tm, ROW_ALIGN), 0))
    full = lambda a: pl.BlockSpec(a.shape, lambda b, i: (0,) * a.ndim)
    out = lambda width: pl.BlockSpec((tm, width), lambda b, i: (b * nt + i, 0))
    weights = [w["wsb"], w["wdsa"], w["wout"], w["g2"], w["wr_hi"], w["wr_lo"]]
    in_specs = ([pl.BlockSpec((None, tm, D), lambda b, i: (b, i, 0))] + [shifted(a, r) for a, r in zip((osb, odsa, gsb, gdsa), row0s)]
                + [full(a) for a in weights])
    return pl.pallas_call(
        functools.partial(_merge_kernel, n_groups=w["n_groups"], per_group=w["per_group"]),
        grid=(B, nt), in_specs=in_specs,
        out_specs=[out(D), out(D), out(LANES)],
        out_shape=[jax.ShapeDtypeStruct((B * R, D), F32), jax.ShapeDtypeStruct((B * R, D), BF16),
                   jax.ShapeDtypeStruct((B * R, LANES), F32)],
        compiler_params=_cparams(("parallel", "parallel")), name="merge_route",
    )(x, osb, odsa, gsb, gdsa, *weights)


def _moe_kernel(h_ref, hn_ref, comb_ref, wg_ref, wu_ref, wd_ref, y_ref, hh_ref, *, n_groups):
    step = pl.program_id(1)
    n_exp, _, fd = wg_ref.shape

    @pl.when(step == 0)
    def _():
        y_ref[...] = h_ref[...]

    hn = hn_ref[...]
    comb = comb_ref[...]
    lane = lax.broadcasted_iota(jnp.int32, comb.shape, 1)
    for e in range(n_exp):
        ce = jnp.sum(jnp.where(lane == n_groups + step * n_exp + e, comb, 0.0), axis=-1, keepdims=True)
        a = _dot(hn, wg_ref[e])
        b = _dot(hn, wu_ref[e])
        hh_ref[:, e * fd:(e + 1) * fd] = (a * jax.nn.sigmoid(a) * b * ce).astype(BF16)
    y_ref[...] += _dot(hh_ref[...], wd_ref[...])


def _moe(h, hn, comb, w):
    N, D = h.shape
    E, _, Fd = w["wgate"].shape
    tm = next(t for t in (1024, 512, 256, 128, 64, 32, 16) if N % t == 0)
    ne = MOE_EXPERTS_PER_STEP
    assert E % ne == 0
    return pl.pallas_call(
        functools.partial(_moe_kernel, n_groups=w["n_groups"]),
        grid=(N // tm, E // ne),
        in_specs=[pl.BlockSpec((tm, D), lambda i, e: (i, 0)),
                  pl.BlockSpec((tm, D), lambda i, e: (i, 0)),
                  pl.BlockSpec((tm, LANES), lambda i, e: (i, 0)),
                  pl.BlockSpec((ne, D, Fd), lambda i, e: (e, 0, 0)),
                  pl.BlockSpec((ne, D, Fd), lambda i, e: (e, 0, 0)),
                  pl.BlockSpec((ne * Fd, D), lambda i, e: (e, 0))],
        out_specs=pl.BlockSpec((tm, D), lambda i, e: (i, 0)),
        out_shape=jax.ShapeDtypeStruct((N, D), F32),
        scratch_shapes=[pltpu.VMEM((tm, ne * Fd), BF16)],
        compiler_params=_cparams(("parallel", "arbitrary")), name="moe_experts",
    )(h, hn, comb, w["wgate"], w["wup"], w["wdown"].reshape(E * Fd, D))


def _prep_weights(norm1_g, w_in, qg, kg, w_sbb, w_dsab, w_out, norm2_g, w_group, w_router, w_gate, w_up, w_down,
                  dims):
    sb_w, dsa_w, kv_w, idx_w = dims
    D = w_in.shape[0]
    o_dv = 3 * sb_w + dsa_w + kv_w
    o_ik = o_dv + kv_w + idx_w
    o_iw = o_ik + HEAD_DIM
    o_g = o_iw + IDX_HEADS
    w_ik = w_in[:, o_ik:o_iw]
    n_groups = w_group.shape[1]
    n_experts = w_router.shape[1]
    assert n_groups + n_experts <= LANES
    wr = jnp.concatenate([w_group, w_router, jnp.zeros((D, LANES - n_groups - n_experts), F32)], axis=1)
    wr_hi, wr_lo = _split_bf16(wr)
    tile2 = lambda g: jnp.tile(g.astype(F32), LANES // HEAD_DIM)[None, :]
    head_id = jnp.arange(LANES) // HEAD_DIM
    return dict(
        g1=norm1_g.astype(F32)[None, :],
        wa=jnp.concatenate([w_in[:, :o_iw], w_ik], axis=1).astype(BF16),
        wg=w_in[:, o_g:].astype(BF16),
        wiw=jnp.pad(w_in[:, o_iw:o_g], ((0, 0), (0, LANES - IDX_HEADS))).astype(BF16),
        wdvt=w_in[:, o_dv:o_dv + kv_w].T.astype(BF16),
        qg=tile2(qg), kg=tile2(kg),
        gmat=((head_id[:, None] == head_id[None, :]).astype(F32) / HEAD_DIM).astype(BF16),
        wsb=w_sbb.astype(BF16), wdsa=w_dsab.astype(BF16), wout=w_out.astype(BF16),
        g2=norm2_g.astype(F32)[None, :], wr_hi=wr_hi, wr_lo=wr_lo,
        wgate=w_gate.astype(BF16), wup=w_up.astype(BF16), wdown=w_down.astype(BF16),
        n_groups=n_groups, per_group=n_experts // n_groups)


def _tail(x, osb, odsa, gsb, gdsa, w, row0s):
    h, hn, comb = _merge(x, osb, odsa, gsb, gdsa, w, row0s)
    return _moe(h, hn, comb, w).reshape(x.shape)


def kernel(x_prompt, x_sample, cache_sb_k, cache_sb_v, cache_dsa_k, cache_dsa_v, cache_idx_k, meta_tokens, norm1_g, w_in, dsa_q_norm_g, dsa_k_norm_g, w_sb_branch, w_dsa_branch, w_out, norm2_g, w_group, w_router, w_gate, w_up, w_down):
    B, S, D = x_prompt.shape
    Bs, T, _ = x_sample.shape
    depth, _, past, sb_heads, hd = cache_sb_k.shape
    kv_heads = cache_dsa_k.shape[3]
    n_meta = meta_tokens.shape[0]
    assert depth == 1 and hd == HEAD_DIM and cache_idx_k.shape[-1] == HEAD_DIM
    assert past % SLAB == 0 and T <= BLK and S % CHUNK == 0
    sb_w = sb_heads * HEAD_DIM
    dsa_w = w_dsa_branch.shape[1]
    kv_w = kv_heads * HEAD_DIM
    idx_w = IDX_HEADS * HEAD_DIM
    assert kv_w == LANES and sb_w % LANES == 0 and dsa_w % LANES == 0
    dims = (sb_w, dsa_w, kv_w, idx_w)
    w = _prep_weights(norm1_g[0], w_in[0], dsa_q_norm_g[0], dsa_k_norm_g[0], w_sb_branch[0], w_dsa_branch[0],
                      w_out[0], norm2_g[0], w_group[0], w_router[0], w_gate[0], w_up[0], w_down[0], dims)

    Lr = n_meta + S
    Lp = -(-Lr // SLAB) * SLAB
    meta = jnp.broadcast_to(meta_tokens[None].astype(x_prompt.dtype), (B, n_meta, D))
    xp = jnp.concatenate([meta, x_prompt, jnp.zeros((B, Lp - Lr, D), x_prompt.dtype)], axis=1)
    pr = _project(xp, jnp.arange(Lp, dtype=jnp.int32), w, dims, Lr)
    o_sb = _sb_attention(pr["sqb"], pr["skb"], pr["svb"], 0)
    assert S % SLAB == 0 and n_meta % ROW_ALIGN == 0 and n_meta < CHUNK
    o_dsa = _dsa_attention(pr["dqb"], pr["iqb"], pr["iw"], pr["dkb"], pr["dvt"], pr["ikb"], tq=SLAB, n_q=S // SLAB,
                           q_row0=n_meta, q_pos0=n_meta, last_rows=n_meta, chunk_off=n_meta, n_valid=Lr,
                           n_sel=min(TOPK_MAX, S // 4), kv_heads=kv_heads)
    y_prompt = _tail(x_prompt, o_sb, o_dsa, pr["gsb"], pr["gdsa"], w, (n_meta, 0, n_meta, n_meta))

    xs = jnp.concatenate([x_sample, jnp.zeros((Bs, SLAB - T, D), x_sample.dtype)], axis=1)
    sr = _project(xs, past + jnp.arange(SLAB, dtype=jnp.int32), w, dims, T)
    cat = lambda c, new: jnp.concatenate([c.astype(BF16), new], axis=1)
    k_sb = cat(cache_sb_k[0].reshape(Bs, past, sb_w), sr["skb"])
    v_sb = cat(cache_sb_v[0].reshape(Bs, past, sb_w), sr["svb"])
    k_ds = cat(cache_dsa_k[0].reshape(Bs, past, kv_w), sr["dkb"])
    k_ix = cat(jnp.tile(cache_idx_k[0], (1, 1, LANES // HEAD_DIM)), sr["ikb"])
    vt_c = cache_dsa_v[0].reshape(Bs, past // SLAB, SLAB, kv_w).transpose(0, 1, 3, 2).astype(BF16)
    vt_ds = jnp.concatenate([vt_c, sr["dvt"]], axis=1)
    qb = past // SLAB
    o_sb_s = _sb_attention(sr["sqb"], k_sb, v_sb, qb)
    o_dsa_s = _dsa_attention(sr["dqb"], sr["iqb"], sr["iw"], k_ds, vt_ds, k_ix, tq=BLK, n_q=1, q_row0=0, q_pos0=past,
                             last_rows=-(-T // ROW_ALIGN) * ROW_ALIGN, chunk_off=0, n_valid=past + T, n_sel=min(TOPK_MAX, (past + T) // 4),
                             kv_heads=kv_heads)
    y_sample = _tail(x_sample, o_sb_s, o_dsa_s, sr["gsb"], sr["gdsa"], w, (0, 0, 0, 0))

    heads = lambda a, n: a.reshape(1, a.shape[0], a.shape[1], n, HEAD_DIM)
    return (y_prompt, y_sample,
            heads(pr["skf"], sb_heads), heads(pr["svf"], sb_heads),
            heads(pr["dkf"], kv_heads), heads(pr["dvf"], kv_heads), pr["ikf"][None],
            heads(sr["skf"], sb_heads), heads(sr["svf"], sb_heads),
            heads(sr["dkf"], kv_heads), heads(sr["dvf"], kv_heads), sr["ikf"][None])
```

```python
import functools

import jax
import jax.numpy as jnp
from jax import lax
from jax.experimental import pallas as pl
from jax.experimental.pallas import tpu as pltpu

F32 = jnp.float32
BF16 = jnp.bfloat16

CHUNK = 64
TOPK_MAX = 256
ROPE_THETA = 10000.0
EPS = 1e-6
HEAD_DIM = 64
IDX_HEADS = 8
EXPERT_TOPK = 2

LANES = 128
BLK = LANES
SLAB = 2 * BLK
SB_HEADS_PER_STEP = 8
MOE_EXPERTS_PER_STEP = 4
ROW_ALIGN = 16
VMEM_LIMIT = 56 * 1024 * 1024
INT_MIN = -(2 ** 31)
NEG_BIG = -1e30
LOG2E = 1.4426950408889634
SB_UNDERFLOW = -151.0


def _cparams(sem):
    return pltpu.CompilerParams(dimension_semantics=sem, vmem_limit_bytes=VMEM_LIMIT)


def _dot(a, b):
    return jnp.dot(a, b, preferred_element_type=F32)


def _dot_nt(a, b):
    return lax.dot_general(a, b, (((1,), (1,)), ((), ())), preferred_element_type=F32)


def _split_bf16(x):
    hi = x.astype(BF16)
    lo = (x - hi.astype(F32)).astype(BF16)
    return hi, lo


def _proj_kernel(x_ref, g1_ref, wa_ref, wg_ref, wiw_ref, wdvt_ref, cos_ref, sin_ref,
                 qg_ref, kg_ref, gmat_ref,
                 sqb_ref, skf_ref, svf_ref, skb_ref, svb_ref, dqb_ref, dkf_ref, dvf_ref,
                 dkb_ref, dvt_ref, iqb_ref, ikf_ref, ikb_ref, iw_ref, gsb_ref, gdsa_ref,
                 *, sb_w, dsa_w, kv_w, idx_w):
    x = x_ref[...]
    ms = jnp.mean(x * x, axis=-1, keepdims=True)
    xn = x * lax.rsqrt(ms + EPS) * g1_ref[...]
    xb = xn.astype(BF16)
    cos = cos_ref[...]
    sin = sin_ref[...]
    gmat = gmat_ref[...]
    lane = lax.broadcasted_iota(jnp.int32, cos.shape, 1)
    first_half = (lane % HEAD_DIM) < (HEAD_DIM // 2)

    def mm(lo, width):
        return _dot(xb, wa_ref[:, lo:lo + width])

    def head_norm(y, gain):
        hi, lo = _split_bf16(y * y)
        m = _dot(hi, gmat) + _dot(lo, gmat)
        return y * lax.rsqrt(m + EPS) * gain

    def rope(y):
        swapped = jnp.where(first_half, pltpu.roll(y, LANES - HEAD_DIM // 2, 1),
                            pltpu.roll(y, HEAD_DIM // 2, 1))
        return y * cos + swapped * sin

    scale = HEAD_DIM ** -0.5
    scale2 = scale * LOG2E
    off = 0
    sqb_ref[...] = (mm(off, sb_w) * scale2).astype(BF16)
    off += sb_w
    n_out = skf_ref.shape[0]
    sk = mm(off, sb_w)
    skf_ref[...] = sk[:n_out]
    skb_ref[...] = sk.astype(BF16)
    off += sb_w
    sv = mm(off, sb_w)
    svf_ref[...] = sv[:n_out]
    svb_ref[...] = sv.astype(BF16)
    off += sb_w
    for c in range(dsa_w // LANES):
        y = rope(head_norm(mm(off + c * LANES, LANES), qg_ref[...]))
        dqb_ref[:, c * LANES:(c + 1) * LANES] = (y * scale2).astype(BF16)
    off += dsa_w
    for c in range(kv_w // LANES):
        y = rope(head_norm(mm(off + c * LANES, LANES), kg_ref[...]))
        dkf_ref[:, c * LANES:(c + 1) * LANES] = y[:n_out]
        dkb_ref[:, c * LANES:(c + 1) * LANES] = y.astype(BF16)
    off += kv_w
    dvf_ref[...] = mm(off, kv_w)[:n_out]
    off += kv_w
    for c in range(idx_w // LANES):
        y = rope(mm(off + c * LANES, LANES))
        iqb_ref[:, c * LANES:(c + 1) * LANES] = (y * scale).astype(BF16)
    off += idx_w
    y = rope(mm(off, LANES))
    ikf_ref[...] = y[:n_out, :HEAD_DIM]
    ikb_ref[...] = y.astype(BF16)
    iw_ref[...] = _dot(xb, wiw_ref[...])
    dvt_ref[...] = _dot_nt(wdvt_ref[...], xb).astype(BF16)
    d = gsb_ref.shape[-1]
    gsb_ref[...] = jax.nn.sigmoid(_dot(xb, wg_ref[:, :d]))
    gdsa_ref[...] = jax.nn.sigmoid(_dot(xb, wg_ref[:, d:]))


def _rope_tables(pos):
    half = HEAD_DIM // 2
    freqs = ROPE_THETA ** (-jnp.arange(half, dtype=F32) / half)
    ang = pos.astype(F32)[:, None] * freqs[None, :]
    cos, sin = jnp.cos(ang), jnp.sin(ang)
    cos_t = jnp.tile(cos, (1, LANES // half))
    sin_t = jnp.tile(jnp.concatenate([-sin, sin], axis=1), (1, LANES // HEAD_DIM))
    return cos_t, sin_t


def _project(xpad, pos, w, dims, n_real):
    B, L, D = xpad.shape
    sb_w, dsa_w, kv_w, idx_w = dims
    tm = SLAB
    cos_t, sin_t = _rope_tables(pos)
    grid = (B, L // tm)
    row = lambda width: pl.BlockSpec((None, tm, width), lambda b, i: (b, i, 0))
    full = lambda a: pl.BlockSpec(a.shape, lambda b, i: (0,) * a.ndim)
    tab = pl.BlockSpec((tm, LANES), lambda b, i: (i, 0))
    out_shapes = dict(
        sqb=(BF16, sb_w), skf=(F32, sb_w), svf=(F32, sb_w), skb=(BF16, sb_w), svb=(BF16, sb_w),
        dqb=(BF16, dsa_w), dkf=(F32, kv_w), dvf=(F32, kv_w), dkb=(BF16, kv_w))
    names = ["sqb", "skf", "svf", "skb", "svb", "dqb", "dkf", "dvf", "dkb", "dvt", "iqb", "ikf",
             "ikb", "iw", "gsb", "gdsa"]
    shapes, specs = [], []
    f32_rows = min(tm, n_real)
    for n in names:
        if n in out_shapes:
            dt, width = out_shapes[n]
        elif n == "dvt":
            shapes.append(jax.ShapeDtypeStruct((B, L // SLAB, kv_w, SLAB), BF16))
            specs.append(pl.BlockSpec((None, None, kv_w, SLAB), lambda b, i: (b, i, 0, 0)))
            continue
        elif n == "iqb":
            dt, width = BF16, idx_w
        elif n == "ikf":
            dt, width = F32, HEAD_DIM
        elif n == "ikb":
            dt, width = BF16, LANES
        elif n == "iw":
            dt, width = F32, LANES
        else:
            dt, width = F32, D
        if n in ("skf", "svf", "dkf", "dvf", "ikf"):
            shapes.append(jax.ShapeDtypeStruct((B, n_real, width), dt))
            specs.append(pl.BlockSpec((None, f32_rows, width), lambda b, i: (b, i, 0)))
            continue
        shapes.append(jax.ShapeDtypeStruct((B, L, width), dt))
        specs.append(row(width))
    ins = [xpad, w["g1"], w["wa"], w["wg"], w["wiw"], w["wdvt"], cos_t, sin_t, w["qg"], w["kg"], w["gmat"]]
    in_specs = [row(D), full(w["g1"]), full(w["wa"]), full(w["wg"]), full(w["wiw"]), full(w["wdvt"]),
                tab, tab, full(w["qg"]), full(w["kg"]), full(w["gmat"])]
    outs = pl.pallas_call(
        functools.partial(_proj_kernel, sb_w=sb_w, dsa_w=dsa_w, kv_w=kv_w, idx_w=idx_w),
        grid=grid, in_specs=in_specs, out_specs=specs, out_shape=shapes,
        compiler_params=_cparams(("parallel", "parallel")), name="proj",
    )(*ins)
    return dict(zip(names, outs))


def _sb_kernel(q_ref, k_ref, v_ref, uo_ref, o_ref, *, q0):
    qi = q0 + pl.program_id(2)
    q = q_ref[...]
    uo = uo_ref[...]
    n_heads = q.shape[1] // HEAD_DIM
    per_slab = LANES // HEAD_DIM
    causal = (lax.broadcasted_iota(jnp.int32, (SLAB, SLAB), 1) < lax.broadcasted_iota(jnp.int32, (SLAB, SLAB), 0))
    v_lane_head = lax.broadcasted_iota(jnp.int32, (SLAB, LANES), 1) // HEAD_DIM

    def fold(j, cs, accs, diagonal):
        start = pl.multiple_of(j * SLAB, SLAB)
        kslab = k_ref[pl.ds(start, SLAB), :]
        vslab = v_ref[pl.ds(start, SLAB), :]
        cs_out, accs = [], list(accs)
        for h in range(n_heads):
            hs = slice(h * HEAD_DIM, (h + 1) * HEAD_DIM)
            z = _dot_nt(q[:, hs], kslab[:, hs])
            ls_pos = jnp.minimum(z, 0.0) - jnp.log2(1.0 + jnp.exp2(-jnp.abs(z)))
            ls_neg = ls_pos - z
            if diagonal:
                ls_neg = jnp.where(causal, ls_neg, 0.0)
            hi, lo = _split_bf16(ls_neg)
            r = _dot(hi, uo) + _dot(lo, uo)
            wgt = jnp.exp2(ls_pos + r[:, :SLAB] + jnp.concatenate([cs[h]] * (SLAB // BLK), axis=1))
            if diagonal:
                wgt = jnp.where(causal, wgt, 0.0)
            vs = vslab[:, (h // per_slab) * LANES:(h // per_slab + 1) * LANES]
            vh = jnp.where(v_lane_head == h % per_slab, vs, jnp.zeros_like(vs))
            accs[h // per_slab] = accs[h // per_slab] + _dot(wgt.astype(BF16), vh)
            cs_out.append(cs[h] + r[:, SLAB:])
        cmax = functools.reduce(jnp.maximum, [jnp.max(c) for c in cs_out])
        return tuple(cs_out), tuple(accs), cmax

    zeros = jnp.zeros((SLAB, BLK), F32)
    cs, accs, cmax = fold(qi, (zeros,) * n_heads, (jnp.zeros((SLAB, LANES), F32),) * (n_heads // per_slab), True)

    def body(carry):
        j, cs, accs, _ = carry
        cs, accs, cmax = fold(j, cs, accs, False)
        return j - 1, cs, accs, cmax

    def cond(carry):
        j, _, _, cmax = carry
        return jnp.logical_and(j >= 0, cmax > SB_UNDERFLOW)

    _, _, accs, _ = lax.while_loop(cond, body, (qi - 1, cs, accs, cmax))
    o_ref[...] = jnp.concatenate(accs, axis=1).astype(o_ref.dtype)


def _sb_attention(q, k, v, q0):
    B, Lq, W = q.shape
    Lk = k.shape[1]
    uo = jnp.concatenate([jnp.tril(jnp.ones((SLAB, SLAB), F32), -1), jnp.ones((SLAB, BLK), F32)], axis=1).astype(BF16)
    wb = SB_HEADS_PER_STEP * HEAD_DIM
    grid = (B, W // wb, Lq // SLAB)
    return pl.pallas_call(
        functools.partial(_sb_kernel, q0=q0),
        grid=grid,
        in_specs=[pl.BlockSpec((None, SLAB, wb), lambda b, h, i: (b, i, h)),
                  pl.BlockSpec((None, Lk, wb), lambda b, h, i: (b, 0, h)),
                  pl.BlockSpec((None, Lk, wb), lambda b, h, i: (b, 0, h)),
                  pl.BlockSpec(uo.shape, lambda b, h, i: (0, 0))],
        out_specs=pl.BlockSpec((None, SLAB, wb), lambda b, h, i: (b, i, h)),
        out_shape=jax.ShapeDtypeStruct((B, Lq, W), BF16),
        compiler_params=_cparams(("parallel", "parallel", "arbitrary")), name="sb_attn",
    )(q, k, v, uo)


def _dsa_kernel(q_ref, iq_ref, iw_ref, k_ref, vt_ref, ik_ref, lt_ref, o_ref,
                keys_ref, acc_ref, *, q_pos0, last_rows, nslab_total, chunk_off, n_valid, n_sel, n_heads, kv_heads):
    tq = q_ref.shape[1]
    q_start = q_pos0 + pl.program_id(1) * tq
    nslab = jnp.minimum((q_start + tq) // SLAB + 1, nslab_total)
    group = n_heads // kv_heads
    idx_w_scale = IDX_HEADS ** -0.5

    qpos = q_start + lax.broadcasted_iota(jnp.int32, (1, tq), 1)
    chunk_end = chunk_off + CHUNK * (jnp.right_shift(qpos - chunk_off, CHUNK.bit_length() - 1) + 1)
    key_limit = jnp.minimum(jnp.where(qpos < chunk_off, chunk_off, chunk_end), n_valid)
    slab_row = lax.broadcasted_iota(jnp.int32, (SLAB, tq), 0)

    iwt = iw_ref[0].T
    by_head = lambda x, heads: jnp.concatenate([x[:, h * HEAD_DIM:(h + 1) * HEAD_DIM] for h in heads], axis=0)
    iq_rows = by_head(iq_ref[0], range(IDX_HEADS))

    def score_keys(ik_rows, pos0):
        rows = ik_rows.shape[0]
        s = _dot_nt(ik_rows[:, :HEAD_DIM], iq_rows)
        acc = jnp.zeros((rows, tq), F32)
        for h in range(IDX_HEADS):
            acc = acc + jnp.maximum(s[:, h * tq:(h + 1) * tq], 0.0) * iwt[h:h + 1, :]
        bits = lax.bitcast_convert_type(acc * idx_w_scale, jnp.int32)
        key = bits ^ ((bits >> 31) & 0x7FFFFFFF)
        row = lax.broadcasted_iota(jnp.int32, (rows, tq), 0)
        return jnp.where(row < key_limit - pos0, key, INT_MIN)

    def score_slab(j, _):
        start = pl.multiple_of(j * SLAB, SLAB)
        keys_ref[pl.ds(start, SLAB), :] = score_keys(ik_ref[pl.ds(start, SLAB), :], start)
        return 0

    nwhole = nslab - 1
    def score4(t, _):
        for u in range(4):
            score_slab(4 * t + u, 0)
        return 0
    lax.fori_loop(0, nwhole // 4, score4, 0)
    lax.fori_loop(4 * (nwhole // 4), nwhole, score_slab, 0)
    last_start = pl.multiple_of(nwhole * SLAB, SLAB)
    keys_ref[pl.ds(last_start, last_rows), :] = score_keys(ik_ref[pl.ds(last_start, last_rows), :], last_start)

    def select_threshold(lanes):
        def count_ge(cand):
            def body(j, cnt):
                start = pl.multiple_of(j * SLAB, SLAB)
                hit = jnp.where(keys_ref[pl.ds(start, SLAB), lanes] >= cand, 1.0, 0.0)
                return cnt + hit[:BLK] + hit[BLK:]
            def body4(t, c):
                for u in range(4):
                    c = body(4 * t + u, c)
                return c
            cnt = lax.fori_loop(0, nwhole // 4, body4, jnp.zeros((BLK, BLK), F32))
            cnt = lax.fori_loop(4 * (nwhole // 4), nwhole, body, cnt)
            last = pl.multiple_of((nslab - 1) * SLAB, SLAB)
            hit = jnp.where(keys_ref[pl.ds(last, last_rows), lanes] >= cand, 1.0, 0.0)
            return jnp.sum(cnt, axis=0, keepdims=True) + jnp.sum(hit, axis=0, keepdims=True)

        def search(it, thr):
            cand = thr + jnp.left_shift(jnp.int32(1), 31 - it)
            return jnp.where(count_ge(cand) >= n_sel, cand, thr)

        thr = lax.fori_loop(0, 32, search, jnp.full((1, BLK), INT_MIN, jnp.int32))
        return thr, n_sel - count_ge(thr + 1)

    halves = [select_threshold(slice(c * BLK, (c + 1) * BLK)) for c in range(tq // BLK)]
    thr = jnp.concatenate([t for t, _ in halves], axis=1)
    room = jnp.concatenate([r for _, r in halves], axis=1)

    acc_ref[...] = jnp.zeros(acc_ref.shape, F32)
    q = q_ref[0]
    q_rows = [by_head(q, range(kvh * group, (kvh + 1) * group)) for kvh in range(kv_heads)]
    lt = lt_ref[...]

    def attend(kk, kj, vtj, carry):
        eq_seen, m_all, l_all = carry
        rows = kk.shape[0]
        eq = kk == thr
        prefix = _dot(lt[:rows, :rows], jnp.where(eq, 1.0, 0.0).astype(BF16)) + eq_seen
        sel = jnp.logical_or(kk > thr, jnp.logical_and(eq, prefix <= room))
        sel = jnp.logical_and(sel, kk != INT_MIN)
        bias = jnp.where(sel, 0.0, NEG_BIG)
        m_out, l_out = [], []
        for kvh in range(kv_heads):
            s_all = _dot_nt(kj[:, kvh * HEAD_DIM:(kvh + 1) * HEAD_DIM], q_rows[kvh])
            ps, alphas = [], []
            for g in range(group):
                h = kvh * group + g
                s = s_all[:, g * tq:(g + 1) * tq] + bias
                m_new = jnp.maximum(m_all[h], jnp.max(s, axis=0, keepdims=True))
                p = jnp.exp2(s - m_new)
                alpha = jnp.exp2(m_all[h] - m_new)
                l_out.append(alpha * l_all[h] + jnp.sum(p, axis=0, keepdims=True))
                m_out.append(m_new)
                ps.append(p.astype(BF16))
                alphas.append(alpha)
            pv = _dot(vtj[kvh * HEAD_DIM:(kvh + 1) * HEAD_DIM, :], jnp.concatenate(ps, axis=1))
            acc_ref[kvh] = jnp.concatenate(alphas, axis=1) * acc_ref[kvh] + pv
        return prefix[rows - 1:rows, :], tuple(m_out), tuple(l_out)

    def attend_slab(j, carry):
        start = pl.multiple_of(j * SLAB, SLAB)
        return attend(keys_ref[pl.ds(start, SLAB), :], k_ref[pl.ds(start, SLAB), :], vt_ref[j], carry)

    init = (jnp.zeros((1, tq), F32), (jnp.full((1, tq), NEG_BIG, F32),) * n_heads,
            (jnp.zeros((1, tq), F32),) * n_heads)
    def attend4(t, c):
        for u in range(4):
            c = attend_slab(4 * t + u, c)
        return c
    carry = lax.fori_loop(0, nwhole // 4, attend4, init)
    carry = lax.fori_loop(4 * (nwhole // 4), nwhole, attend_slab, carry)
    _, _, l_all = attend(keys_ref[pl.ds(last_start, last_rows), :], k_ref[pl.ds(last_start, last_rows), :],
                         vt_ref[nwhole][:, :last_rows], carry)

    outs = []
    for kvh in range(kv_heads):
        acc = acc_ref[kvh]
        for g in range(group):
            outs.append(acc[:, g * tq:(g + 1) * tq] / l_all[kvh * group + g])
    o_ref[...] = jnp.concatenate(outs, axis=0).T.astype(o_ref.dtype)


def _dsa_attention(q, iq, iw, k, vt, ik, *, tq, n_q, q_row0, q_pos0, last_rows, chunk_off, n_valid, n_sel, kv_heads):
    B, _, W = q.shape
    Lk = k.shape[1]
    n_heads = W // HEAD_DIM
    lt = jnp.tril(jnp.ones((SLAB, SLAB), F32)).astype(BF16)
    kern = functools.partial(_dsa_kernel, q_pos0=q_pos0, last_rows=last_rows, nslab_total=Lk // SLAB,
                             chunk_off=chunk_off, n_valid=n_valid, n_sel=n_sel, n_heads=n_heads, kv_heads=kv_heads)
    rows = lambda a: pl.BlockSpec((pl.Element(1), pl.Element(tq), pl.Element(a.shape[2])),
                                  lambda b, i: (b, pl.multiple_of(q_row0 + i * tq, ROW_ALIGN), 0))
    return pl.pallas_call(
        kern,
        grid=(B, n_q),
        in_specs=[rows(q), rows(iq), rows(iw),
                  pl.BlockSpec((None, Lk, k.shape[2]), lambda b, i: (b, 0, 0)),
                  pl.BlockSpec((None,) + vt.shape[1:], lambda b, i: (b, 0, 0, 0)),
                  pl.BlockSpec((None, Lk, ik.shape[2]), lambda b, i: (b, 0, 0)),
                  pl.BlockSpec(lt.shape, lambda b, i: (0, 0))],
        out_specs=pl.BlockSpec((None, tq, W), lambda b, i: (b, i, 0)),
        out_shape=jax.ShapeDtypeStruct((B, n_q * tq, W), BF16),
        scratch_shapes=[pltpu.VMEM((Lk, tq), jnp.int32),
                        pltpu.VMEM((kv_heads, HEAD_DIM, (n_heads // kv_heads) * tq), F32)],
        compiler_params=_cparams(("parallel", "arbitrary")), name="dsa_attn",
    )(q, iq, iw, k, vt, ik, lt)


def _merge_kernel(x_ref, osb_ref, odsa_ref, gsb_ref, gdsa_ref, wsb_ref, wdsa_ref, wout_ref, g2_ref,
                  wr_hi_ref, wr_lo_ref, h_ref, hn_ref, comb_ref, *, n_groups, per_group):
    merged = gsb_ref[0] * _dot(osb_ref[0], wsb_ref[...]) + gdsa_ref[0] * _dot(odsa_ref[0], wdsa_ref[...])
    h = x_ref[...] + _dot(merged.astype(BF16), wout_ref[...])
    h_ref[...] = h
    ms = jnp.mean(h * h, axis=-1, keepdims=True)
    hn = h * lax.rsqrt(ms + EPS) * g2_ref[...]
    hn_ref[...] = hn.astype(BF16)

    hi, lo = _split_bf16(hn)
    logits = _dot(hi, wr_hi_ref[...]) + _dot(lo, wr_hi_ref[...]) + _dot(hi, wr_lo_ref[...])
    lane = lax.broadcasted_iota(jnp.int32, logits.shape, 1)
    big = jnp.int32(LANES)
    neg_inf = -jnp.inf

    def first_argmax(vals):
        top = jnp.max(vals, axis=-1, keepdims=True)
        idx = jnp.min(jnp.where(vals == top, lane, big), axis=-1, keepdims=True)
        return top, idx

    is_group = lane < n_groups
    gl = jnp.where(is_group, logits, neg_inf)
    ge = jnp.exp(gl - jnp.max(gl, axis=-1, keepdims=True))
    probs = jnp.where(is_group, ge / jnp.sum(ge, axis=-1, keepdims=True), neg_inf)
    gp, gi = first_argmax(probs)

    expert = lane - n_groups
    in_group = jnp.logical_and(expert >= gi * per_group, expert < (gi + 1) * per_group)
    vals = jnp.where(in_group, logits, neg_inf)
    ev0, i0 = first_argmax(vals)
    ev1, i1 = first_argmax(jnp.where(lane == i0, neg_inf, vals))
    e1 = jnp.exp(ev1 - ev0)
    w0 = gp / (1.0 + e1)
    w1 = gp * e1 / (1.0 + e1)
    comb_ref[...] = jnp.where(lane == i0, w0, jnp.where(lane == i1, w1, 0.0))


def _merge(x, osb, odsa, gsb, gdsa, w, row0s):
    B, R, D = x.shape
    tm = next(t for t in (512, 256, 128, 64, 32, 16) if R % t == 0)
    nt = R // tm
    shifted = lambda a, row0: pl.BlockSpec((pl.Element(1), pl.Element(tm), pl.Element(a.shape[2])),
                                           lambda b, i: (b, pl.multiple_of(row0 + i * tm, ROW_ALIGN), 0))
    full = lambda a: pl.BlockSpec(a.shape, lambda b, i: (0,) * a.ndim)
    out = lambda width: pl.BlockSpec((tm, width), lambda b, i: (b * nt + i, 0))
    weights = [w["wsb"], w["wdsa"], w["wout"], w["g2"], w["wr_hi"], w["wr_lo"]]
    in_specs = ([pl.BlockSpec((None, tm, D), lambda b, i: (b, i, 0))] + [shifted(a, r) for a, r in zip((osb, odsa, gsb, gdsa), row0s)]
                + [full(a) for a in weights])
    return pl.pallas_call(
        functools.partial(_merge_kernel, n_groups=w["n_groups"], per_group=w["per_group"]),
        grid=(B, nt), in_specs=in_specs,
        out_specs=[out(D), out(D), out(LANES)],
        out_shape=[jax.ShapeDtypeStruct((B * R, D), F32), jax.ShapeDtypeStruct((B * R, D), BF16),
                   jax.ShapeDtypeStruct((B * R, LANES), F32)],
        compiler_params=_cparams(("parallel", "parallel")), name="merge_route",
    )(x, osb, odsa, gsb, gdsa, *weights)


def _moe_kernel(h_ref, hn_ref, comb_ref, wg_ref, wu_ref, wd_ref, y_ref, hh_ref, *, n_groups):
    step = pl.program_id(1)
    n_exp, _, fd = wg_ref.shape

    @pl.when(step == 0)
    def _():
        y_ref[...] = h_ref[...]

    hn = hn_ref[...]
    comb = comb_ref[...]
    lane = lax.broadcasted_iota(jnp.int32, comb.shape, 1)
    for e in range(n_exp):
        ce = jnp.sum(jnp.where(lane == n_groups + step * n_exp + e, comb, 0.0), axis=-1, keepdims=True)
        a = _dot(hn, wg_ref[e])
        b = _dot(hn, wu_ref[e])
        hh_ref[:, e * fd:(e + 1) * fd] = (a * jax.nn.sigmoid(a) * b * ce).astype(BF16)
    y_ref[...] += _dot(hh_ref[...], wd_ref[...])


def _moe(h, hn, comb, w):
    N, D = h.shape
    E, _, Fd = w["wgate"].shape
    tm = next(t for t in (1024, 512, 256, 128, 64, 32, 16) if N % t == 0)
    ne = MOE_EXPERTS_PER_STEP
    assert E % ne == 0
    return pl.pallas_call(
        functools.partial(_moe_kernel, n_groups=w["n_groups"]),
        grid=(N // tm, E // ne),
        in_specs=[pl.BlockSpec((tm, D), lambda i, e: (i, 0)),
                  pl.BlockSpec((tm, D), lambda i, e: (i, 0)),
                  pl.BlockSpec((tm, LANES), lambda i, e: (i, 0)),
                  pl.BlockSpec((ne, D, Fd), lambda i, e: (e, 0, 0)),
                  pl.BlockSpec((ne, D, Fd), lambda i, e: (e, 0, 0)),
                  pl.BlockSpec((ne * Fd, D), lambda i, e: (e, 0))],
        out_specs=pl.BlockSpec((tm, D), lambda i, e: (i, 0)),
        out_shape=jax.ShapeDtypeStruct((N, D), F32),
        scratch_shapes=[pltpu.VMEM((tm, ne * Fd), BF16)],
        compiler_params=_cparams(("parallel", "arbitrary")), name="moe_experts",
    )(h, hn, comb, w["wgate"], w["wup"], w["wdown"].reshape(E * Fd, D))


def _prep_weights(norm1_g, w_in, qg, kg, w_sbb, w_dsab, w_out, norm2_g, w_group, w_router, w_gate, w_up, w_down,
                  dims):
    sb_w, dsa_w, kv_w, idx_w = dims
    D = w_in.shape[0]
    o_dv = 3 * sb_w + dsa_w + kv_w
    o_ik = o_dv + kv_w + idx_w
    o_iw = o_ik + HEAD_DIM
    o_g = o_iw + IDX_HEADS
    w_ik = w_in[:, o_ik:o_iw]
    n_groups = w_group.shape[1]
    n_experts = w_router.shape[1]
    assert n_groups + n_experts <= LANES
    wr = jnp.concatenate([w_group, w_router, jnp.zeros((D, LANES - n_groups - n_experts), F32)], axis=1)
    wr_hi, wr_lo = _split_bf16(wr)
    tile2 = lambda g: jnp.tile(g.astype(F32), LANES // HEAD_DIM)[None, :]
    head_id = jnp.arange(LANES) // HEAD_DIM
    return dict(
        g1=norm1_g.astype(F32)[None, :],
        wa=jnp.concatenate([w_in[:, :o_iw], w_ik], axis=1).astype(BF16),
        wg=w_in[:, o_g:].astype(BF16),
        wiw=jnp.pad(w_in[:, o_iw:o_g], ((0, 0), (0, LANES - IDX_HEADS))).astype(BF16),
        wdvt=w_in[:, o_dv:o_dv + kv_w].T.astype(BF16),
        qg=tile2(qg), kg=tile2(kg),
        gmat=((head_id[:, None] == head_id[None, :]).astype(F32) / HEAD_DIM).astype(BF16),
        wsb=w_sbb.astype(BF16), wdsa=w_dsab.astype(BF16), wout=w_out.astype(BF16),
        g2=norm2_g.astype(F32)[None, :], wr_hi=wr_hi, wr_lo=wr_lo,
        wgate=w_gate.astype(BF16), wup=w_up.astype(BF16), wdown=w_down.astype(BF16),
        n_groups=n_groups, per_group=n_experts // n_groups)


def _tail(x, osb, odsa, gsb, gdsa, w, row0s):
    h, hn, comb = _merge(x, osb, odsa, gsb, gdsa, w, row0s)
    return _moe(h, hn, comb, w).reshape(x.shape)


def kernel(x_prompt, x_sample, cache_sb_k, cache_sb_v, cache_dsa_k, cache_dsa_v, cache_idx_k, meta_tokens, norm1_g, w_in, dsa_q_norm_g, dsa_k_norm_g, w_sb_branch, w_dsa_branch, w_out, norm2_g, w_group, w_router, w_gate, w_up, w_down):
    B, S, D = x_prompt.shape
    Bs, T, _ = x_sample.shape
    depth, _, past, sb_heads, hd = cache_sb_k.shape
    kv_heads = cache_dsa_k.shape[3]
    n_meta = meta_tokens.shape[0]
    assert depth == 1 and hd == HEAD_DIM and cache_idx_k.shape[-1] == HEAD_DIM
    assert past % SLAB == 0 and T <= BLK and S % CHUNK == 0
    sb_w = sb_heads * HEAD_DIM
    dsa_w = w_dsa_branch.shape[1]
    kv_w = kv_heads * HEAD_DIM
    idx_w = IDX_HEADS * HEAD_DIM
    assert kv_w == LANES and sb_w % LANES == 0 and dsa_w % LANES == 0
    dims = (sb_w, dsa_w, kv_w, idx_w)
    w = _prep_weights(norm1_g[0], w_in[0], dsa_q_norm_g[0], dsa_k_norm_g[0], w_sb_branch[0], w_dsa_branch[0],
                      w_out[0], norm2_g[0], w_group[0], w_router[0], w_gate[0], w_up[0], w_down[0], dims)

    Lr = n_meta + S
    Lp = -(-Lr // SLAB) * SLAB
    meta = jnp.broadcast_to(meta_tokens[None].astype(x_prompt.dtype), (B, n_meta, D))
    xp = jnp.concatenate([meta, x_prompt, jnp.zeros((B, Lp - Lr, D), x_prompt.dtype)], axis=1)
    pr = _project(xp, jnp.arange(Lp, dtype=jnp.int32), w, dims, Lr)
    o_sb = _sb_attention(pr["sqb"], pr["skb"], pr["svb"], 0)
    assert S % SLAB == 0 and n_meta % ROW_ALIGN == 0 and n_meta < CHUNK
    o_dsa = _dsa_attention(pr["dqb"], pr["iqb"], pr["iw"], pr["dkb"], pr["dvt"], pr["ikb"], tq=SLAB, n_q=S // SLAB,
                           q_row0=n_meta, q_pos0=n_meta, last_rows=n_meta, chunk_off=n_meta, n_valid=Lr,
                           n_sel=min(TOPK_MAX, S // 4), kv_heads=kv_heads)
    y_prompt = _tail(x_prompt, o_sb, o_dsa, pr["gsb"], pr["gdsa"], w, (n_meta, 0, n_meta, n_meta))

    xs = jnp.concatenate([x_sample, jnp.zeros((Bs, SLAB - T, D), x_sample.dtype)], axis=1)
    sr = _project(xs, past + jnp.arange(SLAB, dtype=jnp.int32), w, dims, T)
    cat = lambda c, new: jnp.concatenate([c.astype(BF16), new], axis=1)
    k_sb = cat(cache_sb_k[0].reshape(Bs, past, sb_w), sr["skb"])
    v_sb = cat(cache_sb_v[0].reshape(Bs, past, sb_w), sr["svb"])
    k_ds = cat(cache_dsa_k[0].reshape(Bs, past, kv_w), sr["dkb"])
    k_ix = cat(jnp.tile(cache_idx_k[0], (1, 1, LANES // HEAD_DIM)), sr["ikb"])
    vt_c = cache_dsa_v[0].reshape(Bs, past // SLAB, SLAB, kv_w).transpose(0, 1, 3, 2).astype(BF16)
    vt_ds = jnp.concatenate([vt_c, sr["dvt"]], axis=1)
    qb = past // SLAB
    o_sb_s = _sb_attention(sr["sqb"], k_sb, v_sb, qb)
    o_dsa_s = _dsa_attention(sr["dqb"], sr["iqb"], sr["iw"], k_ds, vt_ds, k_ix, tq=BLK, n_q=1, q_row0=0, q_pos0=past,
                             last_rows=-(-T // ROW_ALIGN) * ROW_ALIGN, chunk_off=0, n_valid=past + T, n_sel=min(TOPK_MAX, (past + T) // 4),
                             kv_heads=kv_heads)
    y_sample = _tail(x_sample, o_sb_s, o_dsa_s, sr["gsb"], sr["gdsa"], w, (0, 0, 0, 0))

    heads = lambda a, n: a.reshape(1, a.shape[0], a.shape[1], n, HEAD_DIM)
    return (y_prompt, y_sample,
            heads(pr["skf"], sb_heads), heads(pr["svf"], sb_heads),
            heads(pr["dkf"], kv_heads), heads(pr["dvf"], kv_heads), pr["ikf"][None],
            heads(sr["skf"], sb_heads), heads(sr["svf"], sb_heads),
            heads(sr["dkf"], kv_heads), heads(sr["dvf"], kv_heads), sr["ikf"][None])
```

```python
import functools

import jax
import jax.numpy as jnp
from jax import lax
from jax.experimental import pallas as pl
from jax.experimental.pallas import tpu as pltpu

F32 = jnp.float32
BF16 = jnp.bfloat16

CHUNK = 64
TOPK_MAX = 256
ROPE_THETA = 10000.0
EPS = 1e-6
HEAD_DIM = 64
IDX_HEADS = 8
EXPERT_TOPK = 2

LANES = 128
BLK = LANES
SLAB = 2 * BLK
SB_HEADS_PER_STEP = 8
MOE_EXPERTS_PER_STEP = 4
ROW_ALIGN = 16
VMEM_LIMIT = 56 * 1024 * 1024
INT_MIN = -(2 ** 31)
NEG_BIG = -1e30
LOG2E = 1.4426950408889634
SB_UNDERFLOW = -151.0


def _cparams(sem):
    return pltpu.CompilerParams(dimension_semantics=sem, vmem_limit_bytes=VMEM_LIMIT)


def _dot(a, b):
    return jnp.dot(a, b, preferred_element_type=F32)


def _dot_nt(a, b):
    return lax.dot_general(a, b, (((1,), (1,)), ((), ())), preferred_element_type=F32)


def _split_bf16(x):
    hi = x.astype(BF16)
    lo = (x - hi.astype(F32)).astype(BF16)
    return hi, lo


def _proj_kernel(x_ref, pre_ref, g1_ref, wa_ref, wg_ref, wiw_ref, wdvt_ref, cos_ref, sin_ref,
                 qg_ref, kg_ref, gmat_ref,
                 sqb_ref, skf_ref, svf_ref, skb_ref, svb_ref, dqb_ref, dkf_ref, dvf_ref,
                 dkb_ref, dvt_ref, iqb_ref, ikf_ref, ikb_ref, iw_ref, gsb_ref, gdsa_ref,
                 *, sb_w, dsa_w, kv_w, idx_w, window):
    if window is None:
        x = x_ref[...]
    else:
        n_pre, shift, n_tiles = window
        xw = x_ref[0]
        tile = jnp.full((xw.shape[0], 1), pl.program_id(1), jnp.int32)
        first = jnp.concatenate([pre_ref[...], xw[:xw.shape[0] - n_pre]], axis=0)
        last = jnp.concatenate([xw[shift:], jnp.zeros((shift, xw.shape[1]), xw.dtype)], axis=0)
        x = jnp.where(tile == 0, first, jnp.where(tile == n_tiles - 1, last, xw))
    ms = jnp.mean(x * x, axis=-1, keepdims=True)
    xn = x * lax.rsqrt(ms + EPS) * g1_ref[...]
    xb = xn.astype(BF16)
    cos = cos_ref[...]
    sin = sin_ref[...]
    gmat = gmat_ref[...]
    lane = lax.broadcasted_iota(jnp.int32, cos.shape, 1)
    first_half = (lane % HEAD_DIM) < (HEAD_DIM // 2)

    def mm(lo, width):
        return _dot(xb, wa_ref[:, lo:lo + width])

    def head_norm(y, gain):
        hi, lo = _split_bf16(y * y)
        m = _dot(hi, gmat) + _dot(lo, gmat)
        return y * lax.rsqrt(m + EPS) * gain

    def rope(y):
        swapped = jnp.where(first_half, pltpu.roll(y, LANES - HEAD_DIM // 2, 1),
                            pltpu.roll(y, HEAD_DIM // 2, 1))
        return y * cos + swapped * sin

    scale = HEAD_DIM ** -0.5
    scale2 = scale * LOG2E
    off = 0
    sqb_ref[...] = (mm(off, sb_w) * scale2).astype(BF16)
    off += sb_w
    n_out = skf_ref.shape[0]
    sk = mm(off, sb_w)
    skf_ref[...] = sk[:n_out]
    skb_ref[...] = sk.astype(BF16)
    off += sb_w
    sv = mm(off, sb_w)
    svf_ref[...] = sv[:n_out]
    svb_ref[...] = sv.astype(BF16)
    off += sb_w
    for c in range(dsa_w // LANES):
        y = rope(head_norm(mm(off + c * LANES, LANES), qg_ref[...]))
        dqb_ref[:, c * LANES:(c + 1) * LANES] = (y * scale2).astype(BF16)
    off += dsa_w
    for c in range(kv_w // LANES):
        y = rope(head_norm(mm(off + c * LANES, LANES), kg_ref[...]))
        dkf_ref[:, c * LANES:(c + 1) * LANES] = y[:n_out]
        dkb_ref[:, c * LANES:(c + 1) * LANES] = y.astype(BF16)
    off += kv_w
    dvf_ref[...] = mm(off, kv_w)[:n_out]
    off += kv_w
    for c in range(idx_w // LANES):
        y = rope(mm(off + c * LANES, LANES))
        iqb_ref[:, c * LANES:(c + 1) * LANES] = (y * scale).astype(BF16)
    off += idx_w
    y = rope(mm(off, LANES))
    ikf_ref[...] = y[:n_out, :HEAD_DIM]
    ikb_ref[...] = y.astype(BF16)
    iw_ref[...] = _dot(xb, wiw_ref[...])
    dvt_ref[...] = _dot_nt(wdvt_ref[...], xb).astype(BF16)
    d = gsb_ref.shape[-1]
    gsb_ref[...] = jax.nn.sigmoid(_dot(xb, wg_ref[:, :d]))
    gdsa_ref[...] = jax.nn.sigmoid(_dot(xb, wg_ref[:, d:]))


def _rope_tables(pos):
    half = HEAD_DIM // 2
    freqs = ROPE_THETA ** (-jnp.arange(half, dtype=F32) / half)
    ang = pos.astype(F32)[:, None] * freqs[None, :]
    cos, sin = jnp.cos(ang), jnp.sin(ang)
    cos_t = jnp.tile(cos, (1, LANES // half))
    sin_t = jnp.tile(jnp.concatenate([-sin, sin], axis=1), (1, LANES // HEAD_DIM))
    return cos_t, sin_t


def _project(x, pos, w, dims, n_real, prefix=None):
    B, R, D = x.shape
    L = pos.shape[0]
    sb_w, dsa_w, kv_w, idx_w = dims
    tm = SLAB
    cos_t, sin_t = _rope_tables(pos)
    grid = (B, L // tm)
    row = lambda width: pl.BlockSpec((None, tm, width), lambda b, i: (b, i, 0))
    if prefix is None:
        assert R == L
        window, x_spec, prefix = None, row(D), jnp.zeros((ROW_ALIGN, D), x.dtype)
    else:
        n_pre = prefix.shape[0]
        assert 0 < n_pre <= tm <= R and n_pre % ROW_ALIGN == 0 and n_real == n_pre + R and L - tm < n_real <= L
        assert (L - tm) - n_pre + tm > R >= (L - 2 * tm) - n_pre + tm
        window = (n_pre, (L - tm - n_pre) - (R - tm), L // tm)
        x_spec = pl.BlockSpec((pl.Element(1), pl.Element(tm), pl.Element(D)),
                              lambda b, i: (b, pl.multiple_of(jnp.clip(i * tm - n_pre, 0, R - tm), ROW_ALIGN), 0))
    full = lambda a: pl.BlockSpec(a.shape, lambda b, i: (0,) * a.ndim)
    tab = pl.BlockSpec((tm, LANES), lambda b, i: (i, 0))
    out_shapes = dict(
        sqb=(BF16, sb_w), skf=(F32, sb_w), svf=(F32, sb_w), skb=(BF16, sb_w), svb=(BF16, sb_w),
        dqb=(BF16, dsa_w), dkf=(F32, kv_w), dvf=(F32, kv_w), dkb=(BF16, kv_w))
    names = ["sqb", "skf", "svf", "skb", "svb", "dqb", "dkf", "dvf", "dkb", "dvt", "iqb", "ikf",
             "ikb", "iw", "gsb", "gdsa"]
    shapes, specs = [], []
    f32_rows = min(tm, n_real)
    for n in names:
        if n in out_shapes:
            dt, width = out_shapes[n]
        elif n == "dvt":
            shapes.append(jax.ShapeDtypeStruct((B, L // SLAB, kv_w, SLAB), BF16))
            specs.append(pl.BlockSpec((None, None, kv_w, SLAB), lambda b, i: (b, i, 0, 0)))
            continue
        elif n == "iqb":
            dt, width = BF16, idx_w
        elif n == "ikf":
            dt, width = F32, HEAD_DIM
        elif n == "ikb":
            dt, width = BF16, LANES
        elif n == "iw":
            dt, width = F32, LANES
        else:
            dt, width = F32, D
        if n in ("skf", "svf", "dkf", "dvf", "ikf"):
            shapes.append(jax.ShapeDtypeStruct((B, n_real, width), dt))
            specs.append(pl.BlockSpec((None, f32_rows, width), lambda b, i: (b, i, 0)))
            continue
        shapes.append(jax.ShapeDtypeStruct((B, L, width), dt))
        specs.append(row(width))
    ins = [x, prefix, w["g1"], w["wa"], w["wg"], w["wiw"], w["wdvt"], cos_t, sin_t, w["qg"], w["kg"], w["gmat"]]
    in_specs = [x_spec, full(prefix), full(w["g1"]), full(w["wa"]), full(w["wg"]), full(w["wiw"]), full(w["wdvt"]),
                tab, tab, full(w["qg"]), full(w["kg"]), full(w["gmat"])]
    outs = pl.pallas_call(
        functools.partial(_proj_kernel, sb_w=sb_w, dsa_w=dsa_w, kv_w=kv_w, idx_w=idx_w, window=window),
        grid=grid, in_specs=in_specs, out_specs=specs, out_shape=shapes,
        compiler_params=_cparams(("parallel", "parallel")), name="proj",
    )(*ins)
    return dict(zip(names, outs))


def _sb_kernel(q_ref, k_ref, v_ref, uo_ref, o_ref, *, q0):
    qi = q0 + pl.program_id(2)
    q = q_ref[...]
    uo = uo_ref[...]
    n_heads = q.shape[1] // HEAD_DIM
    per_slab = LANES // HEAD_DIM
    causal = (lax.broadcasted_iota(jnp.int32, (SLAB, SLAB), 1) < lax.broadcasted_iota(jnp.int32, (SLAB, SLAB), 0))
    v_lane_head = lax.broadcasted_iota(jnp.int32, (SLAB, LANES), 1) // HEAD_DIM

    def fold(j, cs, accs, diagonal):
        start = pl.multiple_of(j * SLAB, SLAB)
        kslab = k_ref[pl.ds(start, SLAB), :]
        vslab = v_ref[pl.ds(start, SLAB), :]
        cs_out, accs = [], list(accs)
        for h in range(n_heads):
            hs = slice(h * HEAD_DIM, (h + 1) * HEAD_DIM)
            z = _dot_nt(q[:, hs], kslab[:, hs])
            ls_pos = jnp.minimum(z, 0.0) - jnp.log2(1.0 + jnp.exp2(-jnp.abs(z)))
            ls_neg = ls_pos - z
            if diagonal:
                ls_neg = jnp.where(causal, ls_neg, 0.0)
            hi, lo = _split_bf16(ls_neg)
            r = _dot(hi, uo) + _dot(lo, uo)
            wgt = jnp.exp2(ls_pos + r[:, :SLAB] + jnp.concatenate([cs[h]] * (SLAB // BLK), axis=1))
            if diagonal:
                wgt = jnp.where(causal, wgt, 0.0)
            vs = vslab[:, (h // per_slab) * LANES:(h // per_slab + 1) * LANES]
            vh = jnp.where(v_lane_head == h % per_slab, vs, jnp.zeros_like(vs))
            accs[h // per_slab] = accs[h // per_slab] + _dot(wgt.astype(BF16), vh)
            cs_out.append(cs[h] + r[:, SLAB:])
        cmax = functools.reduce(jnp.maximum, [jnp.max(c) for c in cs_out])
        return tuple(cs_out), tuple(accs), cmax

    zeros = jnp.zeros((SLAB, BLK), F32)
    cs, accs, cmax = fold(qi, (zeros,) * n_heads, (jnp.zeros((SLAB, LANES), F32),) * (n_heads // per_slab), True)

    def body(carry):
        j, cs, accs, _ = carry
        cs, accs, cmax = fold(j, cs, accs, False)
        return j - 1, cs, accs, cmax

    def cond(carry):
        j, _, _, cmax = carry
        return jnp.logical_and(j >= 0, cmax > SB_UNDERFLOW)

    _, _, accs, _ = lax.while_loop(cond, body, (qi - 1, cs, accs, cmax))
    o_ref[...] = jnp.concatenate(accs, axis=1).astype(o_ref.dtype)


def _sb_attention(q, k, v, q0):
    B, Lq, W = q.shape
    Lk = k.shape[1]
    uo = jnp.concatenate([jnp.tril(jnp.ones((SLAB, SLAB), F32), -1), jnp.ones((SLAB, BLK), F32)], axis=1).astype(BF16)
    wb = SB_HEADS_PER_STEP * HEAD_DIM
    grid = (B, W // wb, Lq // SLAB)
    return pl.pallas_call(
        functools.partial(_sb_kernel, q0=q0),
        grid=grid,
        in_specs=[pl.BlockSpec((None, SLAB, wb), lambda b, h, i: (b, i, h)),
                  pl.BlockSpec((None, Lk, wb), lambda b, h, i: (b, 0, h)),
                  pl.BlockSpec((None, Lk, wb), lambda b, h, i: (b, 0, h)),
                  pl.BlockSpec(uo.shape, lambda b, h, i: (0, 0))],
        out_specs=pl.BlockSpec((None, SLAB, wb), lambda b, h, i: (b, i, h)),
        out_shape=jax.ShapeDtypeStruct((B, Lq, W), BF16),
        compiler_params=_cparams(("parallel", "parallel", "arbitrary")), name="sb_attn",
    )(q, k, v, uo)


def _dsa_kernel(q_ref, iq_ref, iw_ref, k_ref, vt_ref, ik_ref, lt_ref, o_ref,
                keys_ref, acc_ref, *, q_pos0, last_rows, nslab_total, chunk_off, n_valid, n_sel, n_heads, kv_heads):
    tq = q_ref.shape[1]
    q_start = q_pos0 + pl.program_id(1) * tq
    nslab = jnp.minimum((q_start + tq) // SLAB + 1, nslab_total)
    group = n_heads // kv_heads
    idx_w_scale = IDX_HEADS ** -0.5

    qpos = q_start + lax.broadcasted_iota(jnp.int32, (1, tq), 1)
    chunk_end = chunk_off + CHUNK * (jnp.right_shift(qpos - chunk_off, CHUNK.bit_length() - 1) + 1)
    key_limit = jnp.minimum(jnp.where(qpos < chunk_off, chunk_off, chunk_end), n_valid)
    slab_row = lax.broadcasted_iota(jnp.int32, (SLAB, tq), 0)

    iwt = iw_ref[0].T
    by_head = lambda x, heads: jnp.concatenate([x[:, h * HEAD_DIM:(h + 1) * HEAD_DIM] for h in heads], axis=0)
    iq_rows = by_head(iq_ref[0], range(IDX_HEADS))

    def score_keys(ik_rows, pos0):
        rows = ik_rows.shape[0]
        s = _dot_nt(ik_rows[:, :HEAD_DIM], iq_rows)
        acc = jnp.zeros((rows, tq), F32)
        for h in range(IDX_HEADS):
            acc = acc + jnp.maximum(s[:, h * tq:(h + 1) * tq], 0.0) * iwt[h:h + 1, :]
        bits = lax.bitcast_convert_type(acc * idx_w_scale, jnp.int32)
        key = bits ^ ((bits >> 31) & 0x7FFFFFFF)
        row = lax.broadcasted_iota(jnp.int32, (rows, tq), 0)
        return jnp.where(row < key_limit - pos0, key, INT_MIN)

    def score_slab(j, _):
        start = pl.multiple_of(j * SLAB, SLAB)
        keys_ref[pl.ds(start, SLAB), :] = score_keys(ik_ref[pl.ds(start, SLAB), :], start)
        return 0

    nwhole = nslab - 1
    def score4(t, _):
        for u in range(4):
            score_slab(4 * t + u, 0)
        return 0
    lax.fori_loop(0, nwhole // 4, score4, 0)
    lax.fori_loop(4 * (nwhole // 4), nwhole, score_slab, 0)
    last_start = pl.multiple_of(nwhole * SLAB, SLAB)
    keys_ref[pl.ds(last_start, last_rows), :] = score_keys(ik_ref[pl.ds(last_start, last_rows), :], last_start)

    def select_threshold(lanes):
        def count_ge(cand):
            def body(j, cnt):
                start = pl.multiple_of(j * SLAB, SLAB)
                hit = jnp.where(keys_ref[pl.ds(start, SLAB), lanes] >= cand, 1.0, 0.0)
                return cnt + hit[:BLK] + hit[BLK:]
            def body4(t, c):
                for u in range(4):
                    c = body(4 * t + u, c)
                return c
            cnt = lax.fori_loop(0, nwhole // 4, body4, jnp.zeros((BLK, BLK), F32))
            cnt = lax.fori_loop(4 * (nwhole // 4), nwhole, body, cnt)
            last = pl.multiple_of((nslab - 1) * SLAB, SLAB)
            hit = jnp.where(keys_ref[pl.ds(last, last_rows), lanes] >= cand, 1.0, 0.0)
            return jnp.sum(cnt, axis=0, keepdims=True) + jnp.sum(hit, axis=0, keepdims=True)

        def search(it, thr):
            cand = thr + jnp.left_shift(jnp.int32(1), 31 - it)
            return jnp.where(count_ge(cand) >= n_sel, cand, thr)

        thr = lax.fori_loop(0, 32, search, jnp.full((1, BLK), INT_MIN, jnp.int32))
        return thr, n_sel - count_ge(thr + 1)

    halves = [select_threshold(slice(c * BLK, (c + 1) * BLK)) for c in range(tq // BLK)]
    thr = jnp.concatenate([t for t, _ in halves], axis=1)
    room = jnp.concatenate([r for _, r in halves], axis=1)

    acc_ref[...] = jnp.zeros(acc_ref.shape, F32)
    q = q_ref[0]
    q_rows = [by_head(q, range(kvh * group, (kvh + 1) * group)) for kvh in range(kv_heads)]
    lt = lt_ref[...]

    def attend(kk, kj, vtj, carry):
        eq_seen, m_all, l_all = carry
        rows = kk.shape[0]
        eq = kk == thr
        prefix = _dot(lt[:rows, :rows], jnp.where(eq, 1.0, 0.0).astype(BF16)) + eq_seen
        sel = jnp.logical_or(kk > thr, jnp.logical_and(eq, prefix <= room))
        sel = jnp.logical_and(sel, kk != INT_MIN)
        bias = jnp.where(sel, 0.0, NEG_BIG)
        m_out, l_out = [], []
        for kvh in range(kv_heads):
            s_all = _dot_nt(kj[:, kvh * HEAD_DIM:(kvh + 1) * HEAD_DIM], q_rows[kvh])
            ps, alphas = [], []
            for g in range(group):
                h = kvh * group + g
                s = s_all[:, g * tq:(g + 1) * tq] + bias
                m_new = jnp.maximum(m_all[h], jnp.max(s, axis=0, keepdims=True))
                p = jnp.exp2(s - m_new)
                alpha = jnp.exp2(m_all[h] - m_new)
                l_out.append(alpha * l_all[h] + jnp.sum(p, axis=0, keepdims=True))
                m_out.append(m_new)
                ps.append(p.astype(BF16))
                alphas.append(alpha)
            pv = _dot(vtj[kvh * HEAD_DIM:(kvh + 1) * HEAD_DIM, :], jnp.concatenate(ps, axis=1))
            acc_ref[kvh] = jnp.concatenate(alphas, axis=1) * acc_ref[kvh] + pv
        return prefix[rows - 1:rows, :], tuple(m_out), tuple(l_out)

    def attend_slab(j, carry):
        start = pl.multiple_of(j * SLAB, SLAB)
        return attend(keys_ref[pl.ds(start, SLAB), :], k_ref[pl.ds(start, SLAB), :], vt_ref[j], carry)

    init = (jnp.zeros((1, tq), F32), (jnp.full((1, tq), NEG_BIG, F32),) * n_heads,
            (jnp.zeros((1, tq), F32),) * n_heads)
    def attend4(t, c):
        for u in range(4):
            c = attend_slab(4 * t + u, c)
        return c
    carry = lax.fori_loop(0, nwhole // 4, attend4, init)
    carry = lax.fori_loop(4 * (nwhole // 4), nwhole, attend_slab, carry)
    _, _, l_all = attend(keys_ref[pl.ds(last_start, last_rows), :], k_ref[pl.ds(last_start, last_rows), :],
                         vt_ref[nwhole][:, :last_rows], carry)

    outs = []
    for kvh in range(kv_heads):
        acc = acc_ref[kvh]
        for g in range(group):
            outs.append(acc[:, g * tq:(g + 1) * tq] / l_all[kvh * group + g])
    o_ref[...] = jnp.concatenate(outs, axis=0).T.astype(o_ref.dtype)


def _dsa_attention(q, iq, iw, k, vt, ik, *, tq, n_q, q_row0, q_pos0, last_rows, chunk_off, n_valid, n_sel, kv_heads):
    B, _, W = q.shape
    Lk = k.shape[1]
    n_heads = W // HEAD_DIM
    lt = jnp.tril(jnp.ones((SLAB, SLAB), F32)).astype(BF16)
    kern = functools.partial(_dsa_kernel, q_pos0=q_pos0, last_rows=last_rows, nslab_total=Lk // SLAB,
                             chunk_off=chunk_off, n_valid=n_valid, n_sel=n_sel, n_heads=n_heads, kv_heads=kv_heads)
    rows = lambda a: pl.BlockSpec((pl.Element(1), pl.Element(tq), pl.Element(a.shape[2])),
                                  lambda b, i: (b, pl.multiple_of(q_row0 + i * tq, ROW_ALIGN), 0))
    return pl.pallas_call(
        kern,
        grid=(B, n_q),
        in_specs=[rows(q), rows(iq), rows(iw),
                  pl.BlockSpec((None, Lk, k.shape[2]), lambda b, i: (b, 0, 0)),
                  pl.BlockSpec((None,) + vt.shape[1:], lambda b, i: (b, 0, 0, 0)),
                  pl.BlockSpec((None, Lk, ik.shape[2]), lambda b, i: (b, 0, 0)),
                  pl.BlockSpec(lt.shape, lambda b, i: (0, 0))],
        out_specs=pl.BlockSpec((None, tq, W), lambda b, i: (b, i, 0)),
        out_shape=jax.ShapeDtypeStruct((B, n_q * tq, W), BF16),
        scratch_shapes=[pltpu.VMEM((Lk, tq), jnp.int32),
                        pltpu.VMEM((kv_heads, HEAD_DIM, (n_heads // kv_heads) * tq), F32)],
        compiler_params=_cparams(("parallel", "arbitrary")), name="dsa_attn",
    )(q, iq, iw, k, vt, ik, lt)


def _merge_kernel(x_ref, osb_ref, odsa_ref, gsb_ref, gdsa_ref, wsb_ref, wdsa_ref, wout_ref, g2_ref,
                  wr_hi_ref, wr_lo_ref, h_ref, hn_ref, comb_ref, *, n_groups, per_group):
    merged = gsb_ref[0] * _dot(osb_ref[0], wsb_ref[...]) + gdsa_ref[0] * _dot(odsa_ref[0], wdsa_ref[...])
    h = x_ref[...] + _dot(merged.astype(BF16), wout_ref[...])
    h_ref[...] = h
    ms = jnp.mean(h * h, axis=-1, keepdims=True)
    hn = h * lax.rsqrt(ms + EPS) * g2_ref[...]
    hn_ref[...] = hn.astype(BF16)

    hi, lo = _split_bf16(hn)
    logits = _dot(hi, wr_hi_ref[...]) + _dot(lo, wr_hi_ref[...]) + _dot(hi, wr_lo_ref[...])
    lane = lax.broadcasted_iota(jnp.int32, logits.shape, 1)
    big = jnp.int32(LANES)
    neg_inf = -jnp.inf

    def first_argmax(vals):
        top = jnp.max(vals, axis=-1, keepdims=True)
        idx = jnp.min(jnp.where(vals == top, lane, big), axis=-1, keepdims=True)
        return top, idx

    is_group = lane < n_groups
    gl = jnp.where(is_group, logits, neg_inf)
    ge = jnp.exp(gl - jnp.max(gl, axis=-1, keepdims=True))
    probs = jnp.where(is_group, ge / jnp.sum(ge, axis=-1, keepdims=True), neg_inf)
    gp, gi = first_argmax(probs)

    expert = lane - n_groups
    in_group = jnp.logical_and(expert >= gi * per_group, expert < (gi + 1) * per_group)
    vals = jnp.where(in_group, logits, neg_inf)
    ev0, i0 = first_argmax(vals)
    ev1, i1 = first_argmax(jnp.where(lane == i0, neg_inf, vals))
    e1 = jnp.exp(ev1 - ev0)
    w0 = gp / (1.0 + e1)
    w1 = gp * e1 / (1.0 + e1)
    comb_ref[...] = jnp.where(lane == i0, w0, jnp.where(lane == i1, w1, 0.0))


def _merge(x, osb, odsa, gsb, gdsa, w, row0s):
    B, R, D = x.shape
    tm = next(t for t in (512, 256, 128, 64, 32, 16) if R % t == 0)
    nt = R // tm
    shifted = lambda a, row0: pl.BlockSpec((pl.Element(1), pl.Element(tm), pl.Element(a.shape[2])),
                                           lambda b, i: (b, pl.multiple_of(row0 + i * tm, ROW_ALIGN), 0))
    full = lambda a: pl.BlockSpec(a.shape, lambda b, i: (0,) * a.ndim)
    out = lambda width: pl.BlockSpec((tm, width), lambda b, i: (b * nt + i, 0))
    weights = [w["wsb"], w["wdsa"], w["wout"], w["g2"], w["wr_hi"], w["wr_lo"]]
    in_specs = ([pl.BlockSpec((None, tm, D), lambda b, i: (b, i, 0))] + [shifted(a, r) for a, r in zip((osb, odsa, gsb, gdsa), row0s)]
                + [full(a) for a in weights])
    return pl.pallas_call(
        functools.partial(_merge_kernel, n_groups=w["n_groups"], per_group=w["per_group"]),
        grid=(B, nt), in_specs=in_specs,
        out_specs=[out(D), out(D), out(LANES)],
        out_shape=[jax.ShapeDtypeStruct((B * R, D), F32), jax.ShapeDtypeStruct((B * R, D), BF16),
                   jax.ShapeDtypeStruct((B * R, LANES), F32)],
        compiler_params=_cparams(("parallel", "parallel")), name="merge_route",
    )(x, osb, odsa, gsb, gdsa, *weights)


def _moe_kernel(h_ref, hn_ref, comb_ref, wg_ref, wu_ref, wd_ref, y_ref, hh_ref, *, n_groups):
    step = pl.program_id(1)
    n_exp, _, fd = wg_ref.shape

    @pl.when(step == 0)
    def _():
        y_ref[...] = h_ref[...]

    hn = hn_ref[...]
    comb = comb_ref[...]
    lane = lax.broadcasted_iota(jnp.int32, comb.shape, 1)
    for e in range(n_exp):
        ce = jnp.sum(jnp.where(lane == n_groups + step * n_exp + e, comb, 0.0), axis=-1, keepdims=True)
        a = _dot(hn, wg_ref[e])
        b = _dot(hn, wu_ref[e])
        hh_ref[:, e * fd:(e + 1) * fd] = (a * jax.nn.sigmoid(a) * b * ce).astype(BF16)
    y_ref[...] += _dot(hh_ref[...], wd_ref[...])


def _moe(h, hn, comb, w):
    N, D = h.shape
    E, _, Fd = w["wgate"].shape
    tm = next(t for t in (1024, 512, 256, 128, 64, 32, 16) if N % t == 0)
    ne = MOE_EXPERTS_PER_STEP
    assert E % ne == 0
    return pl.pallas_call(
        functools.partial(_moe_kernel, n_groups=w["n_groups"]),
        grid=(N // tm, E // ne),
        in_specs=[pl.BlockSpec((tm, D), lambda i, e: (i, 0)),
                  pl.BlockSpec((tm, D), lambda i, e: (i, 0)),
                  pl.BlockSpec((tm, LANES), lambda i, e: (i, 0)),
                  pl.BlockSpec((ne, D, Fd), lambda i, e: (e, 0, 0)),
                  pl.BlockSpec((ne, D, Fd), lambda i, e: (e, 0, 0)),
                  pl.BlockSpec((ne * Fd, D), lambda i, e: (e, 0))],
        out_specs=pl.BlockSpec((tm, D), lambda i, e: (i, 0)),
        out_shape=jax.ShapeDtypeStruct((N, D), F32),
        scratch_shapes=[pltpu.VMEM((tm, ne * Fd), BF16)],
        compiler_params=_cparams(("parallel", "arbitrary")), name="moe_experts",
    )(h, hn, comb, w["wgate"], w["wup"], w["wdown"].reshape(E * Fd, D))


def _prep_weights(norm1_g, w_in, qg, kg, w_sbb, w_dsab, w_out, norm2_g, w_group, w_router, w_gate, w_up, w_down,
                  dims):
    sb_w, dsa_w, kv_w, idx_w = dims
    D = w_in.shape[0]
    o_dv = 3 * sb_w + dsa_w + kv_w
    o_ik = o_dv + kv_w + idx_w
    o_iw = o_ik + HEAD_DIM
    o_g = o_iw + IDX_HEADS
    w_ik = w_in[:, o_ik:o_iw]
    n_groups = w_group.shape[1]
    n_experts = w_router.shape[1]
    assert n_groups + n_experts <= LANES
    wr = jnp.concatenate([w_group, w_router, jnp.zeros((D, LANES - n_groups - n_experts), F32)], axis=1)
    wr_hi, wr_lo = _split_bf16(wr)
    tile2 = lambda g: jnp.tile(g.astype(F32), LANES // HEAD_DIM)[None, :]
    head_id = jnp.arange(LANES) // HEAD_DIM
    return dict(
        g1=norm1_g.astype(F32)[None, :],
        wa=jnp.concatenate([w_in[:, :o_iw], w_ik], axis=1).astype(BF16),
        wg=w_in[:, o_g:].astype(BF16),
        wiw=jnp.pad(w_in[:, o_iw:o_g], ((0, 0), (0, LANES - IDX_HEADS))).astype(BF16),
        wdvt=w_in[:, o_dv:o_dv + kv_w].T.astype(BF16),
        qg=tile2(qg), kg=tile2(kg),
        gmat=((head_id[:, None] == head_id[None, :]).astype(F32) / HEAD_DIM).astype(BF16),
        wsb=w_sbb.astype(BF16), wdsa=w_dsab.astype(BF16), wout=w_out.astype(BF16),
        g2=norm2_g.astype(F32)[None, :], wr_hi=wr_hi, wr_lo=wr_lo,
        wgate=w_gate.astype(BF16), wup=w_up.astype(BF16), wdown=w_down.astype(BF16),
        n_groups=n_groups, per_group=n_experts // n_groups)


def _tail(x, osb, odsa, gsb, gdsa, w, row0s):
    h, hn, comb = _merge(x, osb, odsa, gsb, gdsa, w, row0s)
    return _moe(h, hn, comb, w).reshape(x.shape)


def kernel(x_prompt, x_sample, cache_sb_k, cache_sb_v, cache_dsa_k, cache_dsa_v, cache_idx_k, meta_tokens, norm1_g, w_in, dsa_q_norm_g, dsa_k_norm_g, w_sb_branch, w_dsa_branch, w_out, norm2_g, w_group, w_router, w_gate, w_up, w_down):
    B, S, D = x_prompt.shape
    Bs, T, _ = x_sample.shape
    depth, _, past, sb_heads, hd = cache_sb_k.shape
    kv_heads = cache_dsa_k.shape[3]
    n_meta = meta_tokens.shape[0]
    assert depth == 1 and hd == HEAD_DIM and cache_idx_k.shape[-1] == HEAD_DIM
    assert past % SLAB == 0 and T <= BLK and S % CHUNK == 0
    sb_w = sb_heads * HEAD_DIM
    dsa_w = w_dsa_branch.shape[1]
    kv_w = kv_heads * HEAD_DIM
    idx_w = IDX_HEADS * HEAD_DIM
    assert kv_w == LANES and sb_w % LANES == 0 and dsa_w % LANES == 0
    dims = (sb_w, dsa_w, kv_w, idx_w)
    w = _prep_weights(norm1_g[0], w_in[0], dsa_q_norm_g[0], dsa_k_norm_g[0], w_sb_branch[0], w_dsa_branch[0],
                      w_out[0], norm2_g[0], w_group[0], w_router[0], w_gate[0], w_up[0], w_down[0], dims)

    Lr = n_meta + S
    Lp = -(-Lr // SLAB) * SLAB
    pr = _project(x_prompt, jnp.arange(Lp, dtype=jnp.int32), w, dims, Lr, prefix=meta_tokens.astype(x_prompt.dtype))
    o_sb = _sb_attention(pr["sqb"], pr["skb"], pr["svb"], 0)
    assert S % SLAB == 0 and n_meta % ROW_ALIGN == 0 and n_meta < CHUNK
    o_dsa = _dsa_attention(pr["dqb"], pr["iqb"], pr["iw"], pr["dkb"], pr["dvt"], pr["ikb"], tq=SLAB, n_q=S // SLAB,
                           q_row0=n_meta, q_pos0=n_meta, last_rows=n_meta, chunk_off=n_meta, n_valid=Lr,
                           n_sel=min(TOPK_MAX, S // 4), kv_heads=kv_heads)
    y_prompt = _tail(x_prompt, o_sb, o_dsa, pr["gsb"], pr["gdsa"], w, (n_meta, 0, n_meta, n_meta))

    xs = jnp.concatenate([x_sample, jnp.zeros((Bs, SLAB - T, D), x_sample.dtype)], axis=1)
    sr = _project(xs, past + jnp.arange(SLAB, dtype=jnp.int32), w, dims, T)
    cat = lambda c, new: jnp.concatenate([c.astype(BF16), new], axis=1)
    k_sb = cat(cache_sb_k[0].reshape(Bs, past, sb_w), sr["skb"])
    v_sb = cat(cache_sb_v[0].reshape(Bs, past, sb_w), sr["svb"])
    k_ds = cat(cache_dsa_k[0].reshape(Bs, past, kv_w), sr["dkb"])
    k_ix = cat(jnp.tile(cache_idx_k[0], (1, 1, LANES // HEAD_DIM)), sr["ikb"])
    vt_c = cache_dsa_v[0].reshape(Bs, past // SLAB, SLAB, kv_w).transpose(0, 1, 3, 2).astype(BF16)
    vt_ds = jnp.concatenate([vt_c, sr["dvt"]], axis=1)
    qb = past // SLAB
    o_sb_s = _sb_attention(sr["sqb"], k_sb, v_sb, qb)
    o_dsa_s = _dsa_attention(sr["dqb"], sr["iqb"], sr["iw"], k_ds, vt_ds, k_ix, tq=BLK, n_q=1, q_row0=0, q_pos0=past,
                             last_rows=-(-T // ROW_ALIGN) * ROW_ALIGN, chunk_off=0, n_valid=past + T, n_sel=min(TOPK_MAX, (past + T) // 4),
                             kv_heads=kv_heads)
    y_sample = _tail(x_sample, o_sb_s, o_dsa_s, sr["gsb"], sr["gdsa"], w, (0, 0, 0, 0))

    heads = lambda a, n: a.reshape(1, a.shape[0], a.shape[1], n, HEAD_DIM)
    return (y_prompt, y_sample,
            heads(pr["skf"], sb_heads), heads(pr["svf"], sb_heads),
            heads(pr["dkf"], kv_heads), heads(pr["dvf"], kv_heads), pr["ikf"][None],
            heads(sr["skf"], sb_heads), heads(sr["svf"], sb_heads),
            heads(sr["dkf"], kv_heads), heads(sr["dvf"], kv_heads), sr["ikf"][None])
```

```python
import functools

import jax
import jax.numpy as jnp
from jax import lax
from jax.experimental import pallas as pl
from jax.experimental.pallas import tpu as pltpu

F32 = jnp.float32
BF16 = jnp.bfloat16

CHUNK = 64
TOPK_MAX = 256
ROPE_THETA = 10000.0
EPS = 1e-6
HEAD_DIM = 64
IDX_HEADS = 8
EXPERT_TOPK = 2

LANES = 128
BLK = LANES
SLAB = 2 * BLK
SB_HEADS_PER_STEP = 8
MOE_EXPERTS_PER_STEP = 8
ROW_ALIGN = 16
VMEM_LIMIT = 56 * 1024 * 1024
INT_MIN = -(2 ** 31)
NEG_BIG = -1e30
LOG2E = 1.4426950408889634
SB_UNDERFLOW = -151.0


def _cparams(sem):
    return pltpu.CompilerParams(dimension_semantics=sem, vmem_limit_bytes=VMEM_LIMIT)


def _dot(a, b):
    return jnp.dot(a, b, preferred_element_type=F32)


def _dot_nt(a, b):
    return lax.dot_general(a, b, (((1,), (1,)), ((), ())), preferred_element_type=F32)


def _split_bf16(x):
    hi = x.astype(BF16)
    lo = (x - hi.astype(F32)).astype(BF16)
    return hi, lo


def _proj_kernel(x_ref, pre_ref, g1_ref, wa_ref, wg_ref, wiw_ref, wdvt_ref, cos_ref, sin_ref,
                 qg_ref, kg_ref, gmat_ref,
                 sqb_ref, skf_ref, svf_ref, skb_ref, svb_ref, dqb_ref, dkf_ref, dvf_ref,
                 dkb_ref, dvt_ref, iqb_ref, ikf_ref, ikb_ref, iw_ref, gsb_ref, gdsa_ref,
                 *, sb_w, dsa_w, kv_w, idx_w, window):
    if window is None:
        x = x_ref[...]
    else:
        n_pre, shift, n_tiles = window
        xw = x_ref[0]
        tile = jnp.full((xw.shape[0], 1), pl.program_id(1), jnp.int32)
        first = jnp.concatenate([pre_ref[...], xw[:xw.shape[0] - n_pre]], axis=0)
        last = jnp.concatenate([xw[shift:], jnp.zeros((shift, xw.shape[1]), xw.dtype)], axis=0)
        x = jnp.where(tile == 0, first, jnp.where(tile == n_tiles - 1, last, xw))
    ms = jnp.mean(x * x, axis=-1, keepdims=True)
    xn = x * lax.rsqrt(ms + EPS) * g1_ref[...]
    xb = xn.astype(BF16)
    cos = cos_ref[...]
    sin = sin_ref[...]
    gmat = gmat_ref[...]
    lane = lax.broadcasted_iota(jnp.int32, cos.shape, 1)
    first_half = (lane % HEAD_DIM) < (HEAD_DIM // 2)

    def mm(lo, width):
        return _dot(xb, wa_ref[:, lo:lo + width])

    def head_norm(y, gain):
        hi, lo = _split_bf16(y * y)
        m = _dot(hi, gmat) + _dot(lo, gmat)
        return y * lax.rsqrt(m + EPS) * gain

    def rope(y):
        swapped = jnp.where(first_half, pltpu.roll(y, LANES - HEAD_DIM // 2, 1),
                            pltpu.roll(y, HEAD_DIM // 2, 1))
        return y * cos + swapped * sin

    scale = HEAD_DIM ** -0.5
    scale2 = scale * LOG2E
    off = 0
    sqb_ref[...] = (mm(off, sb_w) * scale2).astype(BF16)
    off += sb_w
    n_out = skf_ref.shape[0]
    sk = mm(off, sb_w)
    skf_ref[...] = sk[:n_out]
    skb_ref[...] = sk.astype(BF16)
    off += sb_w
    sv = mm(off, sb_w)
    svf_ref[...] = sv[:n_out]
    svb_ref[...] = sv.astype(BF16)
    off += sb_w
    for c in range(dsa_w // LANES):
        y = rope(head_norm(mm(off + c * LANES, LANES), qg_ref[...]))
        dqb_ref[:, c * LANES:(c + 1) * LANES] = (y * scale2).astype(BF16)
    off += dsa_w
    for c in range(kv_w // LANES):
        y = rope(head_norm(mm(off + c * LANES, LANES), kg_ref[...]))
        dkf_ref[:, c * LANES:(c + 1) * LANES] = y[:n_out]
        dkb_ref[:, c * LANES:(c + 1) * LANES] = y.astype(BF16)
    off += kv_w
    dvf_ref[...] = mm(off, kv_w)[:n_out]
    off += kv_w
    for c in range(idx_w // LANES):
        y = rope(mm(off + c * LANES, LANES))
        iqb_ref[:, c * LANES:(c + 1) * LANES] = (y * scale).astype(BF16)
    off += idx_w
    y = rope(mm(off, LANES))
    ikf_ref[...] = y[:n_out, :HEAD_DIM]
    ikb_ref[...] = y.astype(BF16)
    iw_ref[...] = _dot(xb, wiw_ref[...])
    dvt_ref[...] = _dot_nt(wdvt_ref[...], xb).astype(BF16)
    d = gsb_ref.shape[-1]
    gsb_ref[...] = jax.nn.sigmoid(_dot(xb, wg_ref[:, :d]))
    gdsa_ref[...] = jax.nn.sigmoid(_dot(xb, wg_ref[:, d:]))


def _rope_tables(pos):
    half = HEAD_DIM // 2
    freqs = ROPE_THETA ** (-jnp.arange(half, dtype=F32) / half)
    ang = pos.astype(F32)[:, None] * freqs[None, :]
    cos, sin = jnp.cos(ang), jnp.sin(ang)
    cos_t = jnp.tile(cos, (1, LANES // half))
    sin_t = jnp.tile(jnp.concatenate([-sin, sin], axis=1), (1, LANES // HEAD_DIM))
    return cos_t, sin_t


def _project(x, pos, w, dims, n_real, prefix=None):
    B, R, D = x.shape
    L = pos.shape[0]
    sb_w, dsa_w, kv_w, idx_w = dims
    tm = SLAB
    cos_t, sin_t = _rope_tables(pos)
    grid = (B, L // tm)
    row = lambda width: pl.BlockSpec((None, tm, width), lambda b, i: (b, i, 0))
    if prefix is None:
        assert R == L
        window, x_spec, prefix = None, row(D), jnp.zeros((ROW_ALIGN, D), x.dtype)
    else:
        n_pre = prefix.shape[0]
        assert 0 < n_pre <= tm <= R and n_pre % ROW_ALIGN == 0 and n_real == n_pre + R and L - tm < n_real <= L
        assert (L - tm) - n_pre + tm > R >= (L - 2 * tm) - n_pre + tm
        window = (n_pre, (L - tm - n_pre) - (R - tm), L // tm)
        x_spec = pl.BlockSpec((pl.Element(1), pl.Element(tm), pl.Element(D)),
                              lambda b, i: (b, pl.multiple_of(jnp.clip(i * tm - n_pre, 0, R - tm), ROW_ALIGN), 0))
    full = lambda a: pl.BlockSpec(a.shape, lambda b, i: (0,) * a.ndim)
    tab = pl.BlockSpec((tm, LANES), lambda b, i: (i, 0))
    out_shapes = dict(
        sqb=(BF16, sb_w), skf=(F32, sb_w), svf=(F32, sb_w), skb=(BF16, sb_w), svb=(BF16, sb_w),
        dqb=(BF16, dsa_w), dkf=(F32, kv_w), dvf=(F32, kv_w), dkb=(BF16, kv_w))
    names = ["sqb", "skf", "svf", "skb", "svb", "dqb", "dkf", "dvf", "dkb", "dvt", "iqb", "ikf",
             "ikb", "iw", "gsb", "gdsa"]
    shapes, specs = [], []
    f32_rows = min(tm, n_real)
    for n in names:
        if n in out_shapes:
            dt, width = out_shapes[n]
        elif n == "dvt":
            shapes.append(jax.ShapeDtypeStruct((B, L // SLAB, kv_w, SLAB), BF16))
            specs.append(pl.BlockSpec((None, None, kv_w, SLAB), lambda b, i: (b, i, 0, 0)))
            continue
        elif n == "iqb":
            dt, width = BF16, idx_w
        elif n == "ikf":
            dt, width = F32, HEAD_DIM
        elif n == "ikb":
            dt, width = BF16, LANES
        elif n == "iw":
            dt, width = F32, LANES
        else:
            dt, width = F32, D
        if n in ("skf", "svf", "dkf", "dvf", "ikf"):
            shapes.append(jax.ShapeDtypeStruct((B, n_real, width), dt))
            specs.append(pl.BlockSpec((None, f32_rows, width), lambda b, i: (b, i, 0)))
            continue
        shapes.append(jax.ShapeDtypeStruct((B, L, width), dt))
        specs.append(row(width))
    ins = [x, prefix, w["g1"], w["wa"], w["wg"], w["wiw"], w["wdvt"], cos_t, sin_t, w["qg"], w["kg"], w["gmat"]]
    in_specs = [x_spec, full(prefix), full(w["g1"]), full(w["wa"]), full(w["wg"]), full(w["wiw"]), full(w["wdvt"]),
                tab, tab, full(w["qg"]), full(w["kg"]), full(w["gmat"])]
    outs = pl.pallas_call(
        functools.partial(_proj_kernel, sb_w=sb_w, dsa_w=dsa_w, kv_w=kv_w, idx_w=idx_w, window=window),
        grid=grid, in_specs=in_specs, out_specs=specs, out_shape=shapes,
        compiler_params=_cparams(("parallel", "parallel")), name="proj",
    )(*ins)
    return dict(zip(names, outs))


def _sb_kernel(q_ref, k_ref, v_ref, uo_ref, o_ref, *, q0):
    qi = q0 + pl.program_id(2)
    q = q_ref[...]
    uo = uo_ref[...]
    n_heads = q.shape[1] // HEAD_DIM
    per_slab = LANES // HEAD_DIM
    causal = (lax.broadcasted_iota(jnp.int32, (SLAB, SLAB), 1) < lax.broadcasted_iota(jnp.int32, (SLAB, SLAB), 0))
    v_lane_head = lax.broadcasted_iota(jnp.int32, (SLAB, LANES), 1) // HEAD_DIM

    def fold(j, cs, accs, diagonal):
        start = pl.multiple_of(j * SLAB, SLAB)
        kslab = k_ref[pl.ds(start, SLAB), :]
        vslab = v_ref[pl.ds(start, SLAB), :]
        cs_out, accs = [], list(accs)
        for h in range(n_heads):
            hs = slice(h * HEAD_DIM, (h + 1) * HEAD_DIM)
            z = _dot_nt(q[:, hs], kslab[:, hs])
            ls_pos = jnp.minimum(z, 0.0) - jnp.log2(1.0 + jnp.exp2(-jnp.abs(z)))
            ls_neg = ls_pos - z
            if diagonal:
                ls_neg = jnp.where(causal, ls_neg, 0.0)
            hi, lo = _split_bf16(ls_neg)
            r = _dot(hi, uo) + _dot(lo, uo)
            wgt = jnp.exp2(ls_pos + r[:, :SLAB] + jnp.concatenate([cs[h]] * (SLAB // BLK), axis=1))
            if diagonal:
                wgt = jnp.where(causal, wgt, 0.0)
            vs = vslab[:, (h // per_slab) * LANES:(h // per_slab + 1) * LANES]
            vh = jnp.where(v_lane_head == h % per_slab, vs, jnp.zeros_like(vs))
            accs[h // per_slab] = accs[h // per_slab] + _dot(wgt.astype(BF16), vh)
            cs_out.append(cs[h] + r[:, SLAB:])
        cmax = functools.reduce(jnp.maximum, [jnp.max(c) for c in cs_out])
        return tuple(cs_out), tuple(accs), cmax

    zeros = jnp.zeros((SLAB, BLK), F32)
    cs, accs, cmax = fold(qi, (zeros,) * n_heads, (jnp.zeros((SLAB, LANES), F32),) * (n_heads // per_slab), True)

    def body(carry):
        j, cs, accs, _ = carry
        cs, accs, cmax = fold(j, cs, accs, False)
        return j - 1, cs, accs, cmax

    def cond(carry):
        j, _, _, cmax = carry
        return jnp.logical_and(j >= 0, cmax > SB_UNDERFLOW)

    _, _, accs, _ = lax.while_loop(cond, body, (qi - 1, cs, accs, cmax))
    o_ref[...] = jnp.concatenate(accs, axis=1).astype(o_ref.dtype)


def _sb_attention(q, k, v, q0):
    B, Lq, W = q.shape
    Lk = k.shape[1]
    uo = jnp.concatenate([jnp.tril(jnp.ones((SLAB, SLAB), F32), -1), jnp.ones((SLAB, BLK), F32)], axis=1).astype(BF16)
    wb = SB_HEADS_PER_STEP * HEAD_DIM
    grid = (B, W // wb, Lq // SLAB)
    return pl.pallas_call(
        functools.partial(_sb_kernel, q0=q0),
        grid=grid,
        in_specs=[pl.BlockSpec((None, SLAB, wb), lambda b, h, i: (b, i, h)),
                  pl.BlockSpec((None, Lk, wb), lambda b, h, i: (b, 0, h)),
                  pl.BlockSpec((None, Lk, wb), lambda b, h, i: (b, 0, h)),
                  pl.BlockSpec(uo.shape, lambda b, h, i: (0, 0))],
        out_specs=pl.BlockSpec((None, SLAB, wb), lambda b, h, i: (b, i, h)),
        out_shape=jax.ShapeDtypeStruct((B, Lq, W), BF16),
        compiler_params=_cparams(("parallel", "parallel", "arbitrary")), name="sb_attn",
    )(q, k, v, uo)


def _dsa_kernel(q_ref, iq_ref, iw_ref, k_ref, vt_ref, ik_ref, lt_ref, o_ref,
                keys_ref, acc_ref, *, q_pos0, last_rows, nslab_total, chunk_off, n_valid, n_sel, n_heads, kv_heads):
    tq = q_ref.shape[1]
    q_start = q_pos0 + pl.program_id(1) * tq
    nslab = jnp.minimum((q_start + tq) // SLAB + 1, nslab_total)
    group = n_heads // kv_heads
    idx_w_scale = IDX_HEADS ** -0.5

    qpos = q_start + lax.broadcasted_iota(jnp.int32, (1, tq), 1)
    chunk_end = chunk_off + CHUNK * (jnp.right_shift(qpos - chunk_off, CHUNK.bit_length() - 1) + 1)
    key_limit = jnp.minimum(jnp.where(qpos < chunk_off, chunk_off, chunk_end), n_valid)
    slab_row = lax.broadcasted_iota(jnp.int32, (SLAB, tq), 0)

    iwt = iw_ref[0].T
    by_head = lambda x, heads: jnp.concatenate([x[:, h * HEAD_DIM:(h + 1) * HEAD_DIM] for h in heads], axis=0)
    iq_rows = by_head(iq_ref[0], range(IDX_HEADS))

    def score_keys(ik_rows, pos0):
        rows = ik_rows.shape[0]
        s = _dot_nt(ik_rows[:, :HEAD_DIM], iq_rows)
        acc = jnp.zeros((rows, tq), F32)
        for h in range(IDX_HEADS):
            acc = acc + jnp.maximum(s[:, h * tq:(h + 1) * tq], 0.0) * iwt[h:h + 1, :]
        bits = lax.bitcast_convert_type(acc * idx_w_scale, jnp.int32)
        key = bits ^ ((bits >> 31) & 0x7FFFFFFF)
        row = lax.broadcasted_iota(jnp.int32, (rows, tq), 0)
        return jnp.where(row < key_limit - pos0, key, INT_MIN)

    def score_slab(j, _):
        start = pl.multiple_of(j * SLAB, SLAB)
        keys_ref[pl.ds(start, SLAB), :] = score_keys(ik_ref[pl.ds(start, SLAB), :], start)
        return 0

    nwhole = nslab - 1
    def score4(t, _):
        for u in range(4):
            score_slab(4 * t + u, 0)
        return 0
    lax.fori_loop(0, nwhole // 4, score4, 0)
    lax.fori_loop(4 * (nwhole // 4), nwhole, score_slab, 0)
    last_start = pl.multiple_of(nwhole * SLAB, SLAB)
    keys_ref[pl.ds(last_start, last_rows), :] = score_keys(ik_ref[pl.ds(last_start, last_rows), :], last_start)

    def select_threshold(lanes):
        def count_ge(cand):
            def body(j, cnt):
                start = pl.multiple_of(j * SLAB, SLAB)
                hit = jnp.where(keys_ref[pl.ds(start, SLAB), lanes] >= cand, 1.0, 0.0)
                return cnt + hit[:BLK] + hit[BLK:]
            def body4(t, c):
                for u in range(4):
                    c = body(4 * t + u, c)
                return c
            cnt = lax.fori_loop(0, nwhole // 4, body4, jnp.zeros((BLK, BLK), F32))
            cnt = lax.fori_loop(4 * (nwhole // 4), nwhole, body, cnt)
            last = pl.multiple_of((nslab - 1) * SLAB, SLAB)
            hit = jnp.where(keys_ref[pl.ds(last, last_rows), lanes] >= cand, 1.0, 0.0)
            return jnp.sum(cnt, axis=0, keepdims=True) + jnp.sum(hit, axis=0, keepdims=True)

        def search(it, thr):
            cand = thr + jnp.left_shift(jnp.int32(1), 31 - it)
            return jnp.where(count_ge(cand) >= n_sel, cand, thr)

        thr = lax.fori_loop(0, 32, search, jnp.full((1, BLK), INT_MIN, jnp.int32))
        return thr, n_sel - count_ge(thr + 1)

    halves = [select_threshold(slice(c * BLK, (c + 1) * BLK)) for c in range(tq // BLK)]
    thr = jnp.concatenate([t for t, _ in halves], axis=1)
    room = jnp.concatenate([r for _, r in halves], axis=1)

    acc_ref[...] = jnp.zeros(acc_ref.shape, F32)
    q = q_ref[0]
    q_rows = [by_head(q, range(kvh * group, (kvh + 1) * group)) for kvh in range(kv_heads)]
    lt = lt_ref[...]

    def attend(kk, kj, vtj, carry):
        eq_seen, m_all, l_all = carry
        rows = kk.shape[0]
        eq = kk == thr
        prefix = _dot(lt[:rows, :rows], jnp.where(eq, 1.0, 0.0).astype(BF16)) + eq_seen
        sel = jnp.logical_or(kk > thr, jnp.logical_and(eq, prefix <= room))
        sel = jnp.logical_and(sel, kk != INT_MIN)
        bias = jnp.where(sel, 0.0, NEG_BIG)
        m_out, l_out = [], []
        for kvh in range(kv_heads):
            s_all = _dot_nt(kj[:, kvh * HEAD_DIM:(kvh + 1) * HEAD_DIM], q_rows[kvh])
            ps, alphas = [], []
            for g in range(group):
                h = kvh * group + g
                s = s_all[:, g * tq:(g + 1) * tq] + bias
                m_new = jnp.maximum(m_all[h], jnp.max(s, axis=0, keepdims=True))
                p = jnp.exp2(s - m_new)
                alpha = jnp.exp2(m_all[h] - m_new)
                l_out.append(alpha * l_all[h] + jnp.sum(p, axis=0, keepdims=True))
                m_out.append(m_new)
                ps.append(p.astype(BF16))
                alphas.append(alpha)
            pv = _dot(vtj[kvh * HEAD_DIM:(kvh + 1) * HEAD_DIM, :], jnp.concatenate(ps, axis=1))
            acc_ref[kvh] = jnp.concatenate(alphas, axis=1) * acc_ref[kvh] + pv
        return prefix[rows - 1:rows, :], tuple(m_out), tuple(l_out)

    def attend_slab(j, carry):
        start = pl.multiple_of(j * SLAB, SLAB)
        return attend(keys_ref[pl.ds(start, SLAB), :], k_ref[pl.ds(start, SLAB), :], vt_ref[j], carry)

    init = (jnp.zeros((1, tq), F32), (jnp.full((1, tq), NEG_BIG, F32),) * n_heads,
            (jnp.zeros((1, tq), F32),) * n_heads)
    def attend4(t, c):
        for u in range(4):
            c = attend_slab(4 * t + u, c)
        return c
    carry = lax.fori_loop(0, nwhole // 4, attend4, init)
    carry = lax.fori_loop(4 * (nwhole // 4), nwhole, attend_slab, carry)
    _, _, l_all = attend(keys_ref[pl.ds(last_start, last_rows), :], k_ref[pl.ds(last_start, last_rows), :],
                         vt_ref[nwhole][:, :last_rows], carry)

    outs = []
    for kvh in range(kv_heads):
        acc = acc_ref[kvh]
        for g in range(group):
            outs.append(acc[:, g * tq:(g + 1) * tq] / l_all[kvh * group + g])
    o_ref[...] = jnp.concatenate(outs, axis=0).T.astype(o_ref.dtype)


def _dsa_attention(q, iq, iw, k, vt, ik, *, tq, n_q, q_row0, q_pos0, last_rows, chunk_off, n_valid, n_sel, kv_heads):
    B, _, W = q.shape
    Lk = k.shape[1]
    n_heads = W // HEAD_DIM
    lt = jnp.tril(jnp.ones((SLAB, SLAB), F32)).astype(BF16)
    kern = functools.partial(_dsa_kernel, q_pos0=q_pos0, last_rows=last_rows, nslab_total=Lk // SLAB,
                             chunk_off=chunk_off, n_valid=n_valid, n_sel=n_sel, n_heads=n_heads, kv_heads=kv_heads)
    rows = lambda a: pl.BlockSpec((pl.Element(1), pl.Element(tq), pl.Element(a.shape[2])),
                                  lambda b, i: (b, pl.multiple_of(q_row0 + i * tq, ROW_ALIGN), 0))
    return pl.pallas_call(
        kern,
        grid=(B, n_q),
        in_specs=[rows(q), rows(iq), rows(iw),
                  pl.BlockSpec((None, Lk, k.shape[2]), lambda b, i: (b, 0, 0)),
                  pl.BlockSpec((None,) + vt.shape[1:], lambda b, i: (b, 0, 0, 0)),
                  pl.BlockSpec((None, Lk, ik.shape[2]), lambda b, i: (b, 0, 0)),
                  pl.BlockSpec(lt.shape, lambda b, i: (0, 0))],
        out_specs=pl.BlockSpec((None, tq, W), lambda b, i: (b, i, 0)),
        out_shape=jax.ShapeDtypeStruct((B, n_q * tq, W), BF16),
        scratch_shapes=[pltpu.VMEM((Lk, tq), jnp.int32),
                        pltpu.VMEM((kv_heads, HEAD_DIM, (n_heads // kv_heads) * tq), F32)],
        compiler_params=_cparams(("parallel", "arbitrary")), name="dsa_attn",
    )(q, iq, iw, k, vt, ik, lt)


def _merge_kernel(x_ref, osb_ref, odsa_ref, gsb_ref, gdsa_ref, wsb_ref, wdsa_ref, wout_ref, g2_ref,
                  wr_hi_ref, wr_lo_ref, h_ref, hn_ref, comb_ref, *, n_groups, per_group):
    merged = gsb_ref[0] * _dot(osb_ref[0], wsb_ref[...]) + gdsa_ref[0] * _dot(odsa_ref[0], wdsa_ref[...])
    h = x_ref[...] + _dot(merged.astype(BF16), wout_ref[...])
    h_ref[...] = h
    ms = jnp.mean(h * h, axis=-1, keepdims=True)
    hn = h * lax.rsqrt(ms + EPS) * g2_ref[...]
    hn_ref[...] = hn.astype(BF16)

    hi, lo = _split_bf16(hn)
    logits = _dot(hi, wr_hi_ref[...]) + _dot(lo, wr_hi_ref[...]) + _dot(hi, wr_lo_ref[...])
    lane = lax.broadcasted_iota(jnp.int32, logits.shape, 1)
    big = jnp.int32(LANES)
    neg_inf = -jnp.inf

    def first_argmax(vals):
        top = jnp.max(vals, axis=-1, keepdims=True)
        idx = jnp.min(jnp.where(vals == top, lane, big), axis=-1, keepdims=True)
        return top, idx

    is_group = lane < n_groups
    gl = jnp.where(is_group, logits, neg_inf)
    ge = jnp.exp(gl - jnp.max(gl, axis=-1, keepdims=True))
    probs = jnp.where(is_group, ge / jnp.sum(ge, axis=-1, keepdims=True), neg_inf)
    gp, gi = first_argmax(probs)

    expert = lane - n_groups
    in_group = jnp.logical_and(expert >= gi * per_group, expert < (gi + 1) * per_group)
    vals = jnp.where(in_group, logits, neg_inf)
    ev0, i0 = first_argmax(vals)
    ev1, i1 = first_argmax(jnp.where(lane == i0, neg_inf, vals))
    e1 = jnp.exp(ev1 - ev0)
    w0 = gp / (1.0 + e1)
    w1 = gp * e1 / (1.0 + e1)
    comb_ref[...] = jnp.where(lane == i0, w0, jnp.where(lane == i1, w1, 0.0))


def _merge(x, osb, odsa, gsb, gdsa, w, row0s):
    B, R, D = x.shape
    tm = next(t for t in (512, 256, 128, 64, 32, 16) if R % t == 0)
    nt = R // tm
    shifted = lambda a, row0: pl.BlockSpec((pl.Element(1), pl.Element(tm), pl.Element(a.shape[2])),
                                           lambda b, i: (b, pl.multiple_of(row0 + i * tm, ROW_ALIGN), 0))
    full = lambda a: pl.BlockSpec(a.shape, lambda b, i: (0,) * a.ndim)
    out = lambda width: pl.BlockSpec((tm, width), lambda b, i: (b * nt + i, 0))
    weights = [w["wsb"], w["wdsa"], w["wout"], w["g2"], w["wr_hi"], w["wr_lo"]]
    in_specs = ([pl.BlockSpec((None, tm, D), lambda b, i: (b, i, 0))] + [shifted(a, r) for a, r in zip((osb, odsa, gsb, gdsa), row0s)]
                + [full(a) for a in weights])
    return pl.pallas_call(
        functools.partial(_merge_kernel, n_groups=w["n_groups"], per_group=w["per_group"]),
        grid=(B, nt), in_specs=in_specs,
        out_specs=[out(D), out(D), out(LANES)],
        out_shape=[jax.ShapeDtypeStruct((B * R, D), F32), jax.ShapeDtypeStruct((B * R, D), BF16),
                   jax.ShapeDtypeStruct((B * R, LANES), F32)],
        compiler_params=_cparams(("parallel", "parallel")), name="merge_route",
    )(x, osb, odsa, gsb, gdsa, *weights)


def _moe_kernel(h_ref, hn_ref, comb_ref, wg_ref, wu_ref, wd_ref, y_ref, hh_ref, *, n_groups):
    step = pl.program_id(1)
    n_exp, _, fd = wg_ref.shape

    @pl.when(step == 0)
    def _():
        y_ref[...] = h_ref[...]

    hn = hn_ref[...]
    comb = comb_ref[...]
    lane = lax.broadcasted_iota(jnp.int32, comb.shape, 1)
    for e in range(n_exp):
        ce = jnp.sum(jnp.where(lane == n_groups + step * n_exp + e, comb, 0.0), axis=-1, keepdims=True)
        a = _dot(hn, wg_ref[e])
        b = _dot(hn, wu_ref[e])
        hh_ref[:, e * fd:(e + 1) * fd] = (a * jax.nn.sigmoid(a) * b * ce).astype(BF16)
    y_ref[...] += _dot(hh_ref[...], wd_ref[...])


def _moe(h, hn, comb, w):
    N, D = h.shape
    E, _, Fd = w["wgate"].shape
    tm = next(t for t in (1024, 512, 256, 128, 64, 32, 16) if N % t == 0)
    ne = MOE_EXPERTS_PER_STEP
    assert E % ne == 0
    return pl.pallas_call(
        functools.partial(_moe_kernel, n_groups=w["n_groups"]),
        grid=(N // tm, E // ne),
        in_specs=[pl.BlockSpec((tm, D), lambda i, e: (i, 0)),
                  pl.BlockSpec((tm, D), lambda i, e: (i, 0)),
                  pl.BlockSpec((tm, LANES), lambda i, e: (i, 0)),
                  pl.BlockSpec((ne, D, Fd), lambda i, e: (e, 0, 0)),
                  pl.BlockSpec((ne, D, Fd), lambda i, e: (e, 0, 0)),
                  pl.BlockSpec((ne * Fd, D), lambda i, e: (e, 0))],
        out_specs=pl.BlockSpec((tm, D), lambda i, e: (i, 0)),
        out_shape=jax.ShapeDtypeStruct((N, D), F32),
        scratch_shapes=[pltpu.VMEM((tm, ne * Fd), BF16)],
        compiler_params=_cparams(("parallel", "arbitrary")), name="moe_experts",
    )(h, hn, comb, w["wgate"], w["wup"], w["wdown"].reshape(E * Fd, D))


def _prep_weights(norm1_g, w_in, qg, kg, w_sbb, w_dsab, w_out, norm2_g, w_group, w_router, w_gate, w_up, w_down,
                  dims):
    sb_w, dsa_w, kv_w, idx_w = dims
    D = w_in.shape[0]
    o_dv = 3 * sb_w + dsa_w + kv_w
    o_ik = o_dv + kv_w + idx_w
    o_iw = o_ik + HEAD_DIM
    o_g = o_iw + IDX_HEADS
    w_ik = w_in[:, o_ik:o_iw]
    n_groups = w_group.shape[1]
    n_experts = w_router.shape[1]
    assert n_groups + n_experts <= LANES
    wr = jnp.concatenate([w_group, w_router, jnp.zeros((D, LANES - n_groups - n_experts), F32)], axis=1)
    wr_hi, wr_lo = _split_bf16(wr)
    tile2 = lambda g: jnp.tile(g.astype(F32), LANES // HEAD_DIM)[None, :]
    head_id = jnp.arange(LANES) // HEAD_DIM
    return dict(
        g1=norm1_g.astype(F32)[None, :],
        wa=jnp.concatenate([w_in[:, :o_iw], w_ik], axis=1).astype(BF16),
        wg=w_in[:, o_g:].astype(BF16),
        wiw=jnp.pad(w_in[:, o_iw:o_g], ((0, 0), (0, LANES - IDX_HEADS))).astype(BF16),
        wdvt=w_in[:, o_dv:o_dv + kv_w].T.astype(BF16),
        qg=tile2(qg), kg=tile2(kg),
        gmat=((head_id[:, None] == head_id[None, :]).astype(F32) / HEAD_DIM).astype(BF16),
        wsb=w_sbb.astype(BF16), wdsa=w_dsab.astype(BF16), wout=w_out.astype(BF16),
        g2=norm2_g.astype(F32)[None, :], wr_hi=wr_hi, wr_lo=wr_lo,
        wgate=w_gate.astype(BF16), wup=w_up.astype(BF16), wdown=w_down.astype(BF16),
        n_groups=n_groups, per_group=n_experts // n_groups)


def _tail(x, osb, odsa, gsb, gdsa, w, row0s):
    h, hn, comb = _merge(x, osb, odsa, gsb, gdsa, w, row0s)
    return _moe(h, hn, comb, w).reshape(x.shape)


def kernel(x_prompt, x_sample, cache_sb_k, cache_sb_v, cache_dsa_k, cache_dsa_v, cache_idx_k, meta_tokens, norm1_g, w_in, dsa_q_norm_g, dsa_k_norm_g, w_sb_branch, w_dsa_branch, w_out, norm2_g, w_group, w_router, w_gate, w_up, w_down):
    B, S, D = x_prompt.shape
    Bs, T, _ = x_sample.shape
    depth, _, past, sb_heads, hd = cache_sb_k.shape
    kv_heads = cache_dsa_k.shape[3]
    n_meta = meta_tokens.shape[0]
    assert depth == 1 and hd == HEAD_DIM and cache_idx_k.shape[-1] == HEAD_DIM
    assert past % SLAB == 0 and T <= BLK and S % CHUNK == 0
    sb_w = sb_heads * HEAD_DIM
    dsa_w = w_dsa_branch.shape[1]
    kv_w = kv_heads * HEAD_DIM
    idx_w = IDX_HEADS * HEAD_DIM
    assert kv_w == LANES and sb_w % LANES == 0 and dsa_w % LANES == 0
    dims = (sb_w, dsa_w, kv_w, idx_w)
    w = _prep_weights(norm1_g[0], w_in[0], dsa_q_norm_g[0], dsa_k_norm_g[0], w_sb_branch[0], w_dsa_branch[0],
                      w_out[0], norm2_g[0], w_group[0], w_router[0], w_gate[0], w_up[0], w_down[0], dims)

    Lr = n_meta + S
    Lp = -(-Lr // SLAB) * SLAB
    pr = _project(x_prompt, jnp.arange(Lp, dtype=jnp.int32), w, dims, Lr, prefix=meta_tokens.astype(x_prompt.dtype))
    o_sb = _sb_attention(pr["sqb"], pr["skb"], pr["svb"], 0)
    assert S % SLAB == 0 and n_meta % ROW_ALIGN == 0 and n_meta < CHUNK
    o_dsa = _dsa_attention(pr["dqb"], pr["iqb"], pr["iw"], pr["dkb"], pr["dvt"], pr["ikb"], tq=SLAB, n_q=S // SLAB,
                           q_row0=n_meta, q_pos0=n_meta, last_rows=n_meta, chunk_off=n_meta, n_valid=Lr,
                           n_sel=min(TOPK_MAX, S // 4), kv_heads=kv_heads)
    y_prompt = _tail(x_prompt, o_sb, o_dsa, pr["gsb"], pr["gdsa"], w, (n_meta, 0, n_meta, n_meta))

    xs = jnp.concatenate([x_sample, jnp.zeros((Bs, SLAB - T, D), x_sample.dtype)], axis=1)
    sr = _project(xs, past + jnp.arange(SLAB, dtype=jnp.int32), w, dims, T)
    cat = lambda c, new: jnp.concatenate([c.astype(BF16), new], axis=1)
    k_sb = cat(cache_sb_k[0].reshape(Bs, past, sb_w), sr["skb"])
    v_sb = cat(cache_sb_v[0].reshape(Bs, past, sb_w), sr["svb"])
    k_ds = cat(cache_dsa_k[0].reshape(Bs, past, kv_w), sr["dkb"])
    k_ix = cat(jnp.tile(cache_idx_k[0], (1, 1, LANES // HEAD_DIM)), sr["ikb"])
    vt_c = cache_dsa_v[0].reshape(Bs, past // SLAB, SLAB, kv_w).transpose(0, 1, 3, 2).astype(BF16)
    vt_ds = jnp.concatenate([vt_c, sr["dvt"]], axis=1)
    qb = past // SLAB
    o_sb_s = _sb_attention(sr["sqb"], k_sb, v_sb, qb)
    o_dsa_s = _dsa_attention(sr["dqb"], sr["iqb"], sr["iw"], k_ds, vt_ds, k_ix, tq=BLK, n_q=1, q_row0=0, q_pos0=past,
                             last_rows=-(-T // ROW_ALIGN) * ROW_ALIGN, chunk_off=0, n_valid=past + T, n_sel=min(TOPK_MAX, (past + T) // 4),
                             kv_heads=kv_heads)
    y_sample = _tail(x_sample, o_sb_s, o_dsa_s, sr["gsb"], sr["gdsa"], w, (0, 0, 0, 0))

    heads = lambda a, n: a.reshape(1, a.shape[0], a.shape[1], n, HEAD_DIM)
    return (y_prompt, y_sample,
            heads(pr["skf"], sb_heads), heads(pr["svf"], sb_heads),
            heads(pr["dkf"], kv_heads), heads(pr["dvf"], kv_heads), pr["ikf"][None],
            heads(sr["skf"], sb_heads), heads(sr["svf"], sb_heads),
            heads(sr["dkf"], kv_heads), heads(sr["dvf"], kv_heads), sr["ikf"][None])
```
